```python
import jax, jax.numpy as jnp
from jax import lax
import numpy as np

D_MODEL = 1024
BATCH = 4
SEQ = 4096
DEPTH = 1

FFN_HIDDEN = 2816
FFN_RESIDUAL_WEIGHT = 0.5

MLA_HEADS = 8
MLA_NOPE_DIM = 64
MLA_ROPE_DIM = 32
MLA_V_DIM = 64
Q_LORA_RANK = 192
KV_LORA_RANK = 128
ROPE_THETA = 10000.0
MAX_POS_OFFSET = 1024

FOX_HEADS = 8
FOX_HEAD_DIM = 64
FOX_WIDTH = FOX_HEADS * FOX_HEAD_DIM
FOX_BF_MIN = 1.0
FOX_BF_MAX = 6.0

MLA_WIDTH = MLA_HEADS * MLA_V_DIM
Q_BLOCK = 128
RMS_EPS = 1e-6

IN_SPLIT_SIZES = (Q_LORA_RANK, KV_LORA_RANK, MLA_ROPE_DIM,
                  FOX_WIDTH, FOX_WIDTH, FOX_WIDTH, FOX_HEADS,
                  D_MODEL, D_MODEL)
IN_WIDTH = sum(IN_SPLIT_SIZES)

kernel_name = "hybrid_mla_fox_macaron_gated"


def rms_norm(x, gain):
    xf = x.astype(jnp.float32)
    y = xf * lax.rsqrt(jnp.mean(xf * xf, axis=-1, keepdims=True) + RMS_EPS)
    return (y * gain.astype(jnp.float32)).astype(x.dtype)


def swiglu(x, w_gate, w_up, w_down):
    return (jax.nn.silu(x @ w_gate) * (x @ w_up)) @ w_down


def rope_tables(positions):
    half = MLA_ROPE_DIM // 2
    inv_freq = ROPE_THETA ** (-jnp.arange(half, dtype=jnp.float32) / half)
    ang = positions.astype(jnp.float32)[..., None] * inv_freq
    return jnp.cos(ang), jnp.sin(ang)


def apply_rotary(x, cos, sin):
    half = x.shape[-1] // 2
    xf = x.astype(jnp.float32)
    x1, x2 = xf[..., :half], xf[..., half:]
    return jnp.concatenate([x1 * cos - x2 * sin, x2 * cos + x1 * sin], axis=-1).astype(x.dtype)


def to_heads(t):
    return t.transpose(0, 2, 1, 3)


def from_heads(t):
    b, h, s, d = t.shape
    return t.transpose(0, 2, 1, 3).reshape(b, s, h * d)


def blocked_causal_attention(q, k, v, log_decay_cum=None):
    b, h, s, dk = q.shape
    dv = v.shape[-1]
    n_blocks = s // Q_BLOCK
    scale = dk ** -0.5
    k_pos = jnp.arange(s)

    def one_block(i):
        start = i * Q_BLOCK
        q_blk = lax.dynamic_slice_in_dim(q, start, Q_BLOCK, axis=2)
        logits = jnp.einsum('bhqd,bhkd->bhqk', q_blk, k,
                            preferred_element_type=jnp.float32) * scale
        if log_decay_cum is not None:
            c_q = lax.dynamic_slice_in_dim(log_decay_cum, start, Q_BLOCK, axis=2)
            logits = logits + (c_q[..., :, None] - log_decay_cum[..., None, :])
        q_pos = start + jnp.arange(Q_BLOCK)
        logits = jnp.where(k_pos[None, :] <= q_pos[:, None], logits, -jnp.inf)
        probs = jax.nn.softmax(logits, axis=-1)
        return jnp.einsum('bhqk,bhkd->bhqd', probs.astype(v.dtype), v)

    out = lax.map(one_block, jnp.arange(n_blocks))
    return out.transpose(1, 2, 0, 3, 4).reshape(b, h, s, dv)


def mla_mixer(q_lat, kv_lat, k_rope, cos, sin, q_lat_norm, w_qb, kv_lat_norm, w_kvb,
              q_nope_gain, q_rope_gain, k_nope_gain, k_rope_gain):
    b, s, _ = q_lat.shape
    q = (rms_norm(q_lat, q_lat_norm) @ w_qb).reshape(b, s, MLA_HEADS, MLA_NOPE_DIM + MLA_ROPE_DIM)
    q_nope = rms_norm(q[..., :MLA_NOPE_DIM], q_nope_gain)
    q_rope = apply_rotary(rms_norm(q[..., MLA_NOPE_DIM:], q_rope_gain),
                          cos[:, :, None, :], sin[:, :, None, :])
    kv = (rms_norm(kv_lat, kv_lat_norm) @ w_kvb).reshape(b, s, MLA_HEADS, MLA_NOPE_DIM + MLA_V_DIM)
    k_nope = rms_norm(kv[..., :MLA_NOPE_DIM], k_nope_gain)
    v = kv[..., MLA_NOPE_DIM:]
    k_r = apply_rotary(rms_norm(k_rope, k_rope_gain), cos, sin)
    k_r = jnp.broadcast_to(k_r[:, :, None, :], (b, s, MLA_HEADS, MLA_ROPE_DIM))
    q_full = jnp.concatenate([q_nope, q_rope], axis=-1)
    k_full = jnp.concatenate([k_nope, k_r], axis=-1)
    out = blocked_causal_attention(to_heads(q_full), to_heads(k_full), to_heads(v))
    return from_heads(out)


def fox_mixer(fq, fk, fv, f_logit, q_gain, k_gain, b_f):
    b, s, _ = fq.shape
    q = rms_norm(fq.reshape(b, s, FOX_HEADS, FOX_HEAD_DIM), q_gain)
    k = rms_norm(fk.reshape(b, s, FOX_HEADS, FOX_HEAD_DIM), k_gain)
    v = fv.reshape(b, s, FOX_HEADS, FOX_HEAD_DIM)
    log_f = jax.nn.log_sigmoid((f_logit + b_f).astype(jnp.float32))
    c = jnp.cumsum(log_f, axis=1).transpose(0, 2, 1)
    out = blocked_causal_attention(to_heads(q), to_heads(k), to_heads(v), c)
    return from_heads(out)


def setup_inputs(seed: int = 0) -> dict:
    key = jax.random.key(seed)
    ks = iter(jax.random.split(key, 40))
    f32 = jnp.float32

    def w(shape, fan_in):
        return jax.random.normal(next(ks), (DEPTH,) + shape, f32) * (fan_in ** -0.5)

    def gain(*shape):
        return 1.0 + 0.02 * jax.random.normal(next(ks), (DEPTH,) + shape, f32)

    x = jax.random.normal(next(ks), (BATCH, SEQ, D_MODEL), f32)
    offsets = jax.random.randint(next(ks), (BATCH, 1), 0, MAX_POS_OFFSET, dtype=jnp.int32)
    positions = (offsets + jnp.arange(SEQ, dtype=jnp.int32)[None, :]).astype(jnp.int32)
    return {
        "x": x,
        "positions": positions,
        "ffn1_norm": gain(D_MODEL),
        "ffn1_w_gate": w((D_MODEL, FFN_HIDDEN), D_MODEL),
        "ffn1_w_up": w((D_MODEL, FFN_HIDDEN), D_MODEL),
        "ffn1_w_down": w((FFN_HIDDEN, D_MODEL), FFN_HIDDEN),
        "mix_norm": gain(D_MODEL),
        "w_in": w((D_MODEL, IN_WIDTH), D_MODEL),
        "mla_q_lat_norm": gain(Q_LORA_RANK),
        "mla_w_qb": w((Q_LORA_RANK, MLA_HEADS * (MLA_NOPE_DIM + MLA_ROPE_DIM)), Q_LORA_RANK),
        "mla_kv_lat_norm": gain(KV_LORA_RANK),
        "mla_w_kvb": w((KV_LORA_RANK, MLA_HEADS * (MLA_NOPE_DIM + MLA_V_DIM)), KV_LORA_RANK),
        "mla_q_nope_gain": gain(MLA_NOPE_DIM),
        "mla_q_rope_gain": gain(MLA_ROPE_DIM),
        "mla_k_nope_gain": gain(MLA_NOPE_DIM),
        "mla_k_rope_gain": gain(MLA_ROPE_DIM),
        "fox_q_gain": gain(FOX_HEAD_DIM),
        "fox_k_gain": gain(FOX_HEAD_DIM),
        "fox_b_f": jax.random.uniform(next(ks), (DEPTH, FOX_HEADS), f32, FOX_BF_MIN, FOX_BF_MAX),
        "w_branch_mla": w((MLA_WIDTH, D_MODEL), MLA_WIDTH),
        "w_branch_fox": w((FOX_WIDTH, D_MODEL), FOX_WIDTH),
        "b_gate": 0.02 * jax.random.normal(next(ks), (DEPTH, 2, D_MODEL), f32),
        "w_o": w((D_MODEL, D_MODEL), D_MODEL),
        "ffn2_norm": gain(D_MODEL),
        "ffn2_w_gate": w((D_MODEL, FFN_HIDDEN), D_MODEL),
        "ffn2_w_up": w((D_MODEL, FFN_HIDDEN), D_MODEL),
        "ffn2_w_down": w((FFN_HIDDEN, D_MODEL), FFN_HIDDEN),
    }


def reference(x, positions, ffn1_norm, ffn1_w_gate, ffn1_w_up, ffn1_w_down, mix_norm, w_in,
              mla_q_lat_norm, mla_w_qb, mla_kv_lat_norm, mla_w_kvb, mla_q_nope_gain,
              mla_q_rope_gain, mla_k_nope_gain, mla_k_rope_gain, fox_q_gain, fox_k_gain,
              fox_b_f, w_branch_mla, w_branch_fox, b_gate, w_o, ffn2_norm, ffn2_w_gate,
              ffn2_w_up, ffn2_w_down):
    split_points = [int(p) for p in np.cumsum(IN_SPLIT_SIZES)[:-1]]
    cos, sin = rope_tables(positions)
    for l in range(DEPTH):
        x = x + FFN_RESIDUAL_WEIGHT * swiglu(rms_norm(x, ffn1_norm[l]),
                                             ffn1_w_gate[l], ffn1_w_up[l], ffn1_w_down[l])
        h = rms_norm(x, mix_norm[l])
        proj = h @ w_in[l]
        (q_lat, kv_lat, k_rope, fq, fk, fv, f_logit,
         g_mla, g_fox) = jnp.split(proj, split_points, axis=-1)
        y_mla = mla_mixer(q_lat, kv_lat, k_rope, cos, sin, mla_q_lat_norm[l], mla_w_qb[l],
                          mla_kv_lat_norm[l], mla_w_kvb[l], mla_q_nope_gain[l],
                          mla_q_rope_gain[l], mla_k_nope_gain[l], mla_k_rope_gain[l])
        y_fox = fox_mixer(fq, fk, fv, f_logit, fox_q_gain[l], fox_k_gain[l], fox_b_f[l])
        mixed = (jax.nn.sigmoid(g_mla + b_gate[l, 0]) * (y_mla @ w_branch_mla[l])
                 + jax.nn.sigmoid(g_fox + b_gate[l, 1]) * (y_fox @ w_branch_fox[l]))
        x = x + mixed @ w_o[l]
        x = x + FFN_RESIDUAL_WEIGHT * swiglu(rms_norm(x, ffn2_norm[l]),
                                             ffn2_w_gate[l], ffn2_w_up[l], ffn2_w_down[l])
    return x
```

```python
import functools

import numpy as np
import jax
import jax.numpy as jnp
from jax import lax
from jax.experimental import pallas as pl
from jax.experimental.pallas import tpu as pltpu

D_MODEL = 1024
FFN_HIDDEN = 2816
FFN_RESIDUAL_WEIGHT = 0.5
HEADS = 8
MLA_NOPE_DIM = 64
MLA_ROPE_DIM = 32
MLA_V_DIM = 64
Q_LORA_RANK = 192
KV_LORA_RANK = 128
ROPE_THETA = 10000.0
FOX_HEAD_DIM = 64
RMS_EPS = 1e-6

LANES = 128
HEAD_PAD = LANES
HP_WIDTH = HEADS * HEAD_PAD
V_WIDTH = HEADS * MLA_V_DIM
Q_LAT_PAD = 256
ROPE_HALF = MLA_ROPE_DIM // 2
ROPE_LANE0 = MLA_NOPE_DIM
BIAS_LANE0 = FOX_HEAD_DIM

TOKEN_TILE = 512
HIDDEN_CHUNK = 256
ATTN_Q_TILE = 512
ATTN_KV_TILE = 512
MASK_VALUE = -1e30
VMEM_LIMIT = 56 * 1024 * 1024

PROJ_QLAT = 0
PROJ_KVLAT = PROJ_QLAT + Q_LAT_PAD
PROJ_KR = PROJ_KVLAT + KV_LORA_RANK
PROJ_FQ = PROJ_KR + LANES
PROJ_FK = PROJ_FQ + HP_WIDTH
PROJ_FV = PROJ_FK + HP_WIDTH
PROJ_GM = PROJ_FV + V_WIDTH
PROJ_GF = PROJ_GM + D_MODEL
PROJ_WIDTH = PROJ_GF + D_MODEL


def _bf16(x):
    return x.astype(jnp.bfloat16)


def _dot(a, b):
    return jnp.dot(a, b, preferred_element_type=jnp.float32)


def _rms_scale(x, n):
    return lax.rsqrt(jnp.sum(x * x, axis=-1, keepdims=True) * (1.0 / n) + RMS_EPS)


def _split2(x):
    hi = _bf16(x)
    lo = _bf16(x - hi.astype(jnp.float32))
    return hi, lo


def _split3(x):
    hi = _bf16(x)
    r = x - hi.astype(jnp.float32)
    mid = _bf16(r)
    lo = _bf16(r - mid.astype(jnp.float32))
    return hi, mid, lo


def _segment_sumsq(x, seg_ref):
    seg = seg_ref[...]
    outs = []
    for c in range(x.shape[1] // 256):
        xs = x[:, c * 256:(c + 1) * 256]
        hi, lo = _split2(xs * xs)
        outs.append(_dot(hi, seg) + _dot(lo, seg))
    return jnp.concatenate(outs, axis=1)


def _swiglu_residual(x, gain, wg_ref, wu_ref, wd_ref, t_ref):
    h = _bf16(x * _rms_scale(x, D_MODEL) * gain)
    for c in range(FFN_HIDDEN // HIDDEN_CHUNK):
        cols = slice(c * HIDDEN_CHUNK, (c + 1) * HIDDEN_CHUNK)
        a = _dot(h, wg_ref[:, cols])
        b = _dot(h, wu_ref[:, cols])
        t_ref[:, cols] = _bf16(a * jax.nn.sigmoid(a) * b)
    return x + FFN_RESIDUAL_WEIGHT * _dot(t_ref[...], wd_ref[...])


def _ffn_kernel(x_ref, gain_ref, wg_ref, wu_ref, wd_ref, o_ref, t_ref):
    o_ref[...] = _swiglu_residual(x_ref[...], gain_ref[...], wg_ref, wu_ref, wd_ref, t_ref)


def _const_spec(shape):
    return pl.BlockSpec(shape, lambda *_: (0,) * len(shape), pipeline_mode=pl.Buffered(1))


def _ffn_call(x, gain, wg, wu, wd):
    n = x.shape[0]
    return pl.pallas_call(
        _ffn_kernel,
        grid=(n // TOKEN_TILE,),
        in_specs=[
            pl.BlockSpec((TOKEN_TILE, D_MODEL), lambda i: (i, 0)),
            _const_spec((1, D_MODEL)),
            _const_spec((D_MODEL, FFN_HIDDEN)),
            _const_spec((D_MODEL, FFN_HIDDEN)),
            _const_spec((FFN_HIDDEN, D_MODEL)),
        ],
        out_specs=pl.BlockSpec((TOKEN_TILE, D_MODEL), lambda i: (i, 0)),
        out_shape=jax.ShapeDtypeStruct((n, D_MODEL), jnp.float32),
        scratch_shapes=[pltpu.VMEM((TOKEN_TILE, FFN_HIDDEN), jnp.bfloat16)],
        compiler_params=pltpu.CompilerParams(
            dimension_semantics=("arbitrary",), vmem_limit_bytes=VMEM_LIMIT),
        name="ffn1",
    )(x, gain, wg, wu, wd)


def _proj_kernel(x_ref, pos_ref, gain_ref, win_ref, wqb_ref, wkvb_ref, tri_ref,
                 segq_ref, segk_ref, eq_ref, ek_ref,
                 qlat_gain_ref, kvlat_gain_ref, q_gain_ref, q_icnt_ref, kn_gain_ref,
                 kr_gain_ref, fq_gain_ref, fk_gain_ref, bf_ref, freq_ref, oneq_ref, onek_ref,
                 qm_ref, km_ref, vm_ref, qf_ref, kf_ref, vf_ref, gm_ref, gf_ref,
                 carry_ref):
    x = x_ref[0]
    h = _bf16(x * _rms_scale(x, D_MODEL) * gain_ref[...])

    def proj(lo, width):
        return _dot(h, win_ref[:, lo:lo + width])

    lane = lax.broadcasted_iota(jnp.int32, (1, LANES), 1)
    rope_lanes = (lane >= ROPE_LANE0) & (lane < ROPE_LANE0 + MLA_ROPE_DIM)

    ang = pos_ref[0].astype(jnp.float32) * freq_ref[...]
    cos_t = jnp.cos(ang)
    sin_t = jnp.sin(ang)
    sin_hi = jnp.where(lane >= ROPE_LANE0 + ROPE_HALF, sin_t, 0.0)
    sin_lo = jnp.where(lane < ROPE_LANE0 + ROPE_HALF, -sin_t, 0.0)

    def rotary(t):
        return (t * cos_t + pltpu.roll(t, ROPE_HALF, 1) * sin_hi
                + pltpu.roll(t, LANES - ROPE_HALF, 1) * sin_lo)

    q_lat = proj(PROJ_QLAT, Q_LAT_PAD)
    q_lat = _bf16(q_lat * _rms_scale(q_lat, Q_LORA_RANK) * qlat_gain_ref[...])
    q = _dot(q_lat, wqb_ref[...])
    q = q * lax.rsqrt(_segment_sumsq(q, segq_ref) * q_icnt_ref[...] + RMS_EPS) * q_gain_ref[...]
    for hd in range(HEADS):
        cols = slice(hd * HEAD_PAD, (hd + 1) * HEAD_PAD)
        qm_ref[0, :, cols] = _bf16(rotary(q[:, cols]))

    kv_lat = proj(PROJ_KVLAT, KV_LORA_RANK)
    kv_lat = _bf16(kv_lat * _rms_scale(kv_lat, KV_LORA_RANK) * kvlat_gain_ref[...])
    kv = _dot(kv_lat, wkvb_ref[...])
    k_nope = kv[:, :HP_WIDTH]
    k_nope = (k_nope * lax.rsqrt(_segment_sumsq(k_nope, segk_ref) * (1.0 / MLA_NOPE_DIM) + RMS_EPS)
              * kn_gain_ref[...])
    vm_ref[0] = _bf16(kv[:, HP_WIDTH:])

    kr_grp = proj(PROJ_KR, LANES)
    kr = jnp.where(rope_lanes, kr_grp, 0.0)
    kr = rotary(kr * _rms_scale(kr, MLA_ROPE_DIM) * kr_gain_ref[...])
    for hd in range(HEADS):
        cols = slice(hd * HEAD_PAD, (hd + 1) * HEAD_PAD)
        km_ref[0, :, cols] = _bf16(k_nope[:, cols] + kr)

    @pl.when(pl.program_id(1) == 0)
    def _():
        carry_ref[...] = jnp.zeros_like(carry_ref)

    log_f = jnp.where(lane < HEADS, jax.nn.log_sigmoid(kr_grp + bf_ref[...]), 0.0)
    tri = tri_ref[...]
    f_hi, f_mid, f_lo = _split3(log_f)
    c = _dot(tri, f_hi) + _dot(tri, f_mid) + _dot(tri, f_lo) + carry_ref[0:1, :]
    carry_ref[0:1, :] = c[TOKEN_TILE - 1:TOKEN_TILE, :]
    c_rep = c + pltpu.roll(c, HEADS, 1) + pltpu.roll(c, 2 * HEADS, 1)
    c_hi, c_mid, c_lo = _split3(c_rep)
    c_parts = jnp.where(lane < HEADS, c_hi, jnp.where(lane < 2 * HEADS, c_mid, c_lo))

    fq = proj(PROJ_FQ, HP_WIDTH)
    fq = fq * lax.rsqrt(_segment_sumsq(fq, segk_ref) * (1.0 / FOX_HEAD_DIM) + RMS_EPS) * fq_gain_ref[...]
    qf_ref[0] = _bf16(fq + _dot(c_parts, eq_ref[...]) + oneq_ref[...])
    fk = proj(PROJ_FK, HP_WIDTH)
    fk = fk * lax.rsqrt(_segment_sumsq(fk, segk_ref) * (1.0 / FOX_HEAD_DIM) + RMS_EPS) * fk_gain_ref[...]
    kf_ref[0] = _bf16(fk + _dot(c_parts, ek_ref[...]) + onek_ref[...])
    vf_ref[0] = _bf16(proj(PROJ_FV, V_WIDTH))

    gm_ref[0] = proj(PROJ_GM, D_MODEL)
    gf_ref[0] = proj(PROJ_GF, D_MODEL)


def _proj_call(x, pos, consts):
    b, s, _ = x.shape
    tile = lambda w: pl.BlockSpec((1, TOKEN_TILE, w), lambda bi, i: (bi, i, 0))
    out_widths = (HP_WIDTH, HP_WIDTH, V_WIDTH, HP_WIDTH, HP_WIDTH, V_WIDTH, D_MODEL, D_MODEL)
    out_dtypes = (jnp.bfloat16,) * 6 + (jnp.float32,) * 2
    return pl.pallas_call(
        _proj_kernel,
        grid=(b, s // TOKEN_TILE),
        in_specs=[tile(D_MODEL), tile(1)] + [_const_spec(c.shape) for c in consts],
        out_specs=[tile(w) for w in out_widths],
        out_shape=[jax.ShapeDtypeStruct((b, s, w), d) for w, d in zip(out_widths, out_dtypes)],
        scratch_shapes=[pltpu.VMEM((8, LANES), jnp.float32)],
        compiler_params=pltpu.CompilerParams(
            dimension_semantics=("arbitrary", "arbitrary"), vmem_limit_bytes=VMEM_LIMIT),
        name="proj",
    )(x, pos, *consts)


def _attn_kernel(q_ref, k_ref, v_ref, o_ref, m_ref, l_ref, acc_ref):
    qi = pl.program_id(1)
    tq, tk = ATTN_Q_TILE, ATTN_KV_TILE
    row = lax.broadcasted_iota(jnp.int32, (tq, tk), 0)
    col = lax.broadcasted_iota(jnp.int32, (tq, tk), 1)
    lane = lax.broadcasted_iota(jnp.int32, (tq, LANES), 1)
    nt = (((1,), (1,)), ((), ()))

    def lane_sums(p):
        return functools.reduce(
            lambda a, b: a + b, [p[:, c * LANES:(c + 1) * LANES] for c in range(tk // LANES)])

    for pair in range(HEADS // 2):
        vcols = slice(pair * LANES, (pair + 1) * LANES)
        outs = []
        for e in range(2):
            hcols = slice((2 * pair + e) * HEAD_PAD, (2 * pair + e + 1) * HEAD_PAD)
            q = q_ref[0, :, hcols]

            start = pl.multiple_of(qi * tq, tq)
            s = lax.dot_general(q, k_ref[0, pl.ds(start, tk), hcols], nt,
                                preferred_element_type=jnp.float32)
            s = jnp.where(col <= row, s, MASK_VALUE)
            m = jnp.max(s, axis=1, keepdims=True)
            p = jnp.exp(s - m)
            m_ref[...] = m
            l_ref[...] = lane_sums(p)
            acc_ref[...] = _dot(_bf16(p), v_ref[0, pl.ds(start, tk), vcols])

            def body(t, _):
                off = pl.multiple_of(t * tk, tk)
                s = lax.dot_general(q, k_ref[0, pl.ds(off, tk), hcols], nt,
                                    preferred_element_type=jnp.float32)
                m_prev = m_ref[...]
                m_new = jnp.maximum(m_prev, jnp.max(s, axis=1, keepdims=True))
                alpha = jnp.exp(m_prev - m_new)
                p = jnp.exp(s - m_new)
                m_ref[...] = m_new
                l_ref[...] = alpha * l_ref[...] + lane_sums(p)
                acc_ref[...] = alpha * acc_ref[...] + _dot(_bf16(p), v_ref[0, pl.ds(off, tk), vcols])
                return 0

            lax.fori_loop(0, qi, body, 0)
            outs.append(acc_ref[...] / jnp.sum(l_ref[...], axis=1, keepdims=True))
        o_ref[0, :, vcols] = jnp.where(lane < MLA_V_DIM, outs[0], outs[1]).astype(o_ref.dtype)


def _attn_call(q, k, v, name):
    b, s, _ = q.shape
    return pl.pallas_call(
        _attn_kernel,
        grid=(b, s // ATTN_Q_TILE),
        in_specs=[
            pl.BlockSpec((1, ATTN_Q_TILE, HP_WIDTH), lambda bi, i: (bi, i, 0)),
            pl.BlockSpec((1, s, HP_WIDTH), lambda bi, i: (bi, 0, 0)),
            pl.BlockSpec((1, s, V_WIDTH), lambda bi, i: (bi, 0, 0)),
        ],
        out_specs=pl.BlockSpec((1, ATTN_Q_TILE, V_WIDTH), lambda bi, i: (bi, i, 0)),
        out_shape=jax.ShapeDtypeStruct((b, s, V_WIDTH), jnp.bfloat16),
        scratch_shapes=[
            pltpu.VMEM((ATTN_Q_TILE, 1), jnp.float32),
            pltpu.VMEM((ATTN_Q_TILE, LANES), jnp.float32),
            pltpu.VMEM((ATTN_Q_TILE, LANES), jnp.float32),
        ],
        compiler_params=pltpu.CompilerParams(
            dimension_semantics=("arbitrary", "arbitrary"), vmem_limit_bytes=VMEM_LIMIT),
        name=name,
    )(q, k, v)


def _merge_kernel(x_ref, ym_ref, yf_ref, gm_ref, gf_ref, bg_ref, wa_ref, wb_ref, wo_ref,
                  gain_ref, wg_ref, wu_ref, wd_ref, o_ref, t_ref):
    mixed = (jax.nn.sigmoid(gm_ref[...] + bg_ref[0:1, :]) * _dot(ym_ref[...], wa_ref[...])
             + jax.nn.sigmoid(gf_ref[...] + bg_ref[1:2, :]) * _dot(yf_ref[...], wb_ref[...]))
    x = x_ref[...] + _dot(_bf16(mixed), wo_ref[...])
    o_ref[...] = _swiglu_residual(x, gain_ref[...], wg_ref, wu_ref, wd_ref, t_ref)


def _merge_call(x, ym, yf, gm, gf, bg, wa, wb, wo, gain, wg, wu, wd):
    n = x.shape[0]
    tile = lambda w: pl.BlockSpec((TOKEN_TILE, w), lambda i: (i, 0))
    consts = (bg, wa, wb, wo, gain, wg, wu, wd)
    return pl.pallas_call(
        _merge_kernel,
        grid=(n // TOKEN_TILE,),
        in_specs=[tile(D_MODEL), tile(V_WIDTH), tile(V_WIDTH), tile(D_MODEL), tile(D_MODEL)]
        + [_const_spec(c.shape) for c in consts],
        out_specs=tile(D_MODEL),
        out_shape=jax.ShapeDtypeStruct((n, D_MODEL), jnp.float32),
        scratch_shapes=[pltpu.VMEM((TOKEN_TILE, FFN_HIDDEN), jnp.bfloat16)],
        compiler_params=pltpu.CompilerParams(
            dimension_semantics=("arbitrary",), vmem_limit_bytes=VMEM_LIMIT),
        name="merge_ffn2",
    )(x, ym, yf, gm, gf, *consts)


def _take_cols(w, idx):
    idx = np.asarray(idx)
    return jnp.where(idx[None, :] >= 0, w[:, np.maximum(idx, 0)], 0.0)


def _head_padded(offset, dim):
    idx = np.full((HEADS, HEAD_PAD), -1, np.int64)
    idx[:, :dim] = offset + np.arange(HEADS)[:, None] * dim + np.arange(dim)[None, :]
    return idx.reshape(-1)


def _head_lanes(vec, lane0=0):
    tile = jnp.zeros((HEAD_PAD,), jnp.float32).at[lane0:lane0 + vec.shape[0]].set(vec)
    return jnp.tile(tile, HEADS)[None, :]


def _proj_constants(mix_norm, w_in, q_lat_norm, w_qb, kv_lat_norm, w_kvb, q_nope_gain, q_rope_gain,
                    k_nope_gain, k_rope_gain, fox_q_gain, fox_k_gain, fox_b_f):
    f32 = jnp.float32
    o_qlat, o_kvlat = 0, Q_LORA_RANK
    o_kr = o_kvlat + KV_LORA_RANK
    o_fq = o_kr + MLA_ROPE_DIM
    o_fk = o_fq + V_WIDTH
    o_fv = o_fk + V_WIDTH
    o_fl = o_fv + V_WIDTH
    o_gm = o_fl + HEADS
    o_gf = o_gm + D_MODEL

    kr_grp = np.full((LANES,), -1, np.int64)
    kr_grp[:HEADS] = o_fl + np.arange(HEADS)
    kr_grp[ROPE_LANE0:ROPE_LANE0 + MLA_ROPE_DIM] = o_kr + np.arange(MLA_ROPE_DIM)
    cols = np.concatenate([
        np.arange(Q_LORA_RANK), np.full((Q_LAT_PAD - Q_LORA_RANK,), -1),
        o_kvlat + np.arange(KV_LORA_RANK),
        kr_grp,
        _head_padded(o_fq, FOX_HEAD_DIM),
        _head_padded(o_fk, FOX_HEAD_DIM),
        o_fv + np.arange(V_WIDTH),
        o_gm + np.arange(D_MODEL),
        o_gf + np.arange(D_MODEL),
    ])
    assert cols.shape[0] == PROJ_WIDTH
    win = _bf16(_take_cols(w_in, cols))

    qk_dim = MLA_NOPE_DIM + MLA_ROPE_DIM
    wqb = _take_cols(w_qb, _head_padded(0, qk_dim))
    wqb = _bf16(jnp.pad(wqb, ((0, Q_LAT_PAD - Q_LORA_RANK), (0, 0))))
    kv_dim = MLA_NOPE_DIM + MLA_V_DIM
    k_idx = np.full((HEADS, HEAD_PAD), -1, np.int64)
    k_idx[:, :MLA_NOPE_DIM] = np.arange(HEADS)[:, None] * kv_dim + np.arange(MLA_NOPE_DIM)[None, :]
    v_idx = (np.arange(HEADS)[:, None] * kv_dim + MLA_NOPE_DIM + np.arange(MLA_V_DIM)[None, :])
    wkvb = _bf16(_take_cols(w_kvb, np.concatenate([k_idx.reshape(-1), v_idx.reshape(-1)])))

    r = np.arange(TOKEN_TILE)
    tri = jnp.asarray(r[:, None] >= r[None, :], jnp.bfloat16)
    l256 = np.arange(256)
    same_head = (l256[:, None] // HEAD_PAD) == (l256[None, :] // HEAD_PAD)
    in_nope = (l256 % HEAD_PAD) < MLA_NOPE_DIM
    in_rope = ((l256 % HEAD_PAD) >= ROPE_LANE0) & ((l256 % HEAD_PAD) < ROPE_LANE0 + MLA_ROPE_DIM)
    seg_k = same_head & in_nope[:, None] & in_nope[None, :]
    seg_q = seg_k | (same_head & in_rope[:, None] & in_rope[None, :])
    seg_q = jnp.asarray(seg_q, jnp.bfloat16)
    seg_k = jnp.asarray(seg_k, jnp.bfloat16)
    eq = np.zeros((LANES, HP_WIDTH), np.float32)
    ek = np.zeros((LANES, HP_WIDTH), np.float32)
    oneq = np.zeros((1, HP_WIDTH), np.float32)
    onek = np.zeros((1, HP_WIDTH), np.float32)
    for hd in range(HEADS):
        for piece in range(3):
            eq[piece * HEADS + hd, hd * HEAD_PAD + BIAS_LANE0 + piece] = 1.0
            ek[piece * HEADS + hd, hd * HEAD_PAD + BIAS_LANE0 + 3 + piece] = -1.0
            oneq[0, hd * HEAD_PAD + BIAS_LANE0 + 3 + piece] = 1.0
            onek[0, hd * HEAD_PAD + BIAS_LANE0 + piece] = 1.0
    eq = jnp.asarray(eq, jnp.bfloat16)
    ek = jnp.asarray(ek, jnp.bfloat16)

    mla_scale = float(qk_dim) ** -0.5
    fox_scale = float(FOX_HEAD_DIM) ** -0.5
    q_gain = _head_lanes(jnp.concatenate([q_nope_gain, q_rope_gain]) * mla_scale)
    icnt = np.ones((HEAD_PAD,), np.float32)
    icnt[:MLA_NOPE_DIM] = 1.0 / MLA_NOPE_DIM
    icnt[ROPE_LANE0:ROPE_LANE0 + MLA_ROPE_DIM] = 1.0 / MLA_ROPE_DIM
    q_icnt = jnp.asarray(np.tile(icnt, HEADS)[None, :])
    kn_gain = _head_lanes(k_nope_gain)
    kr_gain = jnp.zeros((1, LANES), f32).at[0, ROPE_LANE0:ROPE_LANE0 + MLA_ROPE_DIM].set(k_rope_gain)
    fq_gain = _head_lanes(fox_q_gain * fox_scale)
    fk_gain = _head_lanes(fox_k_gain)
    bf = jnp.zeros((1, LANES), f32).at[0, :HEADS].set(fox_b_f)
    inv_freq = ROPE_THETA ** (-jnp.arange(ROPE_HALF, dtype=f32) / ROPE_HALF)
    freq = jnp.zeros((1, LANES), f32).at[0, ROPE_LANE0:ROPE_LANE0 + MLA_ROPE_DIM].set(
        jnp.concatenate([inv_freq, inv_freq]))
    qlat_gain = jnp.pad(q_lat_norm, (0, Q_LAT_PAD - Q_LORA_RANK))[None, :]

    return (mix_norm[None, :], win, wqb, wkvb, tri, seg_q, seg_k, eq, ek,
            qlat_gain, kv_lat_norm[None, :], q_gain, q_icnt, kn_gain, kr_gain, fq_gain, fk_gain,
            bf, freq, jnp.asarray(oneq), jnp.asarray(onek))


def kernel(x, positions, ffn1_norm, ffn1_w_gate, ffn1_w_up, ffn1_w_down, mix_norm, w_in, mla_q_lat_norm, mla_w_qb, mla_kv_lat_norm, mla_w_kvb, mla_q_nope_gain, mla_q_rope_gain, mla_k_nope_gain, mla_k_rope_gain, fox_q_gain, fox_k_gain, fox_b_f, w_branch_mla, w_branch_fox, b_gate, w_o, ffn2_norm, ffn2_w_gate, ffn2_w_up, ffn2_w_down):
    b, s, d = x.shape
    n = b * s
    pos = positions.reshape(b, s, 1)
    xt = x.reshape(n, d)
    for l in range(ffn1_norm.shape[0]):
        xt = _ffn_call(xt, ffn1_norm[l][None, :], _bf16(ffn1_w_gate[l]), _bf16(ffn1_w_up[l]),
                       _bf16(ffn1_w_down[l]))
        consts = _proj_constants(
            mix_norm[l], w_in[l], mla_q_lat_norm[l], mla_w_qb[l], mla_kv_lat_norm[l], mla_w_kvb[l],
            mla_q_nope_gain[l], mla_q_rope_gain[l], mla_k_nope_gain[l], mla_k_rope_gain[l],
            fox_q_gain[l], fox_k_gain[l], fox_b_f[l])
        qm, km, vm, qf, kf, vf, gm, gf = _proj_call(xt.reshape(b, s, d), pos, consts)
        ym = _attn_call(qm, km, vm, "attn_mla")
        yf = _attn_call(qf, kf, vf, "attn_fox")
        xt = _merge_call(
            xt, ym.reshape(n, V_WIDTH), yf.reshape(n, V_WIDTH), gm.reshape(n, d), gf.reshape(n, d),
            b_gate[l], _bf16(w_branch_mla[l]), _bf16(w_branch_fox[l]), _bf16(w_o[l]),
            ffn2_norm[l][None, :], _bf16(ffn2_w_gate[l]), _bf16(ffn2_w_up[l]), _bf16(ffn2_w_down[l]))
    return xt.reshape(b, s, d)
```

```python
import numpy as np
import jax
import jax.numpy as jnp
from jax import lax
from jax.experimental import pallas as pl
from jax.experimental.pallas import tpu as pltpu

D_MODEL = 1024
FFN_HIDDEN = 2816
FFN_RESIDUAL_WEIGHT = 0.5
HEADS = 8
MLA_NOPE_DIM = 64
MLA_ROPE_DIM = 32
MLA_V_DIM = 64
Q_LORA_RANK = 192
KV_LORA_RANK = 128
ROPE_THETA = 10000.0
FOX_HEAD_DIM = 64
RMS_EPS = 1e-6

LANES = 128
HEAD_PAD = LANES
HP_WIDTH = HEADS * HEAD_PAD
V_WIDTH = HEADS * MLA_V_DIM
Q_LAT_PAD = 256
ROPE_HALF = MLA_ROPE_DIM // 2
ROPE_LANE0 = MLA_NOPE_DIM
BIAS_LANE0 = FOX_HEAD_DIM

TOKEN_TILE = 512
HIDDEN_CHUNK = 256
ATTN_Q_TILE = 512
ATTN_KV_TILE = 512
MASK_VALUE = -1e30
VMEM_LIMIT = 56 * 1024 * 1024

PROJ_QLAT = 0
PROJ_KVLAT = PROJ_QLAT + Q_LAT_PAD
PROJ_KR = PROJ_KVLAT + KV_LORA_RANK
PROJ_FQ = PROJ_KR + LANES
PROJ_FK = PROJ_FQ + HP_WIDTH
PROJ_GM = PROJ_FK + HP_WIDTH
PROJ_GF = PROJ_GM + D_MODEL
PROJ_WIDTH = PROJ_GF + D_MODEL

VT_ROWS = 80
VT_ONES_ROW = MLA_V_DIM
VT_WIDTH = HEADS * VT_ROWS
NT_DIMS = (((1,), (1,)), ((), ()))
LOG2E = 1.4426950408889634


def _bf16(x):
    return x.astype(jnp.bfloat16)


def _dot(a, b):
    return jnp.dot(a, b, preferred_element_type=jnp.float32)


def _rms_scale(x, n):
    return lax.rsqrt(jnp.sum(x * x, axis=-1, keepdims=True) * (1.0 / n) + RMS_EPS)


def _split2(x):
    hi = _bf16(x)
    lo = _bf16(x - hi.astype(jnp.float32))
    return hi, lo


def _split3(x):
    hi = _bf16(x)
    r = x - hi.astype(jnp.float32)
    mid = _bf16(r)
    lo = _bf16(r - mid.astype(jnp.float32))
    return hi, mid, lo


def _segment_sumsq(x, seg_ref):
    seg = seg_ref[...]
    outs = []
    for c in range(x.shape[1] // 256):
        xs = x[:, c * 256:(c + 1) * 256]
        hi, lo = _split2(xs * xs)
        outs.append(_dot(hi, seg) + _dot(lo, seg))
    return jnp.concatenate(outs, axis=1)


def _swiglu_residual(x, gain, wg_ref, wu_ref, wd_ref, t_ref):
    h = _bf16(x * _rms_scale(x, D_MODEL) * gain)
    for c in range(FFN_HIDDEN // HIDDEN_CHUNK):
        cols = slice(c * HIDDEN_CHUNK, (c + 1) * HIDDEN_CHUNK)
        a = _dot(h, wg_ref[:, cols])
        b = _dot(h, wu_ref[:, cols])
        t_ref[:, cols] = _bf16(a * jax.nn.sigmoid(a) * b)
    return x + FFN_RESIDUAL_WEIGHT * _dot(t_ref[...], wd_ref[...])


def _ffn_kernel(x_ref, gain_ref, wg_ref, wu_ref, wd_ref, o_ref, t_ref):
    o_ref[...] = _swiglu_residual(x_ref[...], gain_ref[...], wg_ref, wu_ref, wd_ref, t_ref)


def _const_spec(shape):
    return pl.BlockSpec(shape, lambda *_: (0,) * len(shape), pipeline_mode=pl.Buffered(1))


def _ffn_call(x, gain, wg, wu, wd):
    n = x.shape[0]
    return pl.pallas_call(
        _ffn_kernel,
        grid=(n // TOKEN_TILE,),
        in_specs=[
            pl.BlockSpec((TOKEN_TILE, D_MODEL), lambda i: (i, 0)),
            _const_spec((1, D_MODEL)),
            _const_spec((D_MODEL, FFN_HIDDEN)),
            _const_spec((D_MODEL, FFN_HIDDEN)),
            _const_spec((FFN_HIDDEN, D_MODEL)),
        ],
        out_specs=pl.BlockSpec((TOKEN_TILE, D_MODEL), lambda i: (i, 0)),
        out_shape=jax.ShapeDtypeStruct((n, D_MODEL), jnp.float32),
        scratch_shapes=[pltpu.VMEM((TOKEN_TILE, FFN_HIDDEN), jnp.bfloat16)],
        compiler_params=pltpu.CompilerParams(
            dimension_semantics=("arbitrary",), vmem_limit_bytes=VMEM_LIMIT),
        name="ffn1",
    )(x, gain, wg, wu, wd)


def _proj_kernel(x_ref, pos_ref, gain_ref, win_ref, wqb_ref, wkvb_ref, wvt_ref, wfvt_ref, vones_ref,
                 tri_ref, segq_ref, segk_ref, eq_ref, ek_ref,
                 qlat_gain_ref, kvlat_gain_ref, q_gain_ref, q_icnt_ref, kn_gain_ref,
                 kr_gain_ref, fq_gain_ref, fk_gain_ref, bf_ref, freq_ref, oneq_ref, onek_ref,
                 qm_ref, km_ref, vm_ref, qf_ref, kf_ref, vf_ref, gm_ref, gf_ref,
                 carry_ref):
    x = x_ref[0]
    h = _bf16(x * _rms_scale(x, D_MODEL) * gain_ref[...])

    def proj(lo, width):
        return _dot(h, win_ref[:, lo:lo + width])

    lane = lax.broadcasted_iota(jnp.int32, (1, LANES), 1)
    rope_lanes = (lane >= ROPE_LANE0) & (lane < ROPE_LANE0 + MLA_ROPE_DIM)

    ang = pos_ref[0].astype(jnp.float32) * freq_ref[...]
    cos_t = jnp.cos(ang)
    sin_t = jnp.sin(ang)
    sin_hi = jnp.where(lane >= ROPE_LANE0 + ROPE_HALF, sin_t, 0.0)
    sin_lo = jnp.where(lane < ROPE_LANE0 + ROPE_HALF, -sin_t, 0.0)

    def rotary(t):
        return (t * cos_t + pltpu.roll(t, ROPE_HALF, 1) * sin_hi
                + pltpu.roll(t, LANES - ROPE_HALF, 1) * sin_lo)

    def store_values_t(ref, wt_ref, act):
        vt = lax.dot_general(wt_ref[...], act, NT_DIMS, preferred_element_type=jnp.float32)
        ones = jnp.concatenate([vones_ref[...]] * (TOKEN_TILE // LANES), axis=1)
        ref[0, 0] = _bf16(vt + ones)

    q_lat = proj(PROJ_QLAT, Q_LAT_PAD)
    q_lat = _bf16(q_lat * _rms_scale(q_lat, Q_LORA_RANK) * qlat_gain_ref[...])
    q = _dot(q_lat, wqb_ref[...])
    q = q * lax.rsqrt(_segment_sumsq(q, segq_ref) * q_icnt_ref[...] + RMS_EPS) * q_gain_ref[...]
    for hd in range(HEADS):
        cols = slice(hd * HEAD_PAD, (hd + 1) * HEAD_PAD)
        qm_ref[0, :, cols] = _bf16(rotary(q[:, cols]))

    kv_lat = proj(PROJ_KVLAT, KV_LORA_RANK)
    kv_lat = _bf16(kv_lat * _rms_scale(kv_lat, KV_LORA_RANK) * kvlat_gain_ref[...])
    k_nope = _dot(kv_lat, wkvb_ref[...])
    k_nope = (k_nope * lax.rsqrt(_segment_sumsq(k_nope, segk_ref) * (1.0 / MLA_NOPE_DIM) + RMS_EPS)
              * kn_gain_ref[...])
    store_values_t(vm_ref, wvt_ref, kv_lat)

    kr_grp = proj(PROJ_KR, LANES)
    kr = jnp.where(rope_lanes, kr_grp, 0.0)
    kr = rotary(kr * _rms_scale(kr, MLA_ROPE_DIM) * kr_gain_ref[...])
    for hd in range(HEADS):
        cols = slice(hd * HEAD_PAD, (hd + 1) * HEAD_PAD)
        km_ref[0, :, cols] = _bf16(k_nope[:, cols] + kr)

    @pl.when(pl.program_id(1) == 0)
    def _():
        carry_ref[...] = jnp.zeros_like(carry_ref)

    log_f = jnp.where(lane < HEADS, jax.nn.log_sigmoid(kr_grp + bf_ref[...]), 0.0)
    tri = tri_ref[...]
    f_hi, f_mid, f_lo = _split3(log_f)
    c = _dot(tri, f_hi) + _dot(tri, f_mid) + _dot(tri, f_lo) + carry_ref[0:1, :]
    carry_ref[0:1, :] = c[TOKEN_TILE - 1:TOKEN_TILE, :]
    c2 = c * LOG2E
    c_rep = c2 + pltpu.roll(c2, HEADS, 1) + pltpu.roll(c2, 2 * HEADS, 1)
    c_hi, c_mid, c_lo = _split3(c_rep)
    c_parts = jnp.where(lane < HEADS, c_hi, jnp.where(lane < 2 * HEADS, c_mid, c_lo))

    fq = proj(PROJ_FQ, HP_WIDTH)
    fq = fq * lax.rsqrt(_segment_sumsq(fq, segk_ref) * (1.0 / FOX_HEAD_DIM) + RMS_EPS) * fq_gain_ref[...]
    qf_ref[0] = _bf16(fq + _dot(c_parts, eq_ref[...]) + oneq_ref[...])
    fk = proj(PROJ_FK, HP_WIDTH)
    fk = fk * lax.rsqrt(_segment_sumsq(fk, segk_ref) * (1.0 / FOX_HEAD_DIM) + RMS_EPS) * fk_gain_ref[...]
    kf_ref[0] = _bf16(fk + _dot(c_parts, ek_ref[...]) + onek_ref[...])
    store_values_t(vf_ref, wfvt_ref, h)

    gm_ref[0] = proj(PROJ_GM, D_MODEL)
    gf_ref[0] = proj(PROJ_GF, D_MODEL)


def _proj_call(x, pos, consts):
    b, s, _ = x.shape
    bf16, f32 = jnp.bfloat16, jnp.float32
    n_tiles = s // TOKEN_TILE
    tile = lambda w: pl.BlockSpec((1, TOKEN_TILE, w), lambda bi, i: (bi, i, 0))
    tile_t = pl.BlockSpec((1, 1, VT_WIDTH, TOKEN_TILE), lambda bi, i: (bi, i, 0, 0))
    rows = lambda w, d: jax.ShapeDtypeStruct((b, s, w), d)
    rows_t = jax.ShapeDtypeStruct((b, n_tiles, VT_WIDTH, TOKEN_TILE), bf16)
    return pl.pallas_call(
        _proj_kernel,
        grid=(b, n_tiles),
        in_specs=[tile(D_MODEL), tile(1)] + [_const_spec(c.shape) for c in consts],
        out_specs=[tile(HP_WIDTH), tile(HP_WIDTH), tile_t, tile(HP_WIDTH), tile(HP_WIDTH), tile_t,
                   tile(D_MODEL), tile(D_MODEL)],
        out_shape=[rows(HP_WIDTH, bf16), rows(HP_WIDTH, bf16), rows_t, rows(HP_WIDTH, bf16),
                   rows(HP_WIDTH, bf16), rows_t, rows(D_MODEL, f32), rows(D_MODEL, f32)],
        scratch_shapes=[pltpu.VMEM((8, LANES), jnp.float32)],
        compiler_params=pltpu.CompilerParams(
            dimension_semantics=("arbitrary", "arbitrary"), vmem_limit_bytes=VMEM_LIMIT),
        name="proj",
    )(x, pos, *consts)


def _attn_kernel(q_ref, k_ref, vt_ref, o_ref, m_ref, acc_ref):
    qi = pl.program_id(1)
    tq, tk = ATTN_Q_TILE, ATTN_KV_TILE

    def step(t, causal_mask):
        off = pl.multiple_of(t * tk, tk)

        def logits(hd):
            hcols = slice(hd * HEAD_PAD, (hd + 1) * HEAD_PAD)
            return lax.dot_general(k_ref[0, pl.ds(off, tk), hcols], q_ref[0, :, hcols], NT_DIMS,
                                   preferred_element_type=jnp.float32)

        s_next = logits(0)
        for hd in range(HEADS):
            s = s_next
            if hd + 1 < HEADS:
                s_next = logits(hd + 1)
            if causal_mask is not None:
                s = jnp.where(causal_mask, s, MASK_VALUE)
            m_prev = m_ref[hd]
            m_new = jnp.maximum(m_prev, jnp.max(s, axis=0, keepdims=True))
            alpha = jnp.exp2(m_prev - m_new)
            p = _bf16(jnp.exp2(s - m_new))
            vt = vt_ref[0, t, hd * VT_ROWS:(hd + 1) * VT_ROWS, :]
            acc_ref[hd] = alpha * acc_ref[hd] + _dot(vt, p)
            m_ref[hd] = m_new

    m_ref[...] = jnp.full(m_ref.shape, MASK_VALUE, jnp.float32)
    acc_ref[...] = jnp.zeros(acc_ref.shape, jnp.float32)
    key = lax.broadcasted_iota(jnp.int32, (tk, tq), 0)
    query = lax.broadcasted_iota(jnp.int32, (tk, tq), 1)
    step(qi, key <= query)

    def body(t, carry):
        step(t, None)
        return carry

    lax.fori_loop(0, qi, body, 0)

    for pair in range(HEADS // 2):
        y_t = []
        for hd in (2 * pair, 2 * pair + 1):
            acc = acc_ref[hd]
            y_t.append(acc[:MLA_V_DIM] / acc[VT_ONES_ROW:VT_ONES_ROW + 1])
        y = jnp.concatenate(y_t, axis=0).T
        o_ref[0, :, pair * LANES:(pair + 1) * LANES] = y.astype(o_ref.dtype)


def _attn_call(q, k, vt, name):
    b, s, _ = q.shape
    assert ATTN_Q_TILE == ATTN_KV_TILE == TOKEN_TILE
    return pl.pallas_call(
        _attn_kernel,
        grid=(b, s // ATTN_Q_TILE),
        in_specs=[
            pl.BlockSpec((1, ATTN_Q_TILE, HP_WIDTH), lambda bi, i: (bi, i, 0)),
            pl.BlockSpec((1, s, HP_WIDTH), lambda bi, i: (bi, 0, 0)),
            pl.BlockSpec((1,) + vt.shape[1:], lambda bi, i: (bi, 0, 0, 0)),
        ],
        out_specs=pl.BlockSpec((1, ATTN_Q_TILE, V_WIDTH), lambda bi, i: (bi, i, 0)),
        out_shape=jax.ShapeDtypeStruct((b, s, V_WIDTH), jnp.bfloat16),
        scratch_shapes=[
            pltpu.VMEM((HEADS, 1, ATTN_Q_TILE), jnp.float32),
            pltpu.VMEM((HEADS, VT_ROWS, ATTN_Q_TILE), jnp.float32),
        ],
        compiler_params=pltpu.CompilerParams(
            dimension_semantics=("arbitrary", "arbitrary"), vmem_limit_bytes=VMEM_LIMIT),
        name=name,
    )(q, k, vt)


def _merge_kernel(x_ref, ym_ref, yf_ref, gm_ref, gf_ref, bg_ref, wa_ref, wb_ref, wo_ref,
                  gain_ref, wg_ref, wu_ref, wd_ref, o_ref, t_ref):
    mixed = (jax.nn.sigmoid(gm_ref[...] + bg_ref[0:1, :]) * _dot(ym_ref[...], wa_ref[...])
             + jax.nn.sigmoid(gf_ref[...] + bg_ref[1:2, :]) * _dot(yf_ref[...], wb_ref[...]))
    x = x_ref[...] + _dot(_bf16(mixed), wo_ref[...])
    o_ref[...] = _swiglu_residual(x, gain_ref[...], wg_ref, wu_ref, wd_ref, t_ref)


def _merge_call(x, ym, yf, gm, gf, bg, wa, wb, wo, gain, wg, wu, wd):
    n = x.shape[0]
    tile = lambda w: pl.BlockSpec((TOKEN_TILE, w), lambda i: (i, 0))
    consts = (bg, wa, wb, wo, gain, wg, wu, wd)
    return pl.pallas_call(
        _merge_kernel,
        grid=(n // TOKEN_TILE,),
        in_specs=[tile(D_MODEL), tile(V_WIDTH), tile(V_WIDTH), tile(D_MODEL), tile(D_MODEL)]
        + [_const_spec(c.shape) for c in consts],
        out_specs=tile(D_MODEL),
        out_shape=jax.ShapeDtypeStruct((n, D_MODEL), jnp.float32),
        scratch_shapes=[pltpu.VMEM((TOKEN_TILE, FFN_HIDDEN), jnp.bfloat16)],
        compiler_params=pltpu.CompilerParams(
            dimension_semantics=("arbitrary",), vmem_limit_bytes=VMEM_LIMIT),
        name="merge_ffn2",
    )(x, ym, yf, gm, gf, *consts)


def _take_cols(w, idx):
    idx = np.asarray(idx)
    return jnp.where(idx[None, :] >= 0, w[:, np.maximum(idx, 0)], 0.0)


def _head_padded(offset, dim):
    idx = np.full((HEADS, HEAD_PAD), -1, np.int64)
    idx[:, :dim] = offset + np.arange(HEADS)[:, None] * dim + np.arange(dim)[None, :]
    return idx.reshape(-1)


def _head_lanes(vec, lane0=0):
    tile = jnp.zeros((HEAD_PAD,), jnp.float32).at[lane0:lane0 + vec.shape[0]].set(vec)
    return jnp.tile(tile, HEADS)[None, :]


def _proj_constants(mix_norm, w_in, q_lat_norm, w_qb, kv_lat_norm, w_kvb, q_nope_gain, q_rope_gain,
                    k_nope_gain, k_rope_gain, fox_q_gain, fox_k_gain, fox_b_f):
    f32 = jnp.float32
    o_qlat, o_kvlat = 0, Q_LORA_RANK
    o_kr = o_kvlat + KV_LORA_RANK
    o_fq = o_kr + MLA_ROPE_DIM
    o_fk = o_fq + V_WIDTH
    o_fv = o_fk + V_WIDTH
    o_fl = o_fv + V_WIDTH
    o_gm = o_fl + HEADS
    o_gf = o_gm + D_MODEL

    kr_grp = np.full((LANES,), -1, np.int64)
    kr_grp[:HEADS] = o_fl + np.arange(HEADS)
    kr_grp[ROPE_LANE0:ROPE_LANE0 + MLA_ROPE_DIM] = o_kr + np.arange(MLA_ROPE_DIM)
    cols = np.concatenate([
        np.arange(Q_LORA_RANK), np.full((Q_LAT_PAD - Q_LORA_RANK,), -1),
        o_kvlat + np.arange(KV_LORA_RANK),
        kr_grp,
        _head_padded(o_fq, FOX_HEAD_DIM),
        _head_padded(o_fk, FOX_HEAD_DIM),
        o_gm + np.arange(D_MODEL),
        o_gf + np.arange(D_MODEL),
    ])
    assert cols.shape[0] == PROJ_WIDTH
    win = _bf16(_take_cols(w_in, cols))

    qk_dim = MLA_NOPE_DIM + MLA_ROPE_DIM
    wqb = _take_cols(w_qb, _head_padded(0, qk_dim))
    wqb = _bf16(jnp.pad(wqb, ((0, Q_LAT_PAD - Q_LORA_RANK), (0, 0))))
    kv_dim = MLA_NOPE_DIM + MLA_V_DIM
    k_idx = np.full((HEADS, HEAD_PAD), -1, np.int64)
    k_idx[:, :MLA_NOPE_DIM] = np.arange(HEADS)[:, None] * kv_dim + np.arange(MLA_NOPE_DIM)[None, :]
    wkvb = _bf16(_take_cols(w_kvb, k_idx.reshape(-1)))
    head_rows = np.arange(HEADS)[:, None]
    v_rows = np.arange(MLA_V_DIM)[None, :]
    vt_idx = np.full((HEADS, VT_ROWS), -1, np.int64)
    vt_idx[:, :MLA_V_DIM] = head_rows * kv_dim + MLA_NOPE_DIM + v_rows
    wvt = _bf16(_take_cols(w_kvb, vt_idx.reshape(-1)).T)
    vt_idx[:, :MLA_V_DIM] = o_fv + head_rows * FOX_HEAD_DIM + v_rows
    wfvt = _bf16(_take_cols(w_in, vt_idx.reshape(-1)).T)
    vones = np.zeros((HEADS, VT_ROWS, LANES), np.float32)
    vones[:, VT_ONES_ROW, :] = 1.0
    vones = jnp.asarray(vones.reshape(VT_WIDTH, LANES))

    r = np.arange(TOKEN_TILE)
    tri = jnp.asarray(r[:, None] >= r[None, :], jnp.bfloat16)
    l256 = np.arange(256)
    same_head = (l256[:, None] // HEAD_PAD) == (l256[None, :] // HEAD_PAD)
    in_nope = (l256 % HEAD_PAD) < MLA_NOPE_DIM
    in_rope = ((l256 % HEAD_PAD) >= ROPE_LANE0) & ((l256 % HEAD_PAD) < ROPE_LANE0 + MLA_ROPE_DIM)
    seg_k = same_head & in_nope[:, None] & in_nope[None, :]
    seg_q = seg_k | (same_head & in_rope[:, None] & in_rope[None, :])
    seg_q = jnp.asarray(seg_q, jnp.bfloat16)
    seg_k = jnp.asarray(seg_k, jnp.bfloat16)
    eq = np.zeros((LANES, HP_WIDTH), np.float32)
    ek = np.zeros((LANES, HP_WIDTH), np.float32)
    oneq = np.zeros((1, HP_WIDTH), np.float32)
    onek = np.zeros((1, HP_WIDTH), np.float32)
    for hd in range(HEADS):
        for piece in range(3):
            eq[piece * HEADS + hd, hd * HEAD_PAD + BIAS_LANE0 + piece] = 1.0
            ek[piece * HEADS + hd, hd * HEAD_PAD + BIAS_LANE0 + 3 + piece] = -1.0
            oneq[0, hd * HEAD_PAD + BIAS_LANE0 + 3 + piece] = 1.0
            onek[0, hd * HEAD_PAD + BIAS_LANE0 + piece] = 1.0
    eq = jnp.asarray(eq, jnp.bfloat16)
    ek = jnp.asarray(ek, jnp.bfloat16)

    mla_scale = float(qk_dim) ** -0.5 * LOG2E
    fox_scale = float(FOX_HEAD_DIM) ** -0.5 * LOG2E
    q_gain = _head_lanes(jnp.concatenate([q_nope_gain, q_rope_gain]) * mla_scale)
    icnt = np.ones((HEAD_PAD,), np.float32)
    icnt[:MLA_NOPE_DIM] = 1.0 / MLA_NOPE_DIM
    icnt[ROPE_LANE0:ROPE_LANE0 + MLA_ROPE_DIM] = 1.0 / MLA_ROPE_DIM
    q_icnt = jnp.asarray(np.tile(icnt, HEADS)[None, :])
    kn_gain = _head_lanes(k_nope_gain)
    kr_gain = jnp.zeros((1, LANES), f32).at[0, ROPE_LANE0:ROPE_LANE0 + MLA_ROPE_DIM].set(k_rope_gain)
    fq_gain = _head_lanes(fox_q_gain * fox_scale)
    fk_gain = _head_lanes(fox_k_gain)
    bf = jnp.zeros((1, LANES), f32).at[0, :HEADS].set(fox_b_f)
    inv_freq = ROPE_THETA ** (-jnp.arange(ROPE_HALF, dtype=f32) / ROPE_HALF)
    freq = jnp.zeros((1, LANES), f32).at[0, ROPE_LANE0:ROPE_LANE0 + MLA_ROPE_DIM].set(
        jnp.concatenate([inv_freq, inv_freq]))
    qlat_gain = jnp.pad(q_lat_norm, (0, Q_LAT_PAD - Q_LORA_RANK))[None, :]

    return (mix_norm[None, :], win, wqb, wkvb, wvt, wfvt, vones, tri, seg_q, seg_k, eq, ek,
            qlat_gain, kv_lat_norm[None, :], q_gain, q_icnt, kn_gain, kr_gain, fq_gain, fk_gain,
            bf, freq, jnp.asarray(oneq), jnp.asarray(onek))


def kernel(x, positions, ffn1_norm, ffn1_w_gate, ffn1_w_up, ffn1_w_down, mix_norm, w_in, mla_q_lat_norm, mla_w_qb, mla_kv_lat_norm, mla_w_kvb, mla_q_nope_gain, mla_q_rope_gain, mla_k_nope_gain, mla_k_rope_gain, fox_q_gain, fox_k_gain, fox_b_f, w_branch_mla, w_branch_fox, b_gate, w_o, ffn2_norm, ffn2_w_gate, ffn2_w_up, ffn2_w_down):
    b, s, d = x.shape
    n = b * s
    pos = positions.reshape(b, s, 1)
    xt = x.reshape(n, d)
    for l in range(ffn1_norm.shape[0]):
        xt = _ffn_call(xt, ffn1_norm[l][None, :], _bf16(ffn1_w_gate[l]), _bf16(ffn1_w_up[l]),
                       _bf16(ffn1_w_down[l]))
        consts = _proj_constants(
            mix_norm[l], w_in[l], mla_q_lat_norm[l], mla_w_qb[l], mla_kv_lat_norm[l], mla_w_kvb[l],
            mla_q_nope_gain[l], mla_q_rope_gain[l], mla_k_nope_gain[l], mla_k_rope_gain[l],
            fox_q_gain[l], fox_k_gain[l], fox_b_f[l])
        qm, km, vm, qf, kf, vf, gm, gf = _proj_call(xt.reshape(b, s, d), pos, consts)
        ym = _attn_call(qm, km, vm, "attn_mla")
        yf = _attn_call(qf, kf, vf, "attn_fox")
        xt = _merge_call(
            xt, ym.reshape(n, V_WIDTH), yf.reshape(n, V_WIDTH), gm.reshape(n, d), gf.reshape(n, d),
            b_gate[l], _bf16(w_branch_mla[l]), _bf16(w_branch_fox[l]), _bf16(w_o[l]),
            ffn2_norm[l][None, :], _bf16(ffn2_w_gate[l]), _bf16(ffn2_w_up[l]), _bf16(ffn2_w_down[l]))
    return xt.reshape(b, s, d)
```

```python
import numpy as np
import jax
import jax.numpy as jnp
from jax import lax
from jax.experimental import pallas as pl
from jax.experimental.pallas import tpu as pltpu

D_MODEL = 1024
FFN_HIDDEN = 2816
FFN_RESIDUAL_WEIGHT = 0.5
HEADS = 8
MLA_NOPE_DIM = 64
MLA_ROPE_DIM = 32
MLA_V_DIM = 64
Q_LORA_RANK = 192
KV_LORA_RANK = 128
ROPE_THETA = 10000.0
FOX_HEAD_DIM = 64
RMS_EPS = 1e-6

LANES = 128
HEAD_PAD = LANES
HP_WIDTH = HEADS * HEAD_PAD
V_WIDTH = HEADS * MLA_V_DIM
Q_LAT_PAD = 256
ROPE_HALF = MLA_ROPE_DIM // 2
ROPE_LANE0 = MLA_NOPE_DIM
BIAS_LANE0 = FOX_HEAD_DIM

TOKEN_TILE = 512
HIDDEN_CHUNK = 256
ATTN_Q_TILE = 512
ATTN_KV_TILE = 512
QK_LOOKAHEAD = 2
MASK_VALUE = -1e30
VMEM_LIMIT = 56 * 1024 * 1024

PROJ_QLAT = 0
PROJ_KVLAT = PROJ_QLAT + Q_LAT_PAD
PROJ_KR = PROJ_KVLAT + KV_LORA_RANK
PROJ_FQ = PROJ_KR + LANES
PROJ_FK = PROJ_FQ + V_WIDTH
PROJ_GM = PROJ_FK + V_WIDTH
PROJ_GF = PROJ_GM + D_MODEL
PROJ_WIDTH = PROJ_GF + D_MODEL

VT_ROWS = 80
VT_ONES_ROW = MLA_V_DIM
VT_WIDTH = HEADS * VT_ROWS
NT_DIMS = (((1,), (1,)), ((), ()))
LOG2E = 1.4426950408889634


def _bf16(x):
    return x.astype(jnp.bfloat16)


def _dot(a, b):
    return jnp.dot(a, b, preferred_element_type=jnp.float32)


def _rms_scale(x, n):
    return lax.rsqrt(jnp.sum(x * x, axis=-1, keepdims=True) * (1.0 / n) + RMS_EPS)


def _split2(x):
    hi = _bf16(x)
    return hi, _bf16(x - hi.astype(jnp.float32))


def _split3(x):
    hi = _bf16(x)
    mid, lo = _split2(x - hi.astype(jnp.float32))
    return hi, mid, lo


def _segment_sumsq(x, seg_ref):
    seg = seg_ref[...]
    outs = []
    for c in range(x.shape[1] // 256):
        xs = x[:, c * 256:(c + 1) * 256]
        outs.append(_dot(_bf16(xs * xs), seg))
    return jnp.concatenate(outs, axis=1)


def _swiglu_residual(x, gain, wg_ref, wu_ref, wd_ref, t_ref):
    h = _bf16(x * _rms_scale(x, D_MODEL) * gain)
    for c in range(FFN_HIDDEN // HIDDEN_CHUNK):
        cols = slice(c * HIDDEN_CHUNK, (c + 1) * HIDDEN_CHUNK)
        a = _dot(h, wg_ref[:, cols])
        b = _dot(h, wu_ref[:, cols])
        t_ref[:, cols] = _bf16(a * jax.nn.sigmoid(a) * b)
    return x + FFN_RESIDUAL_WEIGHT * _dot(t_ref[...], wd_ref[...])


def _ffn_kernel(x_ref, gain_ref, wg_ref, wu_ref, wd_ref, o_ref, t_ref):
    o_ref[...] = _swiglu_residual(x_ref[...], gain_ref[...], wg_ref, wu_ref, wd_ref, t_ref)


def _const_spec(shape):
    return pl.BlockSpec(shape, lambda *_: (0,) * len(shape), pipeline_mode=pl.Buffered(1))


def _ffn_call(x, gain, wg, wu, wd):
    n = x.shape[0]
    return pl.pallas_call(
        _ffn_kernel,
        grid=(n // TOKEN_TILE,),
        in_specs=[
            pl.BlockSpec((TOKEN_TILE, D_MODEL), lambda i: (i, 0)),
            _const_spec((1, D_MODEL)),
            _const_spec((D_MODEL, FFN_HIDDEN)),
            _const_spec((D_MODEL, FFN_HIDDEN)),
            _const_spec((FFN_HIDDEN, D_MODEL)),
        ],
        out_specs=pl.BlockSpec((TOKEN_TILE, D_MODEL), lambda i: (i, 0)),
        out_shape=jax.ShapeDtypeStruct((n, D_MODEL), jnp.float32),
        scratch_shapes=[pltpu.VMEM((TOKEN_TILE, FFN_HIDDEN), jnp.bfloat16)],
        compiler_params=pltpu.CompilerParams(
            dimension_semantics=("arbitrary",), vmem_limit_bytes=VMEM_LIMIT),
        name="ffn1",
    )(x, gain, wg, wu, wd)


def _proj_kernel(x_ref, pos_ref, gain_ref, win_ref, wqb_ref, wkvb_ref, wvt_ref, wfvt_ref, vones_ref,
                 tri_ref, segq_ref, seg64_ref, eq_ref, ek_ref,
                 qlat_gain_ref, kvlat_gain_ref, q_gain_ref, q_icnt_ref, kn_gain_ref,
                 kr_gain_ref, fq_gain_ref, fk_gain_ref, bf_ref, freq_ref, oneq_ref, onek_ref,
                 qm_ref, km_ref, vm_ref, qf_ref, kf_ref, vf_ref, gm_ref, gf_ref,
                 carry_ref):
    x = x_ref[0]
    h = _bf16(x * _rms_scale(x, D_MODEL) * gain_ref[...])

    def proj(lo, width):
        return _dot(h, win_ref[:, lo:lo + width])

    lane = lax.broadcasted_iota(jnp.int32, (1, LANES), 1)
    rope_lanes = (lane >= ROPE_LANE0) & (lane < ROPE_LANE0 + MLA_ROPE_DIM)

    ang = pos_ref[0].astype(jnp.float32) * freq_ref[...]
    cos_t = jnp.cos(ang)
    sin_t = jnp.sin(ang)
    sin_hi = jnp.where(lane >= ROPE_LANE0 + ROPE_HALF, sin_t, 0.0)
    sin_lo = jnp.where(lane < ROPE_LANE0 + ROPE_HALF, -sin_t, 0.0)

    def rotary(t):
        return (t * cos_t + pltpu.roll(t, ROPE_HALF, 1) * sin_hi
                + pltpu.roll(t, LANES - ROPE_HALF, 1) * sin_lo)

    def store_values_t(ref, wt_ref, act):
        vt = lax.dot_general(wt_ref[...], act, NT_DIMS, preferred_element_type=jnp.float32)
        ones = jnp.concatenate([vones_ref[...]] * (TOKEN_TILE // LANES), axis=1)
        ref[0, 0] = _bf16(vt + ones)

    def head_normed(t, g_ref):
        return t * lax.rsqrt(_segment_sumsq(t, seg64_ref) * (1.0 / FOX_HEAD_DIM) + RMS_EPS) * g_ref[...]

    def head_tiles(t):
        low = lane < FOX_HEAD_DIM
        tiles = []
        for pair in range(HEADS // 2):
            src = t[:, pair * LANES:(pair + 1) * LANES]
            tiles.append(jnp.where(low, src, 0.0))
            tiles.append(jnp.where(low, pltpu.roll(src, LANES - FOX_HEAD_DIM, 1), 0.0))
        return tiles

    q_lat = proj(PROJ_QLAT, Q_LAT_PAD)
    q_lat = _bf16(q_lat * _rms_scale(q_lat, Q_LORA_RANK) * qlat_gain_ref[...])
    q = _dot(q_lat, wqb_ref[...])
    q = q * lax.rsqrt(_segment_sumsq(q, segq_ref) * q_icnt_ref[...] + RMS_EPS) * q_gain_ref[...]
    for hd in range(HEADS):
        cols = slice(hd * HEAD_PAD, (hd + 1) * HEAD_PAD)
        qm_ref[0, :, cols] = _bf16(rotary(q[:, cols]))

    kv_lat = proj(PROJ_KVLAT, KV_LORA_RANK)
    kv_lat = _bf16(kv_lat * _rms_scale(kv_lat, KV_LORA_RANK) * kvlat_gain_ref[...])
    k_nope = head_tiles(head_normed(_dot(kv_lat, wkvb_ref[...]), kn_gain_ref))
    store_values_t(vm_ref, wvt_ref, kv_lat)

    kr_grp = proj(PROJ_KR, LANES)
    kr = jnp.where(rope_lanes, kr_grp, 0.0)
    kr = rotary(kr * _rms_scale(kr, MLA_ROPE_DIM) * kr_gain_ref[...])
    for hd in range(HEADS):
        cols = slice(hd * HEAD_PAD, (hd + 1) * HEAD_PAD)
        km_ref[0, :, cols] = _bf16(k_nope[hd] + kr)

    @pl.when(pl.program_id(1) == 0)
    def _():
        carry_ref[...] = jnp.zeros_like(carry_ref)

    log_f = jnp.where(lane < HEADS, jax.nn.log_sigmoid(kr_grp + bf_ref[...]), 0.0)
    tri = tri_ref[...]
    f_hi, f_lo = _split2(log_f)
    c = _dot(tri, f_hi) + _dot(tri, f_lo) + carry_ref[0:1, :]
    carry_ref[0:1, :] = c[TOKEN_TILE - 1:TOKEN_TILE, :]
    c2 = c * LOG2E
    c_rep = c2 + pltpu.roll(c2, HEADS, 1) + pltpu.roll(c2, 2 * HEADS, 1)
    c_hi, c_mid, c_lo = _split3(c_rep)
    c_parts = jnp.where(lane < HEADS, c_hi, jnp.where(lane < 2 * HEADS, c_mid, c_lo))

    for o_ref, col0, g_ref, e_ref, one_ref in ((qf_ref, PROJ_FQ, fq_gain_ref, eq_ref, oneq_ref),
                                               (kf_ref, PROJ_FK, fk_gain_ref, ek_ref, onek_ref)):
        tiles = head_tiles(head_normed(proj(col0, V_WIDTH), g_ref))
        bias = _dot(c_parts, e_ref[...]) + one_ref[...]
        for hd in range(HEADS):
            cols = slice(hd * HEAD_PAD, (hd + 1) * HEAD_PAD)
            o_ref[0, :, cols] = _bf16(tiles[hd] + bias[:, cols])
    store_values_t(vf_ref, wfvt_ref, h)

    gm_ref[0] = proj(PROJ_GM, D_MODEL)
    gf_ref[0] = proj(PROJ_GF, D_MODEL)


def _proj_call(x, pos, consts):
    b, s, _ = x.shape
    bf16, f32 = jnp.bfloat16, jnp.float32
    n_tiles = s // TOKEN_TILE
    tile = lambda w: pl.BlockSpec((1, TOKEN_TILE, w), lambda bi, i: (bi, i, 0))
    tile_t = pl.BlockSpec((1, 1, VT_WIDTH, TOKEN_TILE), lambda bi, i: (bi, i, 0, 0))
    rows = lambda w, d: jax.ShapeDtypeStruct((b, s, w), d)
    rows_t = jax.ShapeDtypeStruct((b, n_tiles, VT_WIDTH, TOKEN_TILE), bf16)
    return pl.pallas_call(
        _proj_kernel,
        grid=(b, n_tiles),
        in_specs=[tile(D_MODEL), tile(1)] + [_const_spec(c.shape) for c in consts],
        out_specs=[tile(HP_WIDTH), tile(HP_WIDTH), tile_t, tile(HP_WIDTH), tile(HP_WIDTH), tile_t,
                   tile(D_MODEL), tile(D_MODEL)],
        out_shape=[rows(HP_WIDTH, bf16), rows(HP_WIDTH, bf16), rows_t, rows(HP_WIDTH, bf16),
                   rows(HP_WIDTH, bf16), rows_t, rows(D_MODEL, f32), rows(D_MODEL, f32)],
        scratch_shapes=[pltpu.VMEM((8, LANES), jnp.float32)],
        compiler_params=pltpu.CompilerParams(
            dimension_semantics=("arbitrary", "arbitrary"), vmem_limit_bytes=VMEM_LIMIT),
        name="proj",
    )(x, pos, *consts)


def _attn_kernel(q_ref, k_ref, vt_ref, o_ref, m_ref, acc_ref):
    qi = pl.program_id(1)
    tq, tk = ATTN_Q_TILE, ATTN_KV_TILE

    def step(t, causal_mask):
        off = pl.multiple_of(t * tk, tk)

        def logits(hd):
            hcols = slice(hd * HEAD_PAD, (hd + 1) * HEAD_PAD)
            return lax.dot_general(k_ref[0, pl.ds(off, tk), hcols], q_ref[0, :, hcols], NT_DIMS,
                                   preferred_element_type=jnp.float32)

        pending = [logits(hd) for hd in range(QK_LOOKAHEAD)]
        for hd in range(HEADS):
            s = pending.pop(0)
            if hd + QK_LOOKAHEAD < HEADS:
                pending.append(logits(hd + QK_LOOKAHEAD))
            if causal_mask is not None:
                s = jnp.where(causal_mask, s, MASK_VALUE)
            m_prev = m_ref[hd]
            m_new = jnp.maximum(m_prev, jnp.max(s, axis=0, keepdims=True))
            alpha = jnp.exp2(m_prev - m_new)
            p = _bf16(jnp.exp2(s - m_new))
            vt = vt_ref[0, t, hd * VT_ROWS:(hd + 1) * VT_ROWS, :]
            acc_ref[hd] = alpha * acc_ref[hd] + _dot(vt, p)
            m_ref[hd] = m_new

    m_ref[...] = jnp.full(m_ref.shape, MASK_VALUE, jnp.float32)
    acc_ref[...] = jnp.zeros(acc_ref.shape, jnp.float32)
    key = lax.broadcasted_iota(jnp.int32, (tk, tq), 0)
    query = lax.broadcasted_iota(jnp.int32, (tk, tq), 1)
    step(qi, key <= query)

    def body(t, carry):
        step(t, None)
        return carry

    lax.fori_loop(0, qi, body, 0)

    for pair in range(HEADS // 2):
        y_t = []
        for hd in (2 * pair, 2 * pair + 1):
            acc = acc_ref[hd]
            y_t.append(acc[:MLA_V_DIM] / acc[VT_ONES_ROW:VT_ONES_ROW + 1])
        y = jnp.concatenate(y_t, axis=0).T
        o_ref[0, :, pair * LANES:(pair + 1) * LANES] = y.astype(o_ref.dtype)


def _attn_call(q, k, vt, name):
    b, s, _ = q.shape
    assert ATTN_Q_TILE == ATTN_KV_TILE == TOKEN_TILE
    return pl.pallas_call(
        _attn_kernel,
        grid=(b, s // ATTN_Q_TILE),
        in_specs=[
            pl.BlockSpec((1, ATTN_Q_TILE, HP_WIDTH), lambda bi, i: (bi, i, 0)),
            pl.BlockSpec((1, s, HP_WIDTH), lambda bi, i: (bi, 0, 0)),
            pl.BlockSpec((1,) + vt.shape[1:], lambda bi, i: (bi, 0, 0, 0)),
        ],
        out_specs=pl.BlockSpec((1, ATTN_Q_TILE, V_WIDTH), lambda bi, i: (bi, i, 0)),
        out_shape=jax.ShapeDtypeStruct((b, s, V_WIDTH), jnp.bfloat16),
        scratch_shapes=[
            pltpu.VMEM((HEADS, 1, ATTN_Q_TILE), jnp.float32),
            pltpu.VMEM((HEADS, VT_ROWS, ATTN_Q_TILE), jnp.float32),
        ],
        compiler_params=pltpu.CompilerParams(
            dimension_semantics=("arbitrary", "arbitrary"), vmem_limit_bytes=VMEM_LIMIT),
        name=name,
    )(q, k, vt)


def _merge_kernel(x_ref, ym_ref, yf_ref, gm_ref, gf_ref, bg_ref, wa_ref, wb_ref, wo_ref,
                  gain_ref, wg_ref, wu_ref, wd_ref, o_ref, t_ref):
    mixed = (jax.nn.sigmoid(gm_ref[...] + bg_ref[0:1, :]) * _dot(ym_ref[...], wa_ref[...])
             + jax.nn.sigmoid(gf_ref[...] + bg_ref[1:2, :]) * _dot(yf_ref[...], wb_ref[...]))
    x = x_ref[...] + _dot(_bf16(mixed), wo_ref[...])
    o_ref[...] = _swiglu_residual(x, gain_ref[...], wg_ref, wu_ref, wd_ref, t_ref)


def _merge_call(x, ym, yf, gm, gf, bg, wa, wb, wo, gain, wg, wu, wd):
    n = x.shape[0]
    tile = lambda w: pl.BlockSpec((TOKEN_TILE, w), lambda i: (i, 0))
    consts = (bg, wa, wb, wo, gain, wg, wu, wd)
    return pl.pallas_call(
        _merge_kernel,
        grid=(n // TOKEN_TILE,),
        in_specs=[tile(D_MODEL), tile(V_WIDTH), tile(V_WIDTH), tile(D_MODEL), tile(D_MODEL)]
        + [_const_spec(c.shape) for c in consts],
        out_specs=tile(D_MODEL),
        out_shape=jax.ShapeDtypeStruct((n, D_MODEL), jnp.float32),
        scratch_shapes=[pltpu.VMEM((TOKEN_TILE, FFN_HIDDEN), jnp.bfloat16)],
        compiler_params=pltpu.CompilerParams(
            dimension_semantics=("arbitrary",), vmem_limit_bytes=VMEM_LIMIT),
        name="merge_ffn2",
    )(x, ym, yf, gm, gf, *consts)


def _pad_heads(w, width):
    rows = w.shape[0]
    w = w.reshape(rows, HEADS, -1)
    return jnp.pad(w, ((0, 0), (0, 0), (0, width - w.shape[2]))).reshape(rows, HEADS * width)


def _head_lanes(vec, lane0=0):
    tile = jnp.zeros((HEAD_PAD,), jnp.float32).at[lane0:lane0 + vec.shape[0]].set(vec)
    return jnp.tile(tile, HEADS)[None, :]


def _proj_constants(mix_norm, w_in, q_lat_norm, w_qb, kv_lat_norm, w_kvb, q_nope_gain, q_rope_gain,
                    k_nope_gain, k_rope_gain, fox_q_gain, fox_k_gain, fox_b_f):
    f32 = jnp.float32
    o_qlat, o_kvlat = 0, Q_LORA_RANK
    o_kr = o_kvlat + KV_LORA_RANK
    o_fq = o_kr + MLA_ROPE_DIM
    o_fk = o_fq + V_WIDTH
    o_fv = o_fk + V_WIDTH
    o_fl = o_fv + V_WIDTH
    o_gm = o_fl + HEADS
    o_gf = o_gm + D_MODEL

    def zero_cols(n):
        return jnp.zeros((D_MODEL, n), w_in.dtype)

    win = _bf16(jnp.concatenate([
        w_in[:, o_qlat:o_kvlat], zero_cols(Q_LAT_PAD - Q_LORA_RANK),
        w_in[:, o_kvlat:o_kr],
        w_in[:, o_fl:o_gm], zero_cols(ROPE_LANE0 - HEADS),
        w_in[:, o_kr:o_fq], zero_cols(LANES - ROPE_LANE0 - MLA_ROPE_DIM),
        w_in[:, o_fq:o_fv],
        w_in[:, o_gm:],
    ], axis=1))
    assert win.shape[1] == PROJ_WIDTH

    qk_dim = MLA_NOPE_DIM + MLA_ROPE_DIM
    wqb = _bf16(jnp.pad(_pad_heads(w_qb, HEAD_PAD), ((0, Q_LAT_PAD - Q_LORA_RANK), (0, 0))))
    w_kvb = w_kvb.reshape(KV_LORA_RANK, HEADS, MLA_NOPE_DIM + MLA_V_DIM)
    wkvb = _bf16(w_kvb[:, :, :MLA_NOPE_DIM].reshape(KV_LORA_RANK, V_WIDTH))
    wvt = _bf16(_pad_heads(w_kvb[:, :, MLA_NOPE_DIM:].reshape(KV_LORA_RANK, V_WIDTH), VT_ROWS).T)
    wfvt = _bf16(_pad_heads(w_in[:, o_fv:o_fl], VT_ROWS).T)
    vones = np.zeros((HEADS, VT_ROWS, LANES), np.float32)
    vones[:, VT_ONES_ROW, :] = 1.0
    vones = jnp.asarray(vones.reshape(VT_WIDTH, LANES))

    r = np.arange(TOKEN_TILE)
    tri = jnp.asarray(r[:, None] >= r[None, :], jnp.bfloat16)
    l256 = np.arange(256)
    same_head = (l256[:, None] // HEAD_PAD) == (l256[None, :] // HEAD_PAD)
    in_nope = (l256 % HEAD_PAD) < MLA_NOPE_DIM
    in_rope = ((l256 % HEAD_PAD) >= ROPE_LANE0) & ((l256 % HEAD_PAD) < ROPE_LANE0 + MLA_ROPE_DIM)
    seg_q = same_head & ((in_nope[:, None] & in_nope[None, :]) | (in_rope[:, None] & in_rope[None, :]))
    seg_q = jnp.asarray(seg_q, jnp.bfloat16)
    seg_64 = jnp.asarray((l256[:, None] // FOX_HEAD_DIM) == (l256[None, :] // FOX_HEAD_DIM),
                         jnp.bfloat16)
    eq = np.zeros((LANES, HP_WIDTH), np.float32)
    ek = np.zeros((LANES, HP_WIDTH), np.float32)
    oneq = np.zeros((1, HP_WIDTH), np.float32)
    onek = np.zeros((1, HP_WIDTH), np.float32)
    for hd in range(HEADS):
        for piece in range(3):
            eq[piece * HEADS + hd, hd * HEAD_PAD + BIAS_LANE0 + piece] = 1.0
            ek[piece * HEADS + hd, hd * HEAD_PAD + BIAS_LANE0 + 3 + piece] = -1.0
            oneq[0, hd * HEAD_PAD + BIAS_LANE0 + 3 + piece] = 1.0
            onek[0, hd * HEAD_PAD + BIAS_LANE0 + piece] = 1.0
    eq = jnp.asarray(eq, jnp.bfloat16)
    ek = jnp.asarray(ek, jnp.bfloat16)

    mla_scale = float(qk_dim) ** -0.5 * LOG2E
    fox_scale = float(FOX_HEAD_DIM) ** -0.5 * LOG2E
    q_gain = _head_lanes(jnp.concatenate([q_nope_gain, q_rope_gain]) * mla_scale)
    icnt = np.ones((HEAD_PAD,), np.float32)
    icnt[:MLA_NOPE_DIM] = 1.0 / MLA_NOPE_DIM
    icnt[ROPE_LANE0:ROPE_LANE0 + MLA_ROPE_DIM] = 1.0 / MLA_ROPE_DIM
    q_icnt = jnp.asarray(np.tile(icnt, HEADS)[None, :])
    kn_gain = jnp.tile(k_nope_gain, HEADS)[None, :]
    kr_gain = jnp.zeros((1, LANES), f32).at[0, ROPE_LANE0:ROPE_LANE0 + MLA_ROPE_DIM].set(k_rope_gain)
    fq_gain = jnp.tile(fox_q_gain * fox_scale, HEADS)[None, :]
    fk_gain = jnp.tile(fox_k_gain, HEADS)[None, :]
    bf = jnp.zeros((1, LANES), f32).at[0, :HEADS].set(fox_b_f)
    inv_freq = ROPE_THETA ** (-jnp.arange(ROPE_HALF, dtype=f32) / ROPE_HALF)
    freq = jnp.zeros((1, LANES), f32).at[0, ROPE_LANE0:ROPE_LANE0 + MLA_ROPE_DIM].set(
        jnp.concatenate([inv_freq, inv_freq]))
    qlat_gain = jnp.pad(q_lat_norm, (0, Q_LAT_PAD - Q_LORA_RANK))[None, :]

    return (mix_norm[None, :], win, wqb, wkvb, wvt, wfvt, vones, tri, seg_q, seg_64, eq, ek,
            qlat_gain, kv_lat_norm[None, :], q_gain, q_icnt, kn_gain, kr_gain, fq_gain, fk_gain,
            bf, freq, jnp.asarray(oneq), jnp.asarray(onek))


def kernel(x, positions, ffn1_norm, ffn1_w_gate, ffn1_w_up, ffn1_w_down, mix_norm, w_in, mla_q_lat_norm, mla_w_qb, mla_kv_lat_norm, mla_w_kvb, mla_q_nope_gain, mla_q_rope_gain, mla_k_nope_gain, mla_k_rope_gain, fox_q_gain, fox_k_gain, fox_b_f, w_branch_mla, w_branch_fox, b_gate, w_o, ffn2_norm, ffn2_w_gate, ffn2_w_up, ffn2_w_down):
    b, s, d = x.shape
    n = b * s
    pos = positions.reshape(b, s, 1)
    xt = x.reshape(n, d)
    for l in range(ffn1_norm.shape[0]):
        xt = _ffn_call(xt, ffn1_norm[l][None, :], _bf16(ffn1_w_gate[l]), _bf16(ffn1_w_up[l]),
                       _bf16(ffn1_w_down[l]))
        consts = _proj_constants(
            mix_norm[l], w_in[l], mla_q_lat_norm[l], mla_w_qb[l], mla_kv_lat_norm[l], mla_w_kvb[l],
            mla_q_nope_gain[l], mla_q_rope_gain[l], mla_k_nope_gain[l], mla_k_rope_gain[l],
            fox_q_gain[l], fox_k_gain[l], fox_b_f[l])
        qm, km, vm, qf, kf, vf, gm, gf = _proj_call(xt.reshape(b, s, d), pos, consts)
        ym = _attn_call(qm, km, vm, "attn_mla")
        yf = _attn_call(qf, kf, vf, "attn_fox")
        xt = _merge_call(
            xt, ym.reshape(n, V_WIDTH), yf.reshape(n, V_WIDTH), gm.reshape(n, d), gf.reshape(n, d),
            b_gate[l], _bf16(w_branch_mla[l]), _bf16(w_branch_fox[l]), _bf16(w_o[l]),
            ffn2_norm[l][None, :], _bf16(ffn2_w_gate[l]), _bf16(ffn2_w_up[l]), _bf16(ffn2_w_down[l]))
    return xt.reshape(b, s, d)
```

```python
import numpy as np
import jax
import jax.numpy as jnp
from jax import lax
from jax.experimental import pallas as pl
from jax.experimental.pallas import tpu as pltpu

D_MODEL = 1024
FFN_HIDDEN = 2816
FFN_RESIDUAL_WEIGHT = 0.5
HEADS = 8
MLA_NOPE_DIM = 64
MLA_ROPE_DIM = 32
MLA_V_DIM = 64
Q_LORA_RANK = 192
KV_LORA_RANK = 128
ROPE_THETA = 10000.0
FOX_HEAD_DIM = 64
RMS_EPS = 1e-6

LANES = 128
HEAD_PAD = LANES
HP_WIDTH = HEADS * HEAD_PAD
V_WIDTH = HEADS * MLA_V_DIM
Q_LAT_PAD = 256
ROPE_HALF = MLA_ROPE_DIM // 2
ROPE_LANE0 = MLA_NOPE_DIM
BIAS_LANE0 = FOX_HEAD_DIM

TOKEN_TILE = 512
HIDDEN_CHUNK = 256
ATTN_Q_TILE = 512
ATTN_KV_TILE = 512
ATTN_KV_SUB = 256
QK_LOOKAHEAD = 2
MASK_VALUE = -1e30
VMEM_LIMIT = 56 * 1024 * 1024

PROJ_QLAT = 0
PROJ_KVLAT = PROJ_QLAT + Q_LAT_PAD
PROJ_KR = PROJ_KVLAT + KV_LORA_RANK
PROJ_FQ = PROJ_KR + LANES
PROJ_FK = PROJ_FQ + V_WIDTH
PROJ_GM = PROJ_FK + V_WIDTH
PROJ_GF = PROJ_GM + D_MODEL
PROJ_WIDTH = PROJ_GF + D_MODEL

VT_ROWS = 80
VT_ONES_ROW = MLA_V_DIM
VT_WIDTH = HEADS * VT_ROWS
NT_DIMS = (((1,), (1,)), ((), ()))
LOG2E = 1.4426950408889634


def _bf16(x):
    return x.astype(jnp.bfloat16)


def _dot(a, b):
    return jnp.dot(a, b, preferred_element_type=jnp.float32)


def _rms_scale(x, n):
    return lax.rsqrt(jnp.sum(x * x, axis=-1, keepdims=True) * (1.0 / n) + RMS_EPS)


def _split2(x):
    hi = _bf16(x)
    return hi, _bf16(x - hi.astype(jnp.float32))


def _split3(x):
    hi = _bf16(x)
    mid, lo = _split2(x - hi.astype(jnp.float32))
    return hi, mid, lo


def _segment_sumsq(x, seg_ref):
    seg = seg_ref[...]
    outs = []
    for c in range(x.shape[1] // 256):
        xs = x[:, c * 256:(c + 1) * 256]
        outs.append(_dot(_bf16(xs * xs), seg))
    return jnp.concatenate(outs, axis=1)


def _swiglu_residual(x, gain, wg_ref, wu_ref, wd_ref, t_ref):
    h = _bf16(x * _rms_scale(x, D_MODEL) * gain)
    for c in range(FFN_HIDDEN // HIDDEN_CHUNK):
        cols = slice(c * HIDDEN_CHUNK, (c + 1) * HIDDEN_CHUNK)
        a = _dot(h, wg_ref[:, cols])
        b = _dot(h, wu_ref[:, cols])
        t_ref[:, cols] = _bf16(a * jax.nn.sigmoid(a) * b)
    return x + FFN_RESIDUAL_WEIGHT * _dot(t_ref[...], wd_ref[...])


def _ffn_kernel(x_ref, gain_ref, wg_ref, wu_ref, wd_ref, o_ref, t_ref):
    o_ref[...] = _swiglu_residual(x_ref[...], gain_ref[...], wg_ref, wu_ref, wd_ref, t_ref)


def _const_spec(shape):
    return pl.BlockSpec(shape, lambda *_: (0,) * len(shape), pipeline_mode=pl.Buffered(1))


def _ffn_call(x, gain, wg, wu, wd):
    n = x.shape[0]
    return pl.pallas_call(
        _ffn_kernel,
        grid=(n // TOKEN_TILE,),
        in_specs=[
            pl.BlockSpec((TOKEN_TILE, D_MODEL), lambda i: (i, 0)),
            _const_spec((1, D_MODEL)),
            _const_spec((D_MODEL, FFN_HIDDEN)),
            _const_spec((D_MODEL, FFN_HIDDEN)),
            _const_spec((FFN_HIDDEN, D_MODEL)),
        ],
        out_specs=pl.BlockSpec((TOKEN_TILE, D_MODEL), lambda i: (i, 0)),
        out_shape=jax.ShapeDtypeStruct((n, D_MODEL), jnp.float32),
        scratch_shapes=[pltpu.VMEM((TOKEN_TILE, FFN_HIDDEN), jnp.bfloat16)],
        compiler_params=pltpu.CompilerParams(
            dimension_semantics=("arbitrary",), vmem_limit_bytes=VMEM_LIMIT),
        name="ffn1",
    )(x, gain, wg, wu, wd)


def _proj_kernel(x_ref, pos_ref, gain_ref, win_ref, wqb_ref, wkvb_ref, wvt_ref, wfvt_ref, vones_ref,
                 tri_ref, segq_ref, seg64_ref, eq_ref, ek_ref,
                 qlat_gain_ref, kvlat_gain_ref, q_gain_ref, q_icnt_ref, kn_gain_ref,
                 kr_gain_ref, fq_gain_ref, fk_gain_ref, bf_ref, freq_ref, oneq_ref, onek_ref,
                 qm_ref, km_ref, vm_ref, qf_ref, kf_ref, vf_ref, gm_ref, gf_ref,
                 carry_ref):
    x = x_ref[0]
    h = _bf16(x * _rms_scale(x, D_MODEL) * gain_ref[...])

    def proj(lo, width):
        return _dot(h, win_ref[:, lo:lo + width])

    lane = lax.broadcasted_iota(jnp.int32, (1, LANES), 1)
    rope_lanes = (lane >= ROPE_LANE0) & (lane < ROPE_LANE0 + MLA_ROPE_DIM)

    ang = pos_ref[0].astype(jnp.float32) * freq_ref[...]
    cos_t = jnp.cos(ang)
    sin_t = jnp.sin(ang)
    sin_hi = jnp.where(lane >= ROPE_LANE0 + ROPE_HALF, sin_t, 0.0)
    sin_lo = jnp.where(lane < ROPE_LANE0 + ROPE_HALF, -sin_t, 0.0)

    def rotary(t):
        return (t * cos_t + pltpu.roll(t, ROPE_HALF, 1) * sin_hi
                + pltpu.roll(t, LANES - ROPE_HALF, 1) * sin_lo)

    def store_values_t(ref, wt_ref, act):
        vt = lax.dot_general(wt_ref[...], act, NT_DIMS, preferred_element_type=jnp.float32)
        ones = jnp.concatenate([vones_ref[...]] * (TOKEN_TILE // LANES), axis=1)
        ref[0, 0] = _bf16(vt + ones)

    def head_normed(t, g_ref):
        return t * lax.rsqrt(_segment_sumsq(t, seg64_ref) * (1.0 / FOX_HEAD_DIM) + RMS_EPS) * g_ref[...]

    def head_tiles(t):
        low = lane < FOX_HEAD_DIM
        tiles = []
        for pair in range(HEADS // 2):
            src = t[:, pair * LANES:(pair + 1) * LANES]
            tiles.append(jnp.where(low, src, 0.0))
            tiles.append(jnp.where(low, pltpu.roll(src, LANES - FOX_HEAD_DIM, 1), 0.0))
        return tiles

    q_lat = proj(PROJ_QLAT, Q_LAT_PAD)
    q_lat = _bf16(q_lat * _rms_scale(q_lat, Q_LORA_RANK) * qlat_gain_ref[...])
    q = _dot(q_lat, wqb_ref[...])
    q = q * lax.rsqrt(_segment_sumsq(q, segq_ref) * q_icnt_ref[...] + RMS_EPS) * q_gain_ref[...]
    for hd in range(HEADS):
        cols = slice(hd * HEAD_PAD, (hd + 1) * HEAD_PAD)
        qm_ref[0, :, cols] = _bf16(rotary(q[:, cols]))

    kv_lat = proj(PROJ_KVLAT, KV_LORA_RANK)
    kv_lat = _bf16(kv_lat * _rms_scale(kv_lat, KV_LORA_RANK) * kvlat_gain_ref[...])
    k_nope = head_tiles(head_normed(_dot(kv_lat, wkvb_ref[...]), kn_gain_ref))
    store_values_t(vm_ref, wvt_ref, kv_lat)

    kr_grp = proj(PROJ_KR, LANES)
    kr = jnp.where(rope_lanes, kr_grp, 0.0)
    kr = rotary(kr * _rms_scale(kr, MLA_ROPE_DIM) * kr_gain_ref[...])
    for hd in range(HEADS):
        cols = slice(hd * HEAD_PAD, (hd + 1) * HEAD_PAD)
        km_ref[0, :, cols] = _bf16(k_nope[hd] + kr)

    @pl.when(pl.program_id(1) == 0)
    def _():
        carry_ref[...] = jnp.zeros_like(carry_ref)

    log_f = jnp.where(lane < HEADS, jax.nn.log_sigmoid(kr_grp + bf_ref[...]), 0.0)
    tri = tri_ref[...]
    f_hi, f_lo = _split2(log_f)
    c = _dot(tri, f_hi) + _dot(tri, f_lo) + carry_ref[0:1, :]
    carry_ref[0:1, :] = c[TOKEN_TILE - 1:TOKEN_TILE, :]
    c2 = c * LOG2E
    c_rep = c2 + pltpu.roll(c2, HEADS, 1) + pltpu.roll(c2, 2 * HEADS, 1)
    c_hi, c_mid, c_lo = _split3(c_rep)
    c_parts = jnp.where(lane < HEADS, c_hi, jnp.where(lane < 2 * HEADS, c_mid, c_lo))

    for o_ref, col0, g_ref, e_ref, one_ref in ((qf_ref, PROJ_FQ, fq_gain_ref, eq_ref, oneq_ref),
                                               (kf_ref, PROJ_FK, fk_gain_ref, ek_ref, onek_ref)):
        tiles = head_tiles(head_normed(proj(col0, V_WIDTH), g_ref))
        bias = _dot(c_parts, e_ref[...]) + one_ref[...]
        for hd in range(HEADS):
            cols = slice(hd * HEAD_PAD, (hd + 1) * HEAD_PAD)
            o_ref[0, :, cols] = _bf16(tiles[hd] + bias[:, cols])
    store_values_t(vf_ref, wfvt_ref, h)

    gm_ref[0] = proj(PROJ_GM, D_MODEL)
    gf_ref[0] = proj(PROJ_GF, D_MODEL)


def _proj_call(x, pos, consts):
    b, s, _ = x.shape
    bf16, f32 = jnp.bfloat16, jnp.float32
    n_tiles = s // TOKEN_TILE
    tile = lambda w: pl.BlockSpec((1, TOKEN_TILE, w), lambda bi, i: (bi, i, 0))
    tile_t = pl.BlockSpec((1, 1, VT_WIDTH, TOKEN_TILE), lambda bi, i: (bi, i, 0, 0))
    rows = lambda w, d: jax.ShapeDtypeStruct((b, s, w), d)
    rows_t = jax.ShapeDtypeStruct((b, n_tiles, VT_WIDTH, TOKEN_TILE), bf16)
    return pl.pallas_call(
        _proj_kernel,
        grid=(b, n_tiles),
        in_specs=[tile(D_MODEL), tile(1)] + [_const_spec(c.shape) for c in consts],
        out_specs=[tile(HP_WIDTH), tile(HP_WIDTH), tile_t, tile(HP_WIDTH), tile(HP_WIDTH), tile_t,
                   tile(D_MODEL), tile(D_MODEL)],
        out_shape=[rows(HP_WIDTH, bf16), rows(HP_WIDTH, bf16), rows_t, rows(HP_WIDTH, bf16),
                   rows(HP_WIDTH, bf16), rows_t, rows(D_MODEL, f32), rows(D_MODEL, f32)],
        scratch_shapes=[pltpu.VMEM((8, LANES), jnp.float32)],
        compiler_params=pltpu.CompilerParams(
            dimension_semantics=("arbitrary", "arbitrary"), vmem_limit_bytes=VMEM_LIMIT),
        name="proj",
    )(x, pos, *consts)


def _attn_kernel(q_ref, k_ref, vt_ref, o_ref, m_ref, acc_ref):
    qi = pl.program_id(1)
    tq, tk, ks = ATTN_Q_TILE, ATTN_KV_TILE, ATTN_KV_SUB

    def step(t, diagonal):
        off = pl.multiple_of(t * tk, tk)
        units = [(hd, sub) for sub in range(tk // ks) for hd in range(HEADS)]

        def logits(hd, sub):
            hcols = slice(hd * HEAD_PAD, (hd + 1) * HEAD_PAD)
            q_lo = sub * ks if diagonal else 0
            return lax.dot_general(k_ref[0, pl.ds(off + sub * ks, ks), hcols], q_ref[0, q_lo:, hcols],
                                   NT_DIMS, preferred_element_type=jnp.float32)

        pending = [logits(*u) for u in units[:QK_LOOKAHEAD]]
        for i, (hd, sub) in enumerate(units):
            s = pending.pop(0)
            if i + QK_LOOKAHEAD < len(units):
                pending.append(logits(*units[i + QK_LOOKAHEAD]))
            q_lo = sub * ks if diagonal else 0
            if diagonal:
                key = lax.broadcasted_iota(jnp.int32, s.shape, 0)
                query = lax.broadcasted_iota(jnp.int32, s.shape, 1)
                s = jnp.where(key <= query, s, MASK_VALUE)
            m_prev = m_ref[hd, :, q_lo:]
            m_new = jnp.maximum(m_prev, jnp.max(s, axis=0, keepdims=True))
            alpha = jnp.exp2(m_prev - m_new)
            p = _bf16(jnp.exp2(s - m_new))
            vt = vt_ref[0, t, hd * VT_ROWS:(hd + 1) * VT_ROWS, sub * ks:(sub + 1) * ks]
            acc_ref[hd, :, q_lo:] = alpha * acc_ref[hd, :, q_lo:] + _dot(vt, p)
            m_ref[hd, :, q_lo:] = m_new

    m_ref[...] = jnp.full(m_ref.shape, MASK_VALUE, jnp.float32)
    acc_ref[...] = jnp.zeros(acc_ref.shape, jnp.float32)
    step(qi, True)

    def body(t, carry):
        step(t, False)
        return carry

    lax.fori_loop(0, qi, body, 0)

    for pair in range(HEADS // 2):
        y_t = []
        for hd in (2 * pair, 2 * pair + 1):
            acc = acc_ref[hd]
            y_t.append(acc[:MLA_V_DIM] / acc[VT_ONES_ROW:VT_ONES_ROW + 1])
        y = jnp.concatenate(y_t, axis=0).T
        o_ref[0, :, pair * LANES:(pair + 1) * LANES] = y.astype(o_ref.dtype)


def _attn_call(q, k, vt, name):
    b, s, _ = q.shape
    assert ATTN_Q_TILE == ATTN_KV_TILE == TOKEN_TILE
    return pl.pallas_call(
        _attn_kernel,
        grid=(b, s // ATTN_Q_TILE),
        in_specs=[
            pl.BlockSpec((1, ATTN_Q_TILE, HP_WIDTH), lambda bi, i: (bi, i, 0)),
            pl.BlockSpec((1, s, HP_WIDTH), lambda bi, i: (bi, 0, 0)),
            pl.BlockSpec((1,) + vt.shape[1:], lambda bi, i: (bi, 0, 0, 0)),
        ],
        out_specs=pl.BlockSpec((1, ATTN_Q_TILE, V_WIDTH), lambda bi, i: (bi, i, 0)),
        out_shape=jax.ShapeDtypeStruct((b, s, V_WIDTH), jnp.bfloat16),
        scratch_shapes=[
            pltpu.VMEM((HEADS, 1, ATTN_Q_TILE), jnp.float32),
            pltpu.VMEM((HEADS, VT_ROWS, ATTN_Q_TILE), jnp.float32),
        ],
        compiler_params=pltpu.CompilerParams(
            dimension_semantics=("arbitrary", "arbitrary"), vmem_limit_bytes=VMEM_LIMIT),
        name=name,
    )(q, k, vt)


def _merge_kernel(x_ref, ym_ref, yf_ref, gm_ref, gf_ref, bg_ref, wa_ref, wb_ref, wo_ref,
                  gain_ref, wg_ref, wu_ref, wd_ref, o_ref, t_ref):
    mixed = (jax.nn.sigmoid(gm_ref[...] + bg_ref[0:1, :]) * _dot(ym_ref[...], wa_ref[...])
             + jax.nn.sigmoid(gf_ref[...] + bg_ref[1:2, :]) * _dot(yf_ref[...], wb_ref[...]))
    x = x_ref[...] + _dot(_bf16(mixed), wo_ref[...])
    o_ref[...] = _swiglu_residual(x, gain_ref[...], wg_ref, wu_ref, wd_ref, t_ref)


def _merge_call(x, ym, yf, gm, gf, bg, wa, wb, wo, gain, wg, wu, wd):
    n = x.shape[0]
    tile = lambda w: pl.BlockSpec((TOKEN_TILE, w), lambda i: (i, 0))
    consts = (bg, wa, wb, wo, gain, wg, wu, wd)
    return pl.pallas_call(
        _merge_kernel,
        grid=(n // TOKEN_TILE,),
        in_specs=[tile(D_MODEL), tile(V_WIDTH), tile(V_WIDTH), tile(D_MODEL), tile(D_MODEL)]
        + [_const_spec(c.shape) for c in consts],
        out_specs=tile(D_MODEL),
        out_shape=jax.ShapeDtypeStruct((n, D_MODEL), jnp.float32),
        scratch_shapes=[pltpu.VMEM((TOKEN_TILE, FFN_HIDDEN), jnp.bfloat16)],
        compiler_params=pltpu.CompilerParams(
            dimension_semantics=("arbitrary",), vmem_limit_bytes=VMEM_LIMIT),
        name="merge_ffn2",
    )(x, ym, yf, gm, gf, *consts)


def _pad_heads(w, width):
    rows = w.shape[0]
    w = w.reshape(rows, HEADS, -1)
    return jnp.pad(w, ((0, 0), (0, 0), (0, width - w.shape[2]))).reshape(rows, HEADS * width)


def _head_lanes(vec, lane0=0):
    tile = jnp.zeros((HEAD_PAD,), jnp.float32).at[lane0:lane0 + vec.shape[0]].set(vec)
    return jnp.tile(tile, HEADS)[None, :]


def _proj_constants(mix_norm, w_in, q_lat_norm, w_qb, kv_lat_norm, w_kvb, q_nope_gain, q_rope_gain,
                    k_nope_gain, k_rope_gain, fox_q_gain, fox_k_gain, fox_b_f):
    f32 = jnp.float32
    o_qlat, o_kvlat = 0, Q_LORA_RANK
    o_kr = o_kvlat + KV_LORA_RANK
    o_fq = o_kr + MLA_ROPE_DIM
    o_fk = o_fq + V_WIDTH
    o_fv = o_fk + V_WIDTH
    o_fl = o_fv + V_WIDTH
    o_gm = o_fl + HEADS
    o_gf = o_gm + D_MODEL

    def zero_cols(n):
        return jnp.zeros((D_MODEL, n), w_in.dtype)

    win = _bf16(jnp.concatenate([
        w_in[:, o_qlat:o_kvlat], zero_cols(Q_LAT_PAD - Q_LORA_RANK),
        w_in[:, o_kvlat:o_kr],
        w_in[:, o_fl:o_gm], zero_cols(ROPE_LANE0 - HEADS),
        w_in[:, o_kr:o_fq], zero_cols(LANES - ROPE_LANE0 - MLA_ROPE_DIM),
        w_in[:, o_fq:o_fv],
        w_in[:, o_gm:],
    ], axis=1))
    assert win.shape[1] == PROJ_WIDTH

    qk_dim = MLA_NOPE_DIM + MLA_ROPE_DIM
    wqb = _bf16(jnp.pad(_pad_heads(w_qb, HEAD_PAD), ((0, Q_LAT_PAD - Q_LORA_RANK), (0, 0))))
    w_kvb = w_kvb.reshape(KV_LORA_RANK, HEADS, MLA_NOPE_DIM + MLA_V_DIM)
    wkvb = _bf16(w_kvb[:, :, :MLA_NOPE_DIM].reshape(KV_LORA_RANK, V_WIDTH))
    wvt = _bf16(_pad_heads(w_kvb[:, :, MLA_NOPE_DIM:].reshape(KV_LORA_RANK, V_WIDTH), VT_ROWS).T)
    wfvt = _bf16(_pad_heads(w_in[:, o_fv:o_fl], VT_ROWS).T)
    vones = np.zeros((HEADS, VT_ROWS, LANES), np.float32)
    vones[:, VT_ONES_ROW, :] = 1.0
    vones = jnp.asarray(vones.reshape(VT_WIDTH, LANES))

    r = np.arange(TOKEN_TILE)
    tri = jnp.asarray(r[:, None] >= r[None, :], jnp.bfloat16)
    l256 = np.arange(256)
    same_head = (l256[:, None] // HEAD_PAD) == (l256[None, :] // HEAD_PAD)
    in_nope = (l256 % HEAD_PAD) < MLA_NOPE_DIM
    in_rope = ((l256 % HEAD_PAD) >= ROPE_LANE0) & ((l256 % HEAD_PAD) < ROPE_LANE0 + MLA_ROPE_DIM)
    seg_q = same_head & ((in_nope[:, None] & in_nope[None, :]) | (in_rope[:, None] & in_rope[None, :]))
    seg_q = jnp.asarray(seg_q, jnp.bfloat16)
    seg_64 = jnp.asarray((l256[:, None] // FOX_HEAD_DIM) == (l256[None, :] // FOX_HEAD_DIM),
                         jnp.bfloat16)
    eq = np.zeros((LANES, HP_WIDTH), np.float32)
    ek = np.zeros((LANES, HP_WIDTH), np.float32)
    oneq = np.zeros((1, HP_WIDTH), np.float32)
    onek = np.zeros((1, HP_WIDTH), np.float32)
    for hd in range(HEADS):
        for piece in range(3):
            eq[piece * HEADS + hd, hd * HEAD_PAD + BIAS_LANE0 + piece] = 1.0
            ek[piece * HEADS + hd, hd * HEAD_PAD + BIAS_LANE0 + 3 + piece] = -1.0
            oneq[0, hd * HEAD_PAD + BIAS_LANE0 + 3 + piece] = 1.0
            onek[0, hd * HEAD_PAD + BIAS_LANE0 + piece] = 1.0
    eq = jnp.asarray(eq, jnp.bfloat16)
    ek = jnp.asarray(ek, jnp.bfloat16)

    mla_scale = float(qk_dim) ** -0.5 * LOG2E
    fox_scale = float(FOX_HEAD_DIM) ** -0.5 * LOG2E
    q_gain = _head_lanes(jnp.concatenate([q_nope_gain, q_rope_gain]) * mla_scale)
    icnt = np.ones((HEAD_PAD,), np.float32)
    icnt[:MLA_NOPE_DIM] = 1.0 / MLA_NOPE_DIM
    icnt[ROPE_LANE0:ROPE_LANE0 + MLA_ROPE_DIM] = 1.0 / MLA_ROPE_DIM
    q_icnt = jnp.asarray(np.tile(icnt, HEADS)[None, :])
    kn_gain = jnp.tile(k_nope_gain, HEADS)[None, :]
    kr_gain = jnp.zeros((1, LANES), f32).at[0, ROPE_LANE0:ROPE_LANE0 + MLA_ROPE_DIM].set(k_rope_gain)
    fq_gain = jnp.tile(fox_q_gain * fox_scale, HEADS)[None, :]
    fk_gain = jnp.tile(fox_k_gain, HEADS)[None, :]
    bf = jnp.zeros((1, LANES), f32).at[0, :HEADS].set(fox_b_f)
    inv_freq = ROPE_THETA ** (-jnp.arange(ROPE_HALF, dtype=f32) / ROPE_HALF)
    freq = jnp.zeros((1, LANES), f32).at[0, ROPE_LANE0:ROPE_LANE0 + MLA_ROPE_DIM].set(
        jnp.concatenate([inv_freq, inv_freq]))
    qlat_gain = jnp.pad(q_lat_norm, (0, Q_LAT_PAD - Q_LORA_RANK))[None, :]

    return (mix_norm[None, :], win, wqb, wkvb, wvt, wfvt, vones, tri, seg_q, seg_64, eq, ek,
            qlat_gain, kv_lat_norm[None, :], q_gain, q_icnt, kn_gain, kr_gain, fq_gain, fk_gain,
            bf, freq, jnp.asarray(oneq), jnp.asarray(onek))


def kernel(x, positions, ffn1_norm, ffn1_w_gate, ffn1_w_up, ffn1_w_down, mix_norm, w_in, mla_q_lat_norm, mla_w_qb, mla_kv_lat_norm, mla_w_kvb, mla_q_nope_gain, mla_q_rope_gain, mla_k_nope_gain, mla_k_rope_gain, fox_q_gain, fox_k_gain, fox_b_f, w_branch_mla, w_branch_fox, b_gate, w_o, ffn2_norm, ffn2_w_gate, ffn2_w_up, ffn2_w_down):
    b, s, d = x.shape
    n = b * s
    pos = positions.reshape(b, s, 1)
    xt = x.reshape(n, d)
    for l in range(ffn1_norm.shape[0]):
        xt = _ffn_call(xt, ffn1_norm[l][None, :], _bf16(ffn1_w_gate[l]), _bf16(ffn1_w_up[l]),
                       _bf16(ffn1_w_down[l]))
        consts = _proj_constants(
            mix_norm[l], w_in[l], mla_q_lat_norm[l], mla_w_qb[l], mla_kv_lat_norm[l], mla_w_kvb[l],
            mla_q_nope_gain[l], mla_q_rope_gain[l], mla_k_nope_gain[l], mla_k_rope_gain[l],
            fox_q_gain[l], fox_k_gain[l], fox_b_f[l])
        qm, km, vm, qf, kf, vf, gm, gf = _proj_call(xt.reshape(b, s, d), pos, consts)
        ym = _attn_call(qm, km, vm, "attn_mla")
        yf = _attn_call(qf, kf, vf, "attn_fox")
        xt = _merge_call(
            xt, ym.reshape(n, V_WIDTH), yf.reshape(n, V_WIDTH), gm.reshape(n, d), gf.reshape(n, d),
            b_gate[l], _bf16(w_branch_mla[l]), _bf16(w_branch_fox[l]), _bf16(w_o[l]),
            ffn2_norm[l][None, :], _bf16(ffn2_w_gate[l]), _bf16(ffn2_w_up[l]), _bf16(ffn2_w_down[l]))
    return xt.reshape(b, s, d)
```

```python
import numpy as np
import jax
import jax.numpy as jnp
from jax import lax
from jax.experimental import pallas as pl
from jax.experimental.pallas import tpu as pltpu

D_MODEL = 1024
FFN_HIDDEN = 2816
FFN_RESIDUAL_WEIGHT = 0.5
HEADS = 8
MLA_NOPE_DIM = 64
MLA_ROPE_DIM = 32
MLA_V_DIM = 64
Q_LORA_RANK = 192
KV_LORA_RANK = 128
ROPE_THETA = 10000.0
FOX_HEAD_DIM = 64
RMS_EPS = 1e-6

LANES = 128
HEAD_PAD = LANES
HP_WIDTH = HEADS * HEAD_PAD
V_WIDTH = HEADS * MLA_V_DIM
Q_LAT_PAD = 256
ROPE_HALF = MLA_ROPE_DIM // 2
ROPE_LANE0 = MLA_NOPE_DIM
BIAS_LANE0 = FOX_HEAD_DIM

TOKEN_TILE = 512
HIDDEN_CHUNK = 256
GATE_CHUNK = 256
ATTN_Q_TILE = 512
ATTN_KV_TILE = 512
ATTN_KV_SUB = 256
QK_LOOKAHEAD = 2
MASK_VALUE = -1e30
VMEM_LIMIT = 56 * 1024 * 1024

PROJ_QLAT = 0
PROJ_KVLAT = PROJ_QLAT + Q_LAT_PAD
PROJ_KR = PROJ_KVLAT + KV_LORA_RANK
PROJ_FQ = PROJ_KR + LANES
PROJ_FK = PROJ_FQ + V_WIDTH
PROJ_GM = PROJ_FK + V_WIDTH
PROJ_GF = PROJ_GM + D_MODEL
PROJ_WIDTH = PROJ_GF + D_MODEL

VT_ROWS = 80
VT_ONES_ROW = MLA_V_DIM
VT_WIDTH = HEADS * VT_ROWS
NT_DIMS = (((1,), (1,)), ((), ()))
LOG2E = 1.4426950408889634


def _bf16(x):
    return x.astype(jnp.bfloat16)


def _dot(a, b):
    return jnp.dot(a, b, preferred_element_type=jnp.float32)


def _rms_scale(x, n):
    return lax.rsqrt(jnp.sum(x * x, axis=-1, keepdims=True) * (1.0 / n) + RMS_EPS)


def _split2(x):
    hi = _bf16(x)
    return hi, _bf16(x - hi.astype(jnp.float32))


def _split3(x):
    hi = _bf16(x)
    mid, lo = _split2(x - hi.astype(jnp.float32))
    return hi, mid, lo


def _segment_sumsq(x, seg_ref):
    seg = seg_ref[...]
    outs = []
    for c in range(x.shape[1] // 256):
        xs = x[:, c * 256:(c + 1) * 256]
        outs.append(_dot(_bf16(xs * xs), seg))
    return jnp.concatenate(outs, axis=1)


def _swiglu_residual(x, gain, wg_ref, wu_ref, wd_ref, t_ref):
    h = _bf16(x * _rms_scale(x, D_MODEL) * gain)
    for c in range(FFN_HIDDEN // HIDDEN_CHUNK):
        cols = slice(c * HIDDEN_CHUNK, (c + 1) * HIDDEN_CHUNK)
        a = _dot(h, wg_ref[:, cols])
        b = _dot(h, wu_ref[:, cols])
        t_ref[:, cols] = _bf16(a * jax.nn.sigmoid(a) * b)
    return x + FFN_RESIDUAL_WEIGHT * _dot(t_ref[...], wd_ref[...])


def _ffn_kernel(x_ref, gain_ref, wg_ref, wu_ref, wd_ref, o_ref, t_ref):
    o_ref[...] = _swiglu_residual(x_ref[...], gain_ref[...], wg_ref, wu_ref, wd_ref, t_ref)


def _const_spec(shape):
    return pl.BlockSpec(shape, lambda *_: (0,) * len(shape), pipeline_mode=pl.Buffered(1))


def _ffn_call(x, gain, wg, wu, wd):
    n = x.shape[0]
    return pl.pallas_call(
        _ffn_kernel,
        grid=(n // TOKEN_TILE,),
        in_specs=[
            pl.BlockSpec((TOKEN_TILE, D_MODEL), lambda i: (i, 0)),
            _const_spec((1, D_MODEL)),
            _const_spec((D_MODEL, FFN_HIDDEN)),
            _const_spec((D_MODEL, FFN_HIDDEN)),
            _const_spec((FFN_HIDDEN, D_MODEL)),
        ],
        out_specs=pl.BlockSpec((TOKEN_TILE, D_MODEL), lambda i: (i, 0)),
        out_shape=jax.ShapeDtypeStruct((n, D_MODEL), jnp.float32),
        scratch_shapes=[pltpu.VMEM((TOKEN_TILE, FFN_HIDDEN), jnp.bfloat16)],
        compiler_params=pltpu.CompilerParams(
            dimension_semantics=("arbitrary",), vmem_limit_bytes=VMEM_LIMIT),
        name="ffn1",
    )(x, gain, wg, wu, wd)


def _proj_kernel(x_ref, pos_ref, gain_ref, win_ref, wqb_ref, wkvb_ref, wvt_ref, wfvt_ref, vones_ref,
                 tri_ref, segq_ref, seg64_ref, eq_ref, ek_ref,
                 qlat_gain_ref, kvlat_gain_ref, q_gain_ref, q_icnt_ref, kn_gain_ref,
                 kr_gain_ref, fq_gain_ref, fk_gain_ref, bf_ref, freq_ref, oneq_ref, onek_ref,
                 qm_ref, km_ref, vm_ref, qf_ref, kf_ref, vf_ref, gm_ref, gf_ref,
                 carry_ref):
    x = x_ref[0]
    h = _bf16(x * _rms_scale(x, D_MODEL) * gain_ref[...])

    def proj(lo, width):
        return _dot(h, win_ref[:, lo:lo + width])

    gate_chunks = [(ref, col0, c * GATE_CHUNK) for ref, col0 in ((gm_ref, PROJ_GM), (gf_ref, PROJ_GF))
                   for c in range(D_MODEL // GATE_CHUNK)]

    def gates(n):
        for _ in range(min(n, len(gate_chunks))):
            ref, col0, c0 = gate_chunks.pop(0)
            ref[0, :, c0:c0 + GATE_CHUNK] = proj(col0 + c0, GATE_CHUNK)

    gates(2)
    lane =lax.broadcasted_iota(jnp.int32, (1, LANES), 1)
    rope_lanes = (lane >= ROPE_LANE0) & (lane < ROPE_LANE0 + MLA_ROPE_DIM)

    ang = pos_ref[0].astype(jnp.float32) * freq_ref[...]
    cos_t = jnp.cos(ang)
    sin_t = jnp.sin(ang)
    sin_hi = jnp.where(lane >= ROPE_LANE0 + ROPE_HALF, sin_t, 0.0)
    sin_lo = jnp.where(lane < ROPE_LANE0 + ROPE_HALF, -sin_t, 0.0)

    def rotary(t):
        return (t * cos_t + pltpu.roll(t, ROPE_HALF, 1) * sin_hi
                + pltpu.roll(t, LANES - ROPE_HALF, 1) * sin_lo)

    def store_values_t(ref, wt_ref, act):
        vt = lax.dot_general(wt_ref[...], act, NT_DIMS, preferred_element_type=jnp.float32)
        ones = jnp.concatenate([vones_ref[...]] * (TOKEN_TILE // LANES), axis=1)
        ref[0, 0] = _bf16(vt + ones)

    def head_normed(t, g_ref):
        return t * lax.rsqrt(_segment_sumsq(t, seg64_ref) * (1.0 / FOX_HEAD_DIM) + RMS_EPS) * g_ref[...]

    def head_tiles(t):
        low = lane < FOX_HEAD_DIM
        tiles = []
        for pair in range(HEADS // 2):
            src = t[:, pair * LANES:(pair + 1) * LANES]
            tiles.append(jnp.where(low, src, 0.0))
            tiles.append(jnp.where(low, pltpu.roll(src, LANES - FOX_HEAD_DIM, 1), 0.0))
        return tiles

    gates(1)
    q_lat = proj(PROJ_QLAT, Q_LAT_PAD)
    q_lat = _bf16(q_lat * _rms_scale(q_lat, Q_LORA_RANK) * qlat_gain_ref[...])
    q = _dot(q_lat, wqb_ref[...])
    gates(1)
    q = q * lax.rsqrt(_segment_sumsq(q, segq_ref) * q_icnt_ref[...] + RMS_EPS) * q_gain_ref[...]
    for hd in range(HEADS):
        if hd % 2 == 0:
            gates(1)
        cols = slice(hd * HEAD_PAD, (hd + 1) * HEAD_PAD)
        qm_ref[0, :, cols] = _bf16(rotary(q[:, cols]))

    kv_lat = proj(PROJ_KVLAT, KV_LORA_RANK)
    kv_lat = _bf16(kv_lat * _rms_scale(kv_lat, KV_LORA_RANK) * kvlat_gain_ref[...])
    gates(1)
    k_nope = head_tiles(head_normed(_dot(kv_lat, wkvb_ref[...]), kn_gain_ref))
    store_values_t(vm_ref, wvt_ref, kv_lat)

    kr_grp = proj(PROJ_KR, LANES)
    kr = jnp.where(rope_lanes, kr_grp, 0.0)
    kr = rotary(kr * _rms_scale(kr, MLA_ROPE_DIM) * kr_gain_ref[...])
    gates(1)
    for hd in range(HEADS):
        cols = slice(hd * HEAD_PAD, (hd + 1) * HEAD_PAD)
        km_ref[0, :, cols] = _bf16(k_nope[hd] + kr)
    gates(1)

    @pl.when(pl.program_id(1) == 0)
    def _():
        carry_ref[...] = jnp.zeros_like(carry_ref)

    log_f = jnp.where(lane < HEADS, jax.nn.log_sigmoid(kr_grp + bf_ref[...]), 0.0)
    tri = tri_ref[...]
    f_hi, f_lo = _split2(log_f)
    c = _dot(tri, f_hi) + _dot(tri, f_lo) + carry_ref[0:1, :]
    carry_ref[0:1, :] = c[TOKEN_TILE - 1:TOKEN_TILE, :]
    c2 = c * LOG2E
    c_rep = c2 + pltpu.roll(c2, HEADS, 1) + pltpu.roll(c2, 2 * HEADS, 1)
    c_hi, c_mid, c_lo = _split3(c_rep)
    c_parts = jnp.where(lane < HEADS, c_hi, jnp.where(lane < 2 * HEADS, c_mid, c_lo))

    for o_ref, col0, g_ref, e_ref, one_ref in ((qf_ref, PROJ_FQ, fq_gain_ref, eq_ref, oneq_ref),
                                               (kf_ref, PROJ_FK, fk_gain_ref, ek_ref, onek_ref)):
        gates(1)
        tiles = head_tiles(head_normed(proj(col0, V_WIDTH), g_ref))
        gates(1)
        bias = _dot(c_parts, e_ref[...]) + one_ref[...]
        for hd in range(HEADS):
            cols = slice(hd * HEAD_PAD, (hd + 1) * HEAD_PAD)
            o_ref[0, :, cols] = _bf16(tiles[hd] + bias[:, cols])
    store_values_t(vf_ref, wfvt_ref, h)
    gates(len(gate_chunks))


def _proj_call(x, pos, consts):
    b, s, _ = x.shape
    bf16, f32 = jnp.bfloat16, jnp.float32
    n_tiles = s // TOKEN_TILE
    tile = lambda w: pl.BlockSpec((1, TOKEN_TILE, w), lambda bi, i: (bi, i, 0))
    tile_t = pl.BlockSpec((1, 1, VT_WIDTH, TOKEN_TILE), lambda bi, i: (bi, i, 0, 0))
    rows = lambda w, d: jax.ShapeDtypeStruct((b, s, w), d)
    rows_t = jax.ShapeDtypeStruct((b, n_tiles, VT_WIDTH, TOKEN_TILE), bf16)
    return pl.pallas_call(
        _proj_kernel,
        grid=(b, n_tiles),
        in_specs=[tile(D_MODEL), tile(1)] + [_const_spec(c.shape) for c in consts],
        out_specs=[tile(HP_WIDTH), tile(HP_WIDTH), tile_t, tile(HP_WIDTH), tile(HP_WIDTH), tile_t,
                   tile(D_MODEL), tile(D_MODEL)],
        out_shape=[rows(HP_WIDTH, bf16), rows(HP_WIDTH, bf16), rows_t, rows(HP_WIDTH, bf16),
                   rows(HP_WIDTH, bf16), rows_t, rows(D_MODEL, f32), rows(D_MODEL, f32)],
        scratch_shapes=[pltpu.VMEM((8, LANES), jnp.float32)],
        compiler_params=pltpu.CompilerParams(
            dimension_semantics=("arbitrary", "arbitrary"), vmem_limit_bytes=VMEM_LIMIT),
        name="proj",
    )(x, pos, *consts)


def _attn_kernel(q_ref, k_ref, vt_ref, o_ref, m_ref, acc_ref, ahead_ref):
    qi = pl.program_id(1)
    tq, tk, ks = ATTN_Q_TILE, ATTN_KV_TILE, ATTN_KV_SUB
    units = [(hd, sub) for sub in range(tk // ks) for hd in range(HEADS)]

    def logits(t, unit, diagonal):
        hd, sub = unit
        hcols = slice(hd * HEAD_PAD, (hd + 1) * HEAD_PAD)
        q_lo = sub * ks if diagonal else 0
        off = pl.multiple_of(t * tk + sub * ks, ks)
        return lax.dot_general(k_ref[0, pl.ds(off, ks), hcols], q_ref[0, q_lo:, hcols],
                               NT_DIMS, preferred_element_type=jnp.float32)

    def step(t, diagonal, next_t):
        pending = [ahead_ref[j] for j in range(QK_LOOKAHEAD)]
        for i, (hd, sub) in enumerate(units):
            s = pending.pop(0)
            j = i + QK_LOOKAHEAD
            if j < len(units):
                pending.append(logits(t, units[j], diagonal))
            elif next_t is not None:
                ahead_ref[j - len(units)] = logits(next_t, units[j - len(units)], False)
            q_lo = sub * ks if diagonal else 0
            if diagonal:
                key = lax.broadcasted_iota(jnp.int32, s.shape, 0)
                query = lax.broadcasted_iota(jnp.int32, s.shape, 1)
                s = jnp.where(key <= query, s, MASK_VALUE)
            m_prev = m_ref[hd, :, q_lo:]
            m_new = jnp.maximum(m_prev, jnp.max(s, axis=0, keepdims=True))
            alpha = jnp.exp2(m_prev - m_new)
            p = _bf16(jnp.exp2(s - m_new))
            vt = vt_ref[0, t, hd * VT_ROWS:(hd + 1) * VT_ROWS, sub * ks:(sub + 1) * ks]
            acc_ref[hd, :, q_lo:] = alpha * acc_ref[hd, :, q_lo:] + _dot(vt, p)
            m_ref[hd, :, q_lo:] = m_new

    m_ref[...] = jnp.full(m_ref.shape, MASK_VALUE, jnp.float32)
    acc_ref[...] = jnp.zeros(acc_ref.shape, jnp.float32)
    for j in range(QK_LOOKAHEAD):
        ahead_ref[j] = logits(0, units[j], False)

    def body(t, carry):
        step(t, False, t + 1)
        return carry

    lax.fori_loop(0, qi, body, 0)
    step(qi, True, None)

    for pair in range(HEADS // 2):
        y_t = []
        for hd in (2 * pair, 2 * pair + 1):
            acc = acc_ref[hd]
            y_t.append(acc[:MLA_V_DIM] / acc[VT_ONES_ROW:VT_ONES_ROW + 1])
        y = jnp.concatenate(y_t, axis=0).T
        o_ref[0, :, pair * LANES:(pair + 1) * LANES] = y.astype(o_ref.dtype)


def _attn_call(q, k, vt, name):
    b, s, _ = q.shape
    assert ATTN_Q_TILE == ATTN_KV_TILE == TOKEN_TILE
    return pl.pallas_call(
        _attn_kernel,
        grid=(b, s // ATTN_Q_TILE),
        in_specs=[
            pl.BlockSpec((1, ATTN_Q_TILE, HP_WIDTH), lambda bi, i: (bi, i, 0)),
            pl.BlockSpec((1, s, HP_WIDTH), lambda bi, i: (bi, 0, 0)),
            pl.BlockSpec((1,) + vt.shape[1:], lambda bi, i: (bi, 0, 0, 0)),
        ],
        out_specs=pl.BlockSpec((1, ATTN_Q_TILE, V_WIDTH), lambda bi, i: (bi, i, 0)),
        out_shape=jax.ShapeDtypeStruct((b, s, V_WIDTH), jnp.bfloat16),
        scratch_shapes=[
            pltpu.VMEM((HEADS, 1, ATTN_Q_TILE), jnp.float32),
            pltpu.VMEM((HEADS, VT_ROWS, ATTN_Q_TILE), jnp.float32),
            pltpu.VMEM((QK_LOOKAHEAD, ATTN_KV_SUB, ATTN_Q_TILE), jnp.float32),
        ],
        compiler_params=pltpu.CompilerParams(
            dimension_semantics=("arbitrary", "arbitrary"), vmem_limit_bytes=VMEM_LIMIT),
        name=name,
    )(q, k, vt)


def _merge_kernel(x_ref, ym_ref, yf_ref, gm_ref, gf_ref, bg_ref, wa_ref, wb_ref, wo_ref,
                  gain_ref, wg_ref, wu_ref, wd_ref, o_ref, t_ref):
    mixed = (jax.nn.sigmoid(gm_ref[...] + bg_ref[0:1, :]) * _dot(ym_ref[...], wa_ref[...])
             + jax.nn.sigmoid(gf_ref[...] + bg_ref[1:2, :]) * _dot(yf_ref[...], wb_ref[...]))
    x = x_ref[...] + _dot(_bf16(mixed), wo_ref[...])
    o_ref[...] = _swiglu_residual(x, gain_ref[...], wg_ref, wu_ref, wd_ref, t_ref)


def _merge_call(x, ym, yf, gm, gf, bg, wa, wb, wo, gain, wg, wu, wd):
    n = x.shape[0]
    tile = lambda w: pl.BlockSpec((TOKEN_TILE, w), lambda i: (i, 0))
    consts = (bg, wa, wb, wo, gain, wg, wu, wd)
    return pl.pallas_call(
        _merge_kernel,
        grid=(n // TOKEN_TILE,),
        in_specs=[tile(D_MODEL), tile(V_WIDTH), tile(V_WIDTH), tile(D_MODEL), tile(D_MODEL)]
        + [_const_spec(c.shape) for c in consts],
        out_specs=tile(D_MODEL),
        out_shape=jax.ShapeDtypeStruct((n, D_MODEL), jnp.float32),
        scratch_shapes=[pltpu.VMEM((TOKEN_TILE, FFN_HIDDEN), jnp.bfloat16)],
        compiler_params=pltpu.CompilerParams(
            dimension_semantics=("arbitrary",), vmem_limit_bytes=VMEM_LIMIT),
        name="merge_ffn2",
    )(x, ym, yf, gm, gf, *consts)


def _pad_heads(w, width):
    rows = w.shape[0]
    w = w.reshape(rows, HEADS, -1)
    return jnp.pad(w, ((0, 0), (0, 0), (0, width - w.shape[2]))).reshape(rows, HEADS * width)


def _head_lanes(vec, lane0=0):
    tile = jnp.zeros((HEAD_PAD,), jnp.float32).at[lane0:lane0 + vec.shape[0]].set(vec)
    return jnp.tile(tile, HEADS)[None, :]


def _proj_constants(mix_norm, w_in, q_lat_norm, w_qb, kv_lat_norm, w_kvb, q_nope_gain, q_rope_gain,
                    k_nope_gain, k_rope_gain, fox_q_gain, fox_k_gain, fox_b_f):
    f32 = jnp.float32
    o_qlat, o_kvlat = 0, Q_LORA_RANK
    o_kr = o_kvlat + KV_LORA_RANK
    o_fq = o_kr + MLA_ROPE_DIM
    o_fk = o_fq + V_WIDTH
    o_fv = o_fk + V_WIDTH
    o_fl = o_fv + V_WIDTH
    o_gm = o_fl + HEADS
    o_gf = o_gm + D_MODEL

    def zero_cols(n):
        return jnp.zeros((D_MODEL, n), w_in.dtype)

    win = _bf16(jnp.concatenate([
        w_in[:, o_qlat:o_kvlat], zero_cols(Q_LAT_PAD - Q_LORA_RANK),
        w_in[:, o_kvlat:o_kr],
        w_in[:, o_fl:o_gm], zero_cols(ROPE_LANE0 - HEADS),
        w_in[:, o_kr:o_fq], zero_cols(LANES - ROPE_LANE0 - MLA_ROPE_DIM),
        w_in[:, o_fq:o_fv],
        w_in[:, o_gm:],
    ], axis=1))
    assert win.shape[1] == PROJ_WIDTH

    qk_dim = MLA_NOPE_DIM + MLA_ROPE_DIM
    wqb = _bf16(jnp.pad(_pad_heads(w_qb, HEAD_PAD), ((0, Q_LAT_PAD - Q_LORA_RANK), (0, 0))))
    w_kvb = w_kvb.reshape(KV_LORA_RANK, HEADS, MLA_NOPE_DIM + MLA_V_DIM)
    wkvb = _bf16(w_kvb[:, :, :MLA_NOPE_DIM].reshape(KV_LORA_RANK, V_WIDTH))
    wvt = _bf16(_pad_heads(w_kvb[:, :, MLA_NOPE_DIM:].reshape(KV_LORA_RANK, V_WIDTH), VT_ROWS).T)
    wfvt = _bf16(_pad_heads(w_in[:, o_fv:o_fl], VT_ROWS).T)
    vones = np.zeros((HEADS, VT_ROWS, LANES), np.float32)
    vones[:, VT_ONES_ROW, :] = 1.0
    vones = jnp.asarray(vones.reshape(VT_WIDTH, LANES))

    r = np.arange(TOKEN_TILE)
    tri = jnp.asarray(r[:, None] >= r[None, :], jnp.bfloat16)
    l256 = np.arange(256)
    same_head = (l256[:, None] // HEAD_PAD) == (l256[None, :] // HEAD_PAD)
    in_nope = (l256 % HEAD_PAD) < MLA_NOPE_DIM
    in_rope = ((l256 % HEAD_PAD) >= ROPE_LANE0) & ((l256 % HEAD_PAD) < ROPE_LANE0 + MLA_ROPE_DIM)
    seg_q = same_head & ((in_nope[:, None] & in_nope[None, :]) | (in_rope[:, None] & in_rope[None, :]))
    seg_q = jnp.asarray(seg_q, jnp.bfloat16)
    seg_64 = jnp.asarray((l256[:, None] // FOX_HEAD_DIM) == (l256[None, :] // FOX_HEAD_DIM),
                         jnp.bfloat16)
    eq = np.zeros((LANES, HP_WIDTH), np.float32)
    ek = np.zeros((LANES, HP_WIDTH), np.float32)
    oneq = np.zeros((1, HP_WIDTH), np.float32)
    onek = np.zeros((1, HP_WIDTH), np.float32)
    for hd in range(HEADS):
        for piece in range(3):
            eq[piece * HEADS + hd, hd * HEAD_PAD + BIAS_LANE0 + piece] = 1.0
            ek[piece * HEADS + hd, hd * HEAD_PAD + BIAS_LANE0 + 3 + piece] = -1.0
            oneq[0, hd * HEAD_PAD + BIAS_LANE0 + 3 + piece] = 1.0
            onek[0, hd * HEAD_PAD + BIAS_LANE0 + piece] = 1.0
    eq = jnp.asarray(eq, jnp.bfloat16)
    ek = jnp.asarray(ek, jnp.bfloat16)

    mla_scale = float(qk_dim) ** -0.5 * LOG2E
    fox_scale = float(FOX_HEAD_DIM) ** -0.5 * LOG2E
    q_gain = _head_lanes(jnp.concatenate([q_nope_gain, q_rope_gain]) * mla_scale)
    icnt = np.ones((HEAD_PAD,), np.float32)
    icnt[:MLA_NOPE_DIM] = 1.0 / MLA_NOPE_DIM
    icnt[ROPE_LANE0:ROPE_LANE0 + MLA_ROPE_DIM] = 1.0 / MLA_ROPE_DIM
    q_icnt = jnp.asarray(np.tile(icnt, HEADS)[None, :])
    kn_gain = jnp.tile(k_nope_gain, HEADS)[None, :]
    kr_gain = jnp.zeros((1, LANES), f32).at[0, ROPE_LANE0:ROPE_LANE0 + MLA_ROPE_DIM].set(k_rope_gain)
    fq_gain = jnp.tile(fox_q_gain * fox_scale, HEADS)[None, :]
    fk_gain = jnp.tile(fox_k_gain, HEADS)[None, :]
    bf = jnp.zeros((1, LANES), f32).at[0, :HEADS].set(fox_b_f)
    inv_freq = ROPE_THETA ** (-jnp.arange(ROPE_HALF, dtype=f32) / ROPE_HALF)
    freq = jnp.zeros((1, LANES), f32).at[0, ROPE_LANE0:ROPE_LANE0 + MLA_ROPE_DIM].set(
        jnp.concatenate([inv_freq, inv_freq]))
    qlat_gain = jnp.pad(q_lat_norm, (0, Q_LAT_PAD - Q_LORA_RANK))[None, :]

    return (mix_norm[None, :], win, wqb, wkvb, wvt, wfvt, vones, tri, seg_q, seg_64, eq, ek,
            qlat_gain, kv_lat_norm[None, :], q_gain, q_icnt, kn_gain, kr_gain, fq_gain, fk_gain,
            bf, freq, jnp.asarray(oneq), jnp.asarray(onek))


def kernel(x, positions, ffn1_norm, ffn1_w_gate, ffn1_w_up, ffn1_w_down, mix_norm, w_in, mla_q_lat_norm, mla_w_qb, mla_kv_lat_norm, mla_w_kvb, mla_q_nope_gain, mla_q_rope_gain, mla_k_nope_gain, mla_k_rope_gain, fox_q_gain, fox_k_gain, fox_b_f, w_branch_mla, w_branch_fox, b_gate, w_o, ffn2_norm, ffn2_w_gate, ffn2_w_up, ffn2_w_down):
    b, s, d = x.shape
    n = b * s
    pos = positions.reshape(b, s, 1)
    xt = x.reshape(n, d)
    for l in range(ffn1_norm.shape[0]):
        xt = _ffn_call(xt, ffn1_norm[l][None, :], _bf16(ffn1_w_gate[l]), _bf16(ffn1_w_up[l]),
                       _bf16(ffn1_w_down[l]))
        consts = _proj_constants(
            mix_norm[l], w_in[l], mla_q_lat_norm[l], mla_w_qb[l], mla_kv_lat_norm[l], mla_w_kvb[l],
            mla_q_nope_gain[l], mla_q_rope_gain[l], mla_k_nope_gain[l], mla_k_rope_gain[l],
            fox_q_gain[l], fox_k_gain[l], fox_b_f[l])
        qm, km, vm, qf, kf, vf, gm, gf = _proj_call(xt.reshape(b, s, d), pos, consts)
        ym = _attn_call(qm, km, vm, "attn_mla")
        yf = _attn_call(qf, kf, vf, "attn_fox")
        xt = _merge_call(
            xt, ym.reshape(n, V_WIDTH), yf.reshape(n, V_WIDTH), gm.reshape(n, d), gf.reshape(n, d),
            b_gate[l], _bf16(w_branch_mla[l]), _bf16(w_branch_fox[l]), _bf16(w_o[l]),
            ffn2_norm[l][None, :], _bf16(ffn2_w_gate[l]), _bf16(ffn2_w_up[l]), _bf16(ffn2_w_down[l]))
    return xt.reshape(b, s, d)
```

```python
import functools

import numpy as np
import jax
import jax.numpy as jnp
from jax import lax
from jax.experimental import pallas as pl
from jax.experimental.pallas import tpu as pltpu

D_MODEL = 1024
FFN_HIDDEN = 2816
FFN_RESIDUAL_WEIGHT = 0.5
HEADS = 8
MLA_NOPE_DIM = 64
MLA_ROPE_DIM = 32
MLA_V_DIM = 64
Q_LORA_RANK = 192
KV_LORA_RANK = 128
ROPE_THETA = 10000.0
FOX_HEAD_DIM = 64
RMS_EPS = 1e-6

LANES = 128
HEAD_PAD = LANES
HP_WIDTH = HEADS * HEAD_PAD
V_WIDTH = HEADS * MLA_V_DIM
Q_LAT_PAD = 256
ROPE_HALF = MLA_ROPE_DIM // 2
ROPE_LANE0 = MLA_NOPE_DIM
BIAS_LANE0 = FOX_HEAD_DIM
FOX_SHIFT_LANE = BIAS_LANE0 + 6
MLA_SHIFT_LANE = ROPE_LANE0 + MLA_ROPE_DIM
SHIFT_MARGIN = 40.0
SHIFT_BOUND_LIMIT = 68.0
BOUND_SLACK = 1.05

TOKEN_TILE = 512
HIDDEN_CHUNK = 256
GATE_CHUNK = 256
ATTN_Q_TILE = 512
ATTN_KV_TILE = 512
ATTN_KV_SUB = 256
QK_LOOKAHEAD = 2
MASK_VALUE = -1e30
VMEM_LIMIT = 56 * 1024 * 1024

PROJ_QLAT = 0
PROJ_KVLAT = PROJ_QLAT + Q_LAT_PAD
PROJ_KR = PROJ_KVLAT + KV_LORA_RANK
PROJ_FQ = PROJ_KR + LANES
PROJ_FK = PROJ_FQ + V_WIDTH
PROJ_GM = PROJ_FK + V_WIDTH
PROJ_GF = PROJ_GM + D_MODEL
PROJ_WIDTH = PROJ_GF + D_MODEL

VT_ROWS = 80
VT_ONES_ROW = MLA_V_DIM
VT_WIDTH = HEADS * VT_ROWS
NT_DIMS = (((1,), (1,)), ((), ()))
LOG2E = 1.4426950408889634


def _bf16(x):
    return x.astype(jnp.bfloat16)


def _dot(a, b):
    return jnp.dot(a, b, preferred_element_type=jnp.float32)


def _rms_scale(x, n):
    return lax.rsqrt(jnp.sum(x * x, axis=-1, keepdims=True) * (1.0 / n) + RMS_EPS)


def _split2(x):
    hi = _bf16(x)
    return hi, _bf16(x - hi.astype(jnp.float32))


def _split3(x):
    hi = _bf16(x)
    mid, lo = _split2(x - hi.astype(jnp.float32))
    return hi, mid, lo


def _segment_sumsq(x, seg_ref):
    seg = seg_ref[...]
    outs = []
    for c in range(x.shape[1] // 256):
        xs = x[:, c * 256:(c + 1) * 256]
        outs.append(_dot(_bf16(xs * xs), seg))
    return jnp.concatenate(outs, axis=1)


def _swiglu_residual(x, gain, wg_ref, wu_ref, wd_ref, t_ref):
    h = _bf16(x * _rms_scale(x, D_MODEL) * gain)
    for c in range(FFN_HIDDEN // HIDDEN_CHUNK):
        cols = slice(c * HIDDEN_CHUNK, (c + 1) * HIDDEN_CHUNK)
        a = _dot(h, wg_ref[:, cols])
        b = _dot(h, wu_ref[:, cols])
        t_ref[:, cols] = _bf16(a * jax.nn.sigmoid(a) * b)
    return x + FFN_RESIDUAL_WEIGHT * _dot(t_ref[...], wd_ref[...])


def _ffn_kernel(x_ref, gain_ref, wg_ref, wu_ref, wd_ref, o_ref, t_ref):
    o_ref[...] = _swiglu_residual(x_ref[...], gain_ref[...], wg_ref, wu_ref, wd_ref, t_ref)


def _const_spec(shape):
    return pl.BlockSpec(shape, lambda *_: (0,) * len(shape), pipeline_mode=pl.Buffered(1))


def _ffn_call(x, gain, wg, wu, wd):
    n = x.shape[0]
    return pl.pallas_call(
        _ffn_kernel,
        grid=(n // TOKEN_TILE,),
        in_specs=[
            pl.BlockSpec((TOKEN_TILE, D_MODEL), lambda i: (i, 0)),
            _const_spec((1, D_MODEL)),
            _const_spec((D_MODEL, FFN_HIDDEN)),
            _const_spec((D_MODEL, FFN_HIDDEN)),
            _const_spec((FFN_HIDDEN, D_MODEL)),
        ],
        out_specs=pl.BlockSpec((TOKEN_TILE, D_MODEL), lambda i: (i, 0)),
        out_shape=jax.ShapeDtypeStruct((n, D_MODEL), jnp.float32),
        scratch_shapes=[pltpu.VMEM((TOKEN_TILE, FFN_HIDDEN), jnp.bfloat16)],
        compiler_params=pltpu.CompilerParams(
            dimension_semantics=("arbitrary",), vmem_limit_bytes=VMEM_LIMIT),
        name="ffn1",
    )(x, gain, wg, wu, wd)


def _proj_kernel(x_ref, pos_ref, gain_ref, win_ref, wqb_ref, wkvb_ref, wvt_ref, wfvt_ref, vones_ref,
                 tri_ref, segq_ref, seg64_ref, eq_ref, ek_ref,
                 qlat_gain_ref, kvlat_gain_ref, q_gain_ref, q_icnt_ref, kn_gain_ref,
                 kr_gain_ref, fq_gain_ref, fk_gain_ref, bf_ref, freq_ref, oneq_ref, onek_ref,
                 qshift_ref, kone_ref,
                 qm_ref, km_ref, vm_ref, qf_ref, kf_ref, vf_ref, gm_ref, gf_ref,
                 carry_ref):
    x = x_ref[0]
    h = _bf16(x * _rms_scale(x, D_MODEL) * gain_ref[...])

    def proj(lo, width):
        return _dot(h, win_ref[:, lo:lo + width])

    gate_chunks = [(ref, col0, c * GATE_CHUNK) for ref, col0 in ((gm_ref, PROJ_GM), (gf_ref, PROJ_GF))
                   for c in range(D_MODEL // GATE_CHUNK)]

    def gates(n):
        for _ in range(min(n, len(gate_chunks))):
            ref, col0, c0 = gate_chunks.pop(0)
            ref[0, :, c0:c0 + GATE_CHUNK] = proj(col0 + c0, GATE_CHUNK)

    gates(2)
    lane =lax.broadcasted_iota(jnp.int32, (1, LANES), 1)
    rope_lanes = (lane >= ROPE_LANE0) & (lane < ROPE_LANE0 + MLA_ROPE_DIM)

    ang = pos_ref[0].astype(jnp.float32) * freq_ref[...]
    cos_t = jnp.cos(ang)
    sin_t = jnp.sin(ang)
    sin_hi = jnp.where(lane >= ROPE_LANE0 + ROPE_HALF, sin_t, 0.0)
    sin_lo = jnp.where(lane < ROPE_LANE0 + ROPE_HALF, -sin_t, 0.0)

    def rotary(t):
        return (t * cos_t + pltpu.roll(t, ROPE_HALF, 1) * sin_hi
                + pltpu.roll(t, LANES - ROPE_HALF, 1) * sin_lo)

    def store_values_t(ref, wt_ref, act):
        vt = lax.dot_general(wt_ref[...], act, NT_DIMS, preferred_element_type=jnp.float32)
        ones = jnp.concatenate([vones_ref[...]] * (TOKEN_TILE // LANES), axis=1)
        ref[0, 0] = _bf16(vt + ones)

    def head_normed(t, g_ref):
        return t * lax.rsqrt(_segment_sumsq(t, seg64_ref) * (1.0 / FOX_HEAD_DIM) + RMS_EPS) * g_ref[...]

    def head_tiles(t):
        low = lane < FOX_HEAD_DIM
        tiles = []
        for pair in range(HEADS // 2):
            src = t[:, pair * LANES:(pair + 1) * LANES]
            tiles.append(jnp.where(low, src, 0.0))
            tiles.append(jnp.where(low, pltpu.roll(src, LANES - FOX_HEAD_DIM, 1), 0.0))
        return tiles

    gates(1)
    q_lat = proj(PROJ_QLAT, Q_LAT_PAD)
    q_lat = _bf16(q_lat * _rms_scale(q_lat, Q_LORA_RANK) * qlat_gain_ref[...])
    q = _dot(q_lat, wqb_ref[...])
    gates(1)
    q = q * lax.rsqrt(_segment_sumsq(q, segq_ref) * q_icnt_ref[...] + RMS_EPS) * q_gain_ref[...]
    for hd in range(HEADS):
        if hd % 2 == 0:
            gates(1)
        cols = slice(hd * HEAD_PAD, (hd + 1) * HEAD_PAD)
        qm_ref[0, :, cols] = _bf16(rotary(q[:, cols]) + qshift_ref[...])

    kv_lat = proj(PROJ_KVLAT, KV_LORA_RANK)
    kv_lat = _bf16(kv_lat * _rms_scale(kv_lat, KV_LORA_RANK) * kvlat_gain_ref[...])
    gates(1)
    k_nope = head_tiles(head_normed(_dot(kv_lat, wkvb_ref[...]), kn_gain_ref))
    store_values_t(vm_ref, wvt_ref, kv_lat)

    kr_grp = proj(PROJ_KR, LANES)
    kr = jnp.where(rope_lanes, kr_grp, 0.0)
    kr = rotary(kr * _rms_scale(kr, MLA_ROPE_DIM) * kr_gain_ref[...]) + kone_ref[...]
    gates(1)
    for hd in range(HEADS):
        cols = slice(hd * HEAD_PAD, (hd + 1) * HEAD_PAD)
        km_ref[0, :, cols] = _bf16(k_nope[hd] + kr)
    gates(1)

    @pl.when(pl.program_id(1) == 0)
    def _():
        carry_ref[...] = jnp.zeros_like(carry_ref)

    log_f = jnp.where(lane < HEADS, jax.nn.log_sigmoid(kr_grp + bf_ref[...]), 0.0)
    tri = tri_ref[...]
    f_hi, f_lo = _split2(log_f)
    c = _dot(tri, f_hi) + _dot(tri, f_lo) + carry_ref[0:1, :]
    carry_ref[0:1, :] = c[TOKEN_TILE - 1:TOKEN_TILE, :]
    c2 = c * LOG2E
    c_rep = c2 + pltpu.roll(c2, HEADS, 1) + pltpu.roll(c2, 2 * HEADS, 1)
    c_hi, c_mid, c_lo = _split3(c_rep)
    c_parts = jnp.where(lane < HEADS, c_hi, jnp.where(lane < 2 * HEADS, c_mid, c_lo))

    for o_ref, col0, g_ref, e_ref, one_ref in ((qf_ref, PROJ_FQ, fq_gain_ref, eq_ref, oneq_ref),
                                               (kf_ref, PROJ_FK, fk_gain_ref, ek_ref, onek_ref)):
        gates(1)
        tiles = head_tiles(head_normed(proj(col0, V_WIDTH), g_ref))
        gates(1)
        bias = _dot(c_parts, e_ref[...]) + one_ref[...]
        for hd in range(HEADS):
            cols = slice(hd * HEAD_PAD, (hd + 1) * HEAD_PAD)
            o_ref[0, :, cols] = _bf16(tiles[hd] + bias[:, cols])
    store_values_t(vf_ref, wfvt_ref, h)
    gates(len(gate_chunks))


def _proj_call(x, pos, consts):
    b, s, _ = x.shape
    bf16, f32 = jnp.bfloat16, jnp.float32
    n_tiles = s // TOKEN_TILE
    tile = lambda w: pl.BlockSpec((1, TOKEN_TILE, w), lambda bi, i: (bi, i, 0))
    tile_t = pl.BlockSpec((1, 1, VT_WIDTH, TOKEN_TILE), lambda bi, i: (bi, i, 0, 0))
    rows = lambda w, d: jax.ShapeDtypeStruct((b, s, w), d)
    rows_t = jax.ShapeDtypeStruct((b, n_tiles, VT_WIDTH, TOKEN_TILE), bf16)
    return pl.pallas_call(
        _proj_kernel,
        grid=(b, n_tiles),
        in_specs=[tile(D_MODEL), tile(1)] + [_const_spec(c.shape) for c in consts],
        out_specs=[tile(HP_WIDTH), tile(HP_WIDTH), tile_t, tile(HP_WIDTH), tile(HP_WIDTH), tile_t,
                   tile(D_MODEL), tile(D_MODEL)],
        out_shape=[rows(HP_WIDTH, bf16), rows(HP_WIDTH, bf16), rows_t, rows(HP_WIDTH, bf16),
                   rows(HP_WIDTH, bf16), rows_t, rows(D_MODEL, f32), rows(D_MODEL, f32)],
        scratch_shapes=[pltpu.VMEM((8, LANES), jnp.float32)],
        compiler_params=pltpu.CompilerParams(
            dimension_semantics=("arbitrary", "arbitrary"), vmem_limit_bytes=VMEM_LIMIT),
        name="proj",
    )(x, pos, *consts)


def _attn_kernel(q_ref, k_ref, vt_ref, o_ref, m_ref, acc_ref, ahead_ref, *, running_max):
    qi = pl.program_id(1)
    tq, tk, ks = ATTN_Q_TILE, ATTN_KV_TILE, ATTN_KV_SUB
    units = [(hd, sub) for sub in range(tk // ks) for hd in range(HEADS)]

    def logits(t, unit, diagonal):
        hd, sub = unit
        hcols = slice(hd * HEAD_PAD, (hd + 1) * HEAD_PAD)
        q_lo = sub * ks if diagonal else 0
        off = pl.multiple_of(t * tk + sub * ks, ks)
        return lax.dot_general(k_ref[0, pl.ds(off, ks), hcols], q_ref[0, q_lo:, hcols],
                               NT_DIMS, preferred_element_type=jnp.float32)

    def step(t, diagonal, next_t):
        pending = [ahead_ref[j] for j in range(QK_LOOKAHEAD)]
        for i, (hd, sub) in enumerate(units):
            s = pending.pop(0)
            j = i + QK_LOOKAHEAD
            if j < len(units):
                pending.append(logits(t, units[j], diagonal))
            elif next_t is not None:
                ahead_ref[j - len(units)] = logits(next_t, units[j - len(units)], False)
            q_lo = sub * ks if diagonal else 0
            if diagonal:
                key = lax.broadcasted_iota(jnp.int32, s.shape, 0)
                query = lax.broadcasted_iota(jnp.int32, s.shape, 1)
                s = jnp.where(key <= query, s, MASK_VALUE)
            vt = vt_ref[0, t, hd * VT_ROWS:(hd + 1) * VT_ROWS, sub * ks:(sub + 1) * ks]
            if running_max:
                m_prev = m_ref[hd, :, q_lo:]
                m_new = jnp.maximum(m_prev, jnp.max(s, axis=0, keepdims=True))
                alpha = jnp.exp2(m_prev - m_new)
                p = _bf16(jnp.exp2(s - m_new))
                acc_ref[hd, :, q_lo:] = alpha * acc_ref[hd, :, q_lo:] + _dot(vt, p)
                m_ref[hd, :, q_lo:] = m_new
            else:
                acc_ref[hd, :, q_lo:] += _dot(vt, _bf16(jnp.exp2(s)))

    if running_max:
        m_ref[...] = jnp.full(m_ref.shape, MASK_VALUE, jnp.float32)
    acc_ref[...] = jnp.zeros(acc_ref.shape, jnp.float32)
    for j in range(QK_LOOKAHEAD):
        ahead_ref[j] = logits(0, units[j], False)

    def body(t, carry):
        step(t, False, t + 1)
        return carry

    lax.fori_loop(0, qi, body, 0)
    step(qi, True, None)

    for pair in range(HEADS // 2):
        y_t = []
        for hd in (2 * pair, 2 * pair + 1):
            acc = acc_ref[hd]
            y_t.append(acc[:MLA_V_DIM] / acc[VT_ONES_ROW:VT_ONES_ROW + 1])
        y = jnp.concatenate(y_t, axis=0).T
        o_ref[0, :, pair * LANES:(pair + 1) * LANES] = y.astype(o_ref.dtype)


def _attn_call(q, k, vt, *, running_max, name):
    b, s, _ = q.shape
    assert ATTN_Q_TILE == ATTN_KV_TILE == TOKEN_TILE
    return pl.pallas_call(
        functools.partial(_attn_kernel, running_max=running_max),
        grid=(b, s // ATTN_Q_TILE),
        in_specs=[
            pl.BlockSpec((1, ATTN_Q_TILE, HP_WIDTH), lambda bi, i: (bi, i, 0)),
            pl.BlockSpec((1, s, HP_WIDTH), lambda bi, i: (bi, 0, 0)),
            pl.BlockSpec((1,) + vt.shape[1:], lambda bi, i: (bi, 0, 0, 0)),
        ],
        out_specs=pl.BlockSpec((1, ATTN_Q_TILE, V_WIDTH), lambda bi, i: (bi, i, 0)),
        out_shape=jax.ShapeDtypeStruct((b, s, V_WIDTH), jnp.bfloat16),
        scratch_shapes=[
            pltpu.VMEM((HEADS, 1, ATTN_Q_TILE), jnp.float32),
            pltpu.VMEM((HEADS, VT_ROWS, ATTN_Q_TILE), jnp.float32),
            pltpu.VMEM((QK_LOOKAHEAD, ATTN_KV_SUB, ATTN_Q_TILE), jnp.float32),
        ],
        compiler_params=pltpu.CompilerParams(
            dimension_semantics=("arbitrary", "arbitrary"), vmem_limit_bytes=VMEM_LIMIT),
        name=name,
    )(q, k, vt)


def _attention(q, k, vt, shift_is_safe, name):
    return lax.cond(
        shift_is_safe,
        functools.partial(_attn_call, running_max=False, name=name + "_shifted"),
        functools.partial(_attn_call, running_max=True, name=name),
        q, k, vt)


def _merge_kernel(x_ref, ym_ref, yf_ref, gm_ref, gf_ref, bg_ref, wa_ref, wb_ref, wo_ref,
                  gain_ref, wg_ref, wu_ref, wd_ref, o_ref, t_ref):
    mixed = (jax.nn.sigmoid(gm_ref[...] + bg_ref[0:1, :]) * _dot(ym_ref[...], wa_ref[...])
             + jax.nn.sigmoid(gf_ref[...] + bg_ref[1:2, :]) * _dot(yf_ref[...], wb_ref[...]))
    x = x_ref[...] + _dot(_bf16(mixed), wo_ref[...])
    o_ref[...] = _swiglu_residual(x, gain_ref[...], wg_ref, wu_ref, wd_ref, t_ref)


def _merge_call(x, ym, yf, gm, gf, bg, wa, wb, wo, gain, wg, wu, wd):
    n = x.shape[0]
    tile = lambda w: pl.BlockSpec((TOKEN_TILE, w), lambda i: (i, 0))
    consts = (bg, wa, wb, wo, gain, wg, wu, wd)
    return pl.pallas_call(
        _merge_kernel,
        grid=(n // TOKEN_TILE,),
        in_specs=[tile(D_MODEL), tile(V_WIDTH), tile(V_WIDTH), tile(D_MODEL), tile(D_MODEL)]
        + [_const_spec(c.shape) for c in consts],
        out_specs=tile(D_MODEL),
        out_shape=jax.ShapeDtypeStruct((n, D_MODEL), jnp.float32),
        scratch_shapes=[pltpu.VMEM((TOKEN_TILE, FFN_HIDDEN), jnp.bfloat16)],
        compiler_params=pltpu.CompilerParams(
            dimension_semantics=("arbitrary",), vmem_limit_bytes=VMEM_LIMIT),
        name="merge_ffn2",
    )(x, ym, yf, gm, gf, *consts)


def _pad_heads(w, width):
    rows = w.shape[0]
    w = w.reshape(rows, HEADS, -1)
    return jnp.pad(w, ((0, 0), (0, 0), (0, width - w.shape[2]))).reshape(rows, HEADS * width)


def _head_lanes(vec, lane0=0):
    tile = jnp.zeros((HEAD_PAD,), jnp.float32).at[lane0:lane0 + vec.shape[0]].set(vec)
    return jnp.tile(tile, HEADS)[None, :]


def _proj_constants(mix_norm, w_in, q_lat_norm, w_qb, kv_lat_norm, w_kvb, q_nope_gain, q_rope_gain,
                    k_nope_gain, k_rope_gain, fox_q_gain, fox_k_gain, fox_b_f):
    f32 = jnp.float32
    o_qlat, o_kvlat = 0, Q_LORA_RANK
    o_kr = o_kvlat + KV_LORA_RANK
    o_fq = o_kr + MLA_ROPE_DIM
    o_fk = o_fq + V_WIDTH
    o_fv = o_fk + V_WIDTH
    o_fl = o_fv + V_WIDTH
    o_gm = o_fl + HEADS
    o_gf = o_gm + D_MODEL

    def zero_cols(n):
        return jnp.zeros((D_MODEL, n), w_in.dtype)

    win = _bf16(jnp.concatenate([
        w_in[:, o_qlat:o_kvlat], zero_cols(Q_LAT_PAD - Q_LORA_RANK),
        w_in[:, o_kvlat:o_kr],
        w_in[:, o_fl:o_gm], zero_cols(ROPE_LANE0 - HEADS),
        w_in[:, o_kr:o_fq], zero_cols(LANES - ROPE_LANE0 - MLA_ROPE_DIM),
        w_in[:, o_fq:o_fv],
        w_in[:, o_gm:],
    ], axis=1))
    assert win.shape[1] == PROJ_WIDTH

    qk_dim = MLA_NOPE_DIM + MLA_ROPE_DIM
    wqb = _bf16(jnp.pad(_pad_heads(w_qb, HEAD_PAD), ((0, Q_LAT_PAD - Q_LORA_RANK), (0, 0))))
    w_kvb = w_kvb.reshape(KV_LORA_RANK, HEADS, MLA_NOPE_DIM + MLA_V_DIM)
    wkvb = _bf16(w_kvb[:, :, :MLA_NOPE_DIM].reshape(KV_LORA_RANK, V_WIDTH))
    wvt = _bf16(_pad_heads(w_kvb[:, :, MLA_NOPE_DIM:].reshape(KV_LORA_RANK, V_WIDTH), VT_ROWS).T)
    wfvt = _bf16(_pad_heads(w_in[:, o_fv:o_fl], VT_ROWS).T)
    vones = np.zeros((HEADS, VT_ROWS, LANES), np.float32)
    vones[:, VT_ONES_ROW, :] = 1.0
    vones = jnp.asarray(vones.reshape(VT_WIDTH, LANES))

    r = np.arange(TOKEN_TILE)
    tri = jnp.asarray(r[:, None] >= r[None, :], jnp.bfloat16)
    l256 = np.arange(256)
    same_head = (l256[:, None] // HEAD_PAD) == (l256[None, :] // HEAD_PAD)
    in_nope = (l256 % HEAD_PAD) < MLA_NOPE_DIM
    in_rope = ((l256 % HEAD_PAD) >= ROPE_LANE0) & ((l256 % HEAD_PAD) < ROPE_LANE0 + MLA_ROPE_DIM)
    seg_q = same_head & ((in_nope[:, None] & in_nope[None, :]) | (in_rope[:, None] & in_rope[None, :]))
    seg_q = jnp.asarray(seg_q, jnp.bfloat16)
    seg_64 = jnp.asarray((l256[:, None] // FOX_HEAD_DIM) == (l256[None, :] // FOX_HEAD_DIM),
                         jnp.bfloat16)
    eq = np.zeros((LANES, HP_WIDTH), np.float32)
    ek = np.zeros((LANES, HP_WIDTH), np.float32)
    oneq = np.zeros((1, HP_WIDTH), np.float32)
    onek = np.zeros((1, HP_WIDTH), np.float32)
    for hd in range(HEADS):
        for piece in range(3):
            eq[piece * HEADS + hd, hd * HEAD_PAD + BIAS_LANE0 + piece] = 1.0
            ek[piece * HEADS + hd, hd * HEAD_PAD + BIAS_LANE0 + 3 + piece] = -1.0
            oneq[0, hd * HEAD_PAD + BIAS_LANE0 + 3 + piece] = 1.0
            onek[0, hd * HEAD_PAD + BIAS_LANE0 + piece] = 1.0
    eq = jnp.asarray(eq, jnp.bfloat16)
    ek = jnp.asarray(ek, jnp.bfloat16)

    mla_scale = float(qk_dim) ** -0.5 * LOG2E
    fox_scale = float(FOX_HEAD_DIM) ** -0.5 * LOG2E
    q_gain = _head_lanes(jnp.concatenate([q_nope_gain, q_rope_gain]) * mla_scale)
    icnt = np.ones((HEAD_PAD,), np.float32)
    icnt[:MLA_NOPE_DIM] = 1.0 / MLA_NOPE_DIM
    icnt[ROPE_LANE0:ROPE_LANE0 + MLA_ROPE_DIM] = 1.0 / MLA_ROPE_DIM
    q_icnt = jnp.asarray(np.tile(icnt, HEADS)[None, :])
    kn_gain = jnp.tile(k_nope_gain, HEADS)[None, :]
    kr_gain = jnp.zeros((1, LANES), f32).at[0, ROPE_LANE0:ROPE_LANE0 + MLA_ROPE_DIM].set(k_rope_gain)
    fq_gain = jnp.tile(fox_q_gain * fox_scale, HEADS)[None, :]
    fk_gain = jnp.tile(fox_k_gain, HEADS)[None, :]
    bf = jnp.zeros((1, LANES), f32).at[0, :HEADS].set(fox_b_f)
    inv_freq = ROPE_THETA ** (-jnp.arange(ROPE_HALF, dtype=f32) / ROPE_HALF)
    freq = jnp.zeros((1, LANES), f32).at[0, ROPE_LANE0:ROPE_LANE0 + MLA_ROPE_DIM].set(
        jnp.concatenate([inv_freq, inv_freq]))
    qlat_gain = jnp.pad(q_lat_norm, (0, Q_LAT_PAD - Q_LORA_RANK))[None, :]

    amax = lambda g: jnp.max(jnp.abs(g))
    bound_fox = (FOX_HEAD_DIM * amax(fox_q_gain) * amax(fox_k_gain) * fox_scale) * BOUND_SLACK
    q_sq = MLA_NOPE_DIM * amax(q_nope_gain) ** 2 + MLA_ROPE_DIM * amax(q_rope_gain) ** 2
    k_sq = MLA_NOPE_DIM * amax(k_nope_gain) ** 2 + MLA_ROPE_DIM * amax(k_rope_gain) ** 2
    bound_mla = jnp.sqrt(q_sq * k_sq) * mla_scale * BOUND_SLACK
    shift_lane = np.zeros((1, LANES), np.float32)
    shift_lane[0, MLA_SHIFT_LANE] = 1.0
    kone = jnp.asarray(shift_lane)
    qshift = kone * -(bound_mla - SHIFT_MARGIN)
    fox_shift_lanes = np.zeros((1, HP_WIDTH), np.float32)
    fox_shift_lanes[0, FOX_SHIFT_LANE::HEAD_PAD] = 1.0
    oneq = jnp.asarray(oneq) + jnp.asarray(fox_shift_lanes) * -(bound_fox - SHIFT_MARGIN)
    onek = jnp.asarray(onek + fox_shift_lanes)

    consts = (mix_norm[None, :], win, wqb, wkvb, wvt, wfvt, vones, tri, seg_q, seg_64, eq, ek,
              qlat_gain, kv_lat_norm[None, :], q_gain, q_icnt, kn_gain, kr_gain, fq_gain, fk_gain,
              bf, freq, oneq, onek, qshift, kone)
    return consts, bound_mla <= SHIFT_BOUND_LIMIT, bound_fox <= SHIFT_BOUND_LIMIT


def kernel(x, positions, ffn1_norm, ffn1_w_gate, ffn1_w_up, ffn1_w_down, mix_norm, w_in, mla_q_lat_norm, mla_w_qb, mla_kv_lat_norm, mla_w_kvb, mla_q_nope_gain, mla_q_rope_gain, mla_k_nope_gain, mla_k_rope_gain, fox_q_gain, fox_k_gain, fox_b_f, w_branch_mla, w_branch_fox, b_gate, w_o, ffn2_norm, ffn2_w_gate, ffn2_w_up, ffn2_w_down):
    b, s, d = x.shape
    n = b * s
    pos = positions.reshape(b, s, 1)
    xt = x.reshape(n, d)
    for l in range(ffn1_norm.shape[0]):
        xt = _ffn_call(xt, ffn1_norm[l][None, :], _bf16(ffn1_w_gate[l]), _bf16(ffn1_w_up[l]),
                       _bf16(ffn1_w_down[l]))
        consts, mla_shift_safe, fox_shift_safe = _proj_constants(
            mix_norm[l], w_in[l], mla_q_lat_norm[l], mla_w_qb[l], mla_kv_lat_norm[l], mla_w_kvb[l],
            mla_q_nope_gain[l], mla_q_rope_gain[l], mla_k_nope_gain[l], mla_k_rope_gain[l],
            fox_q_gain[l], fox_k_gain[l], fox_b_f[l])
        qm, km, vm, qf, kf, vf, gm, gf = _proj_call(xt.reshape(b, s, d), pos, consts)
        ym = _attention(qm, km, vm, mla_shift_safe, "attn_mla")
        yf = _attention(qf, kf, vf, fox_shift_safe, "attn_fox")
        xt = _merge_call(
            xt, ym.reshape(n, V_WIDTH), yf.reshape(n, V_WIDTH), gm.reshape(n, d), gf.reshape(n, d),
            b_gate[l], _bf16(w_branch_mla[l]), _bf16(w_branch_fox[l]), _bf16(w_o[l]),
            ffn2_norm[l][None, :], _bf16(ffn2_w_gate[l]), _bf16(ffn2_w_up[l]), _bf16(ffn2_w_down[l]))
    return xt.reshape(b, s, d)
```

```python
import functools

import numpy as np
import jax
import jax.numpy as jnp
from jax import lax
from jax.experimental import pallas as pl
from jax.experimental.pallas import tpu as pltpu

D_MODEL = 1024
FFN_HIDDEN = 2816
FFN_RESIDUAL_WEIGHT = 0.5
HEADS = 8
MLA_NOPE_DIM = 64
MLA_ROPE_DIM = 32
MLA_V_DIM = 64
Q_LORA_RANK = 192
KV_LORA_RANK = 128
ROPE_THETA = 10000.0
FOX_HEAD_DIM = 64
RMS_EPS = 1e-6

LANES = 128
HEAD_PAD = LANES
HP_WIDTH = HEADS * HEAD_PAD
V_WIDTH = HEADS * MLA_V_DIM
Q_LAT_PAD = 256
ROPE_HALF = MLA_ROPE_DIM // 2
ROPE_LANE0 = MLA_NOPE_DIM
BIAS_LANE0 = FOX_HEAD_DIM
FOX_SHIFT_LANE = BIAS_LANE0 + 6
MLA_SHIFT_LANE = ROPE_LANE0 + MLA_ROPE_DIM
SHIFT_MARGIN = 40.0
SHIFT_BOUND_LIMIT = 68.0
BOUND_SLACK = 1.05

TOKEN_TILE = 512
HIDDEN_CHUNK = 256
GATE_CHUNK = 256
ATTN_Q_TILE = 512
ATTN_KV_TILE = 512
ATTN_KV_SUB = 256
QK_LOOKAHEAD = 2
MASK_VALUE = -1e30
VMEM_LIMIT = 56 * 1024 * 1024

PROJ_QLAT = 0
PROJ_KVLAT = PROJ_QLAT + Q_LAT_PAD
PROJ_KR = PROJ_KVLAT + KV_LORA_RANK
PROJ_FQ = PROJ_KR + LANES
PROJ_FK = PROJ_FQ + V_WIDTH
PROJ_GM = PROJ_FK + V_WIDTH
PROJ_GF = PROJ_GM + D_MODEL
PROJ_WIDTH = PROJ_GF + D_MODEL

VT_ROWS = 80
VT_ONES_ROW = MLA_V_DIM
VT_WIDTH = HEADS * VT_ROWS
NT_DIMS = (((1,), (1,)), ((), ()))
LOG2E = 1.4426950408889634


def _bf16(x):
    return x.astype(jnp.bfloat16)


def _dot(a, b):
    return jnp.dot(a, b, preferred_element_type=jnp.float32)


def _rms_scale(x, n):
    return lax.rsqrt(jnp.sum(x * x, axis=-1, keepdims=True) * (1.0 / n) + RMS_EPS)


def _split2(x):
    hi = _bf16(x)
    return hi, _bf16(x - hi.astype(jnp.float32))


def _split3(x):
    hi = _bf16(x)
    mid, lo = _split2(x - hi.astype(jnp.float32))
    return hi, mid, lo


def _segment_sumsq(x, seg_ref):
    seg = seg_ref[...]
    outs = []
    for c in range(x.shape[1] // 256):
        xs = x[:, c * 256:(c + 1) * 256]
        outs.append(_dot(_bf16(xs * xs), seg))
    return jnp.concatenate(outs, axis=1)


def _swiglu_residual(x, gain, wg_ref, wu_ref, wd_ref, t_ref):
    h = _bf16(x * _rms_scale(x, D_MODEL) * gain)
    for c in range(FFN_HIDDEN // HIDDEN_CHUNK):
        cols = slice(c * HIDDEN_CHUNK, (c + 1) * HIDDEN_CHUNK)
        a = _dot(h, wg_ref[:, cols])
        b = _dot(h, wu_ref[:, cols])
        t_ref[:, cols] = _bf16(a * jax.nn.sigmoid(a) * b)
    return x + FFN_RESIDUAL_WEIGHT * _dot(t_ref[...], wd_ref[...])


def _ffn_kernel(x_ref, gain_ref, wg_ref, wu_ref, wd_ref, o_ref, t_ref):
    o_ref[...] = _swiglu_residual(x_ref[...], gain_ref[...], wg_ref, wu_ref, wd_ref, t_ref)


def _const_spec(shape):
    return pl.BlockSpec(shape, lambda *_: (0,) * len(shape), pipeline_mode=pl.Buffered(1))


def _ffn_call(x, gain, wg, wu, wd):
    n = x.shape[0]
    return pl.pallas_call(
        _ffn_kernel,
        grid=(n // TOKEN_TILE,),
        in_specs=[
            pl.BlockSpec((TOKEN_TILE, D_MODEL), lambda i: (i, 0)),
            _const_spec((1, D_MODEL)),
            _const_spec((D_MODEL, FFN_HIDDEN)),
            _const_spec((D_MODEL, FFN_HIDDEN)),
            _const_spec((FFN_HIDDEN, D_MODEL)),
        ],
        out_specs=pl.BlockSpec((TOKEN_TILE, D_MODEL), lambda i: (i, 0)),
        out_shape=jax.ShapeDtypeStruct((n, D_MODEL), jnp.float32),
        scratch_shapes=[pltpu.VMEM((TOKEN_TILE, FFN_HIDDEN), jnp.bfloat16)],
        compiler_params=pltpu.CompilerParams(
            dimension_semantics=("arbitrary",), vmem_limit_bytes=VMEM_LIMIT),
        name="ffn1",
    )(x, gain, wg, wu, wd)


def _proj_kernel(x_ref, pos_ref, gain_ref, win_ref, wqb_ref, wkvb_ref, wvt_ref, wfvt_ref, vones_ref,
                 tri_ref, segq_ref, seg64_ref, eq_ref, ek_ref,
                 qlat_gain_ref, kvlat_gain_ref, q_gain_ref, q_icnt_ref, kn_gain_ref,
                 kr_gain_ref, fq_gain_ref, fk_gain_ref, bf_ref, freq_ref, oneq_ref, onek_ref,
                 qshift_ref, kone_ref,
                 qm_ref, km_ref, vm_ref, qf_ref, kf_ref, vf_ref, gm_ref, gf_ref,
                 carry_ref):
    x = x_ref[0]
    h = _bf16(x * _rms_scale(x, D_MODEL) * gain_ref[...])

    def proj(lo, width):
        return _dot(h, win_ref[:, lo:lo + width])

    gate_chunks = [(ref, col0, c * GATE_CHUNK) for ref, col0 in ((gm_ref, PROJ_GM), (gf_ref, PROJ_GF))
                   for c in range(D_MODEL // GATE_CHUNK)]

    def gates(n):
        for _ in range(min(n, len(gate_chunks))):
            ref, col0, c0 = gate_chunks.pop(0)
            ref[0, :, c0:c0 + GATE_CHUNK] = proj(col0 + c0, GATE_CHUNK)

    gates(2)
    lane =lax.broadcasted_iota(jnp.int32, (1, LANES), 1)
    rope_lanes = (lane >= ROPE_LANE0) & (lane < ROPE_LANE0 + MLA_ROPE_DIM)

    ang = pos_ref[0].astype(jnp.float32) * freq_ref[...]
    cos_t = jnp.cos(ang)
    sin_t = jnp.sin(ang)
    sin_hi = jnp.where(lane >= ROPE_LANE0 + ROPE_HALF, sin_t, 0.0)
    sin_lo = jnp.where(lane < ROPE_LANE0 + ROPE_HALF, -sin_t, 0.0)

    def rotary(t):
        return (t * cos_t + pltpu.roll(t, ROPE_HALF, 1) * sin_hi
                + pltpu.roll(t, LANES - ROPE_HALF, 1) * sin_lo)

    def store_values_t(ref, wt_ref, act):
        vt = lax.dot_general(wt_ref[...], act, NT_DIMS, preferred_element_type=jnp.float32)
        ones = jnp.concatenate([vones_ref[...]] * (TOKEN_TILE // LANES), axis=1)
        ref[0, 0] = _bf16(vt + ones)

    def head_normed(t, g_ref):
        return t * lax.rsqrt(_segment_sumsq(t, seg64_ref) * (1.0 / FOX_HEAD_DIM) + RMS_EPS) * g_ref[...]

    def head_tiles(t):
        low = lane < FOX_HEAD_DIM
        tiles = []
        for pair in range(HEADS // 2):
            src = t[:, pair * LANES:(pair + 1) * LANES]
            tiles.append(jnp.where(low, src, 0.0))
            tiles.append(jnp.where(low, pltpu.roll(src, LANES - FOX_HEAD_DIM, 1), 0.0))
        return tiles

    gates(1)
    q_lat = proj(PROJ_QLAT, Q_LAT_PAD)
    q_lat = _bf16(q_lat * _rms_scale(q_lat, Q_LORA_RANK) * qlat_gain_ref[...])
    q = _dot(q_lat, wqb_ref[...])
    gates(1)
    q = q * lax.rsqrt(_segment_sumsq(q, segq_ref) * q_icnt_ref[...] + RMS_EPS) * q_gain_ref[...]
    for hd in range(HEADS):
        if hd % 2 == 0:
            gates(1)
        cols = slice(hd * HEAD_PAD, (hd + 1) * HEAD_PAD)
        qm_ref[0, cols, :] = _bf16((rotary(q[:, cols]) + qshift_ref[...]).T)

    kv_lat = proj(PROJ_KVLAT, KV_LORA_RANK)
    kv_lat = _bf16(kv_lat * _rms_scale(kv_lat, KV_LORA_RANK) * kvlat_gain_ref[...])
    gates(1)
    k_nope = head_tiles(head_normed(_dot(kv_lat, wkvb_ref[...]), kn_gain_ref))
    store_values_t(vm_ref, wvt_ref, kv_lat)

    kr_grp = proj(PROJ_KR, LANES)
    kr = jnp.where(rope_lanes, kr_grp, 0.0)
    kr = rotary(kr * _rms_scale(kr, MLA_ROPE_DIM) * kr_gain_ref[...]) + kone_ref[...]
    gates(1)
    for hd in range(HEADS):
        cols = slice(hd * HEAD_PAD, (hd + 1) * HEAD_PAD)
        km_ref[0, :, cols] = _bf16(k_nope[hd] + kr)
    gates(1)

    @pl.when(pl.program_id(1) == 0)
    def _():
        carry_ref[...] = jnp.zeros_like(carry_ref)

    log_f = jnp.where(lane < HEADS, jax.nn.log_sigmoid(kr_grp + bf_ref[...]), 0.0)
    tri = tri_ref[...]
    f_hi, f_lo = _split2(log_f)
    c = _dot(tri, f_hi) + _dot(tri, f_lo) + carry_ref[0:1, :]
    carry_ref[0:1, :] = c[TOKEN_TILE - 1:TOKEN_TILE, :]
    c2 = c * LOG2E
    c_rep = c2 + pltpu.roll(c2, HEADS, 1) + pltpu.roll(c2, 2 * HEADS, 1)
    c_hi, c_mid, c_lo = _split3(c_rep)
    c_parts = jnp.where(lane < HEADS, c_hi, jnp.where(lane < 2 * HEADS, c_mid, c_lo))

    for o_ref, col0, g_ref, e_ref, one_ref in ((qf_ref, PROJ_FQ, fq_gain_ref, eq_ref, oneq_ref),
                                               (kf_ref, PROJ_FK, fk_gain_ref, ek_ref, onek_ref)):
        gates(1)
        tiles = head_tiles(head_normed(proj(col0, V_WIDTH), g_ref))
        gates(1)
        bias = _dot(c_parts, e_ref[...]) + one_ref[...]
        for hd in range(HEADS):
            cols = slice(hd * HEAD_PAD, (hd + 1) * HEAD_PAD)
            tile_hd = tiles[hd] + bias[:, cols]
            if o_ref is qf_ref:
                o_ref[0, cols, :] = _bf16(tile_hd.T)
            else:
                o_ref[0, :, cols] = _bf16(tile_hd)
    store_values_t(vf_ref, wfvt_ref, h)
    gates(len(gate_chunks))


def _proj_call(x, pos, consts):
    b, s, _ = x.shape
    bf16, f32 = jnp.bfloat16, jnp.float32
    n_tiles = s // TOKEN_TILE
    tile = lambda w: pl.BlockSpec((1, TOKEN_TILE, w), lambda bi, i: (bi, i, 0))
    tile_t = pl.BlockSpec((1, 1, VT_WIDTH, TOKEN_TILE), lambda bi, i: (bi, i, 0, 0))
    rows = lambda w, d: jax.ShapeDtypeStruct((b, s, w), d)
    rows_t = jax.ShapeDtypeStruct((b, n_tiles, VT_WIDTH, TOKEN_TILE), bf16)
    tile_q = pl.BlockSpec((1, HP_WIDTH, TOKEN_TILE), lambda bi, i: (bi, 0, i))
    rows_q = jax.ShapeDtypeStruct((b, HP_WIDTH, s), bf16)
    return pl.pallas_call(
        _proj_kernel,
        grid=(b, n_tiles),
        in_specs=[tile(D_MODEL), tile(1)] + [_const_spec(c.shape) for c in consts],
        out_specs=[tile_q, tile(HP_WIDTH), tile_t, tile_q, tile(HP_WIDTH), tile_t,
                   tile(D_MODEL), tile(D_MODEL)],
        out_shape=[rows_q, rows(HP_WIDTH, bf16), rows_t, rows_q, rows(HP_WIDTH, bf16), rows_t,
                   rows(D_MODEL, f32), rows(D_MODEL, f32)],
        scratch_shapes=[pltpu.VMEM((8, LANES), jnp.float32)],
        compiler_params=pltpu.CompilerParams(
            dimension_semantics=("arbitrary", "arbitrary"), vmem_limit_bytes=VMEM_LIMIT),
        name="proj",
    )(x, pos, *consts)


def _attn_kernel(qt_ref, k_ref, vt_ref, o_ref, m_ref, acc_ref, ahead_ref, *, running_max):
    qi = pl.program_id(1)
    tq, tk, ks = ATTN_Q_TILE, ATTN_KV_TILE, ATTN_KV_SUB
    units = [(hd, sub) for sub in range(tk // ks) for hd in range(HEADS)]

    def logits(t, unit, diagonal):
        hd, sub = unit
        hcols = slice(hd * HEAD_PAD, (hd + 1) * HEAD_PAD)
        q_lo = sub * ks if diagonal else 0
        off = pl.multiple_of(t * tk + sub * ks, ks)
        return _dot(k_ref[0, pl.ds(off, ks), hcols], qt_ref[0, hcols, q_lo:])

    def step(t, diagonal, next_t):
        pending = [ahead_ref[j] for j in range(QK_LOOKAHEAD)]
        for i, (hd, sub) in enumerate(units):
            s = pending.pop(0)
            j = i + QK_LOOKAHEAD
            if j < len(units):
                pending.append(logits(t, units[j], diagonal))
            elif next_t is not None:
                ahead_ref[j - len(units)] = logits(next_t, units[j - len(units)], False)
            q_lo = sub * ks if diagonal else 0
            if diagonal:
                key = lax.broadcasted_iota(jnp.int32, s.shape, 0)
                query = lax.broadcasted_iota(jnp.int32, s.shape, 1)
                s = jnp.where(key <= query, s, MASK_VALUE)
            vt = vt_ref[0, t, hd * VT_ROWS:(hd + 1) * VT_ROWS, sub * ks:(sub + 1) * ks]
            if running_max:
                m_prev = m_ref[hd, :, q_lo:]
                m_new = jnp.maximum(m_prev, jnp.max(s, axis=0, keepdims=True))
                alpha = jnp.exp2(m_prev - m_new)
                p = _bf16(jnp.exp2(s - m_new))
                acc_ref[hd, :, q_lo:] = alpha * acc_ref[hd, :, q_lo:] + _dot(vt, p)
                m_ref[hd, :, q_lo:] = m_new
            else:
                acc_ref[hd, :, q_lo:] += _dot(vt, _bf16(jnp.exp2(s)))

    if running_max:
        m_ref[...] = jnp.full(m_ref.shape, MASK_VALUE, jnp.float32)
    acc_ref[...] = jnp.zeros(acc_ref.shape, jnp.float32)
    for j in range(QK_LOOKAHEAD):
        ahead_ref[j] = logits(0, units[j], False)

    def body(t, carry):
        step(t, False, t + 1)
        return carry

    lax.fori_loop(0, qi, body, 0)
    step(qi, True, None)

    for pair in range(HEADS // 2):
        y_t = []
        for hd in (2 * pair, 2 * pair + 1):
            acc = acc_ref[hd]
            y_t.append(acc[:MLA_V_DIM] / acc[VT_ONES_ROW:VT_ONES_ROW + 1])
        y = jnp.concatenate(y_t, axis=0).T
        o_ref[0, :, pair * LANES:(pair + 1) * LANES] = y.astype(o_ref.dtype)


def _attn_call(qt, k, vt, *, running_max, name):
    b, s, _ = k.shape
    assert ATTN_Q_TILE == ATTN_KV_TILE == TOKEN_TILE
    return pl.pallas_call(
        functools.partial(_attn_kernel, running_max=running_max),
        grid=(b, s // ATTN_Q_TILE),
        in_specs=[
            pl.BlockSpec((1, HP_WIDTH, ATTN_Q_TILE), lambda bi, i: (bi, 0, i)),
            pl.BlockSpec((1, s, HP_WIDTH), lambda bi, i: (bi, 0, 0)),
            pl.BlockSpec((1,) + vt.shape[1:], lambda bi, i: (bi, 0, 0, 0)),
        ],
        out_specs=pl.BlockSpec((1, ATTN_Q_TILE, V_WIDTH), lambda bi, i: (bi, i, 0)),
        out_shape=jax.ShapeDtypeStruct((b, s, V_WIDTH), jnp.bfloat16),
        scratch_shapes=[
            pltpu.VMEM((HEADS, 1, ATTN_Q_TILE), jnp.float32),
            pltpu.VMEM((HEADS, VT_ROWS, ATTN_Q_TILE), jnp.float32),
            pltpu.VMEM((QK_LOOKAHEAD, ATTN_KV_SUB, ATTN_Q_TILE), jnp.float32),
        ],
        compiler_params=pltpu.CompilerParams(
            dimension_semantics=("arbitrary", "arbitrary"), vmem_limit_bytes=VMEM_LIMIT),
        name=name,
    )(qt, k, vt)


def _attention(q, k, vt, shift_is_safe, name):
    return lax.cond(
        shift_is_safe,
        functools.partial(_attn_call, running_max=False, name=name + "_shifted"),
        functools.partial(_attn_call, running_max=True, name=name),
        q, k, vt)


def _merge_kernel(x_ref, ym_ref, yf_ref, gm_ref, gf_ref, bg_ref, wa_ref, wb_ref, wo_ref,
                  gain_ref, wg_ref, wu_ref, wd_ref, o_ref, t_ref):
    mixed = (jax.nn.sigmoid(gm_ref[...] + bg_ref[0:1, :]) * _dot(ym_ref[...], wa_ref[...])
             + jax.nn.sigmoid(gf_ref[...] + bg_ref[1:2, :]) * _dot(yf_ref[...], wb_ref[...]))
    x = x_ref[...] + _dot(_bf16(mixed), wo_ref[...])
    o_ref[...] = _swiglu_residual(x, gain_ref[...], wg_ref, wu_ref, wd_ref, t_ref)


def _merge_call(x, ym, yf, gm, gf, bg, wa, wb, wo, gain, wg, wu, wd):
    n = x.shape[0]
    tile = lambda w: pl.BlockSpec((TOKEN_TILE, w), lambda i: (i, 0))
    consts = (bg, wa, wb, wo, gain, wg, wu, wd)
    return pl.pallas_call(
        _merge_kernel,
        grid=(n // TOKEN_TILE,),
        in_specs=[tile(D_MODEL), tile(V_WIDTH), tile(V_WIDTH), tile(D_MODEL), tile(D_MODEL)]
        + [_const_spec(c.shape) for c in consts],
        out_specs=tile(D_MODEL),
        out_shape=jax.ShapeDtypeStruct((n, D_MODEL), jnp.float32),
        scratch_shapes=[pltpu.VMEM((TOKEN_TILE, FFN_HIDDEN), jnp.bfloat16)],
        compiler_params=pltpu.CompilerParams(
            dimension_semantics=("arbitrary",), vmem_limit_bytes=VMEM_LIMIT),
        name="merge_ffn2",
    )(x, ym, yf, gm, gf, *consts)


def _pad_heads(w, width):
    rows = w.shape[0]
    w = w.reshape(rows, HEADS, -1)
    return jnp.pad(w, ((0, 0), (0, 0), (0, width - w.shape[2]))).reshape(rows, HEADS * width)


def _head_lanes(vec, lane0=0):
    tile = jnp.zeros((HEAD_PAD,), jnp.float32).at[lane0:lane0 + vec.shape[0]].set(vec)
    return jnp.tile(tile, HEADS)[None, :]


def _proj_constants(mix_norm, w_in, q_lat_norm, w_qb, kv_lat_norm, w_kvb, q_nope_gain, q_rope_gain,
                    k_nope_gain, k_rope_gain, fox_q_gain, fox_k_gain, fox_b_f):
    f32 = jnp.float32
    o_qlat, o_kvlat = 0, Q_LORA_RANK
    o_kr = o_kvlat + KV_LORA_RANK
    o_fq = o_kr + MLA_ROPE_DIM
    o_fk = o_fq + V_WIDTH
    o_fv = o_fk + V_WIDTH
    o_fl = o_fv + V_WIDTH
    o_gm = o_fl + HEADS
    o_gf = o_gm + D_MODEL

    def zero_cols(n):
        return jnp.zeros((D_MODEL, n), w_in.dtype)

    win = _bf16(jnp.concatenate([
        w_in[:, o_qlat:o_kvlat], zero_cols(Q_LAT_PAD - Q_LORA_RANK),
        w_in[:, o_kvlat:o_kr],
        w_in[:, o_fl:o_gm], zero_cols(ROPE_LANE0 - HEADS),
        w_in[:, o_kr:o_fq], zero_cols(LANES - ROPE_LANE0 - MLA_ROPE_DIM),
        w_in[:, o_fq:o_fv],
        w_in[:, o_gm:],
    ], axis=1))
    assert win.shape[1] == PROJ_WIDTH

    qk_dim = MLA_NOPE_DIM + MLA_ROPE_DIM
    wqb = _bf16(jnp.pad(_pad_heads(w_qb, HEAD_PAD), ((0, Q_LAT_PAD - Q_LORA_RANK), (0, 0))))
    w_kvb = w_kvb.reshape(KV_LORA_RANK, HEADS, MLA_NOPE_DIM + MLA_V_DIM)
    wkvb = _bf16(w_kvb[:, :, :MLA_NOPE_DIM].reshape(KV_LORA_RANK, V_WIDTH))
    wvt = _bf16(_pad_heads(w_kvb[:, :, MLA_NOPE_DIM:].reshape(KV_LORA_RANK, V_WIDTH), VT_ROWS).T)
    wfvt = _bf16(_pad_heads(w_in[:, o_fv:o_fl], VT_ROWS).T)
    vones = np.zeros((HEADS, VT_ROWS, LANES), np.float32)
    vones[:, VT_ONES_ROW, :] = 1.0
    vones = jnp.asarray(vones.reshape(VT_WIDTH, LANES))

    r = np.arange(TOKEN_TILE)
    tri = jnp.asarray(r[:, None] >= r[None, :], jnp.bfloat16)
    l256 = np.arange(256)
    same_head = (l256[:, None] // HEAD_PAD) == (l256[None, :] // HEAD_PAD)
    in_nope = (l256 % HEAD_PAD) < MLA_NOPE_DIM
    in_rope = ((l256 % HEAD_PAD) >= ROPE_LANE0) & ((l256 % HEAD_PAD) < ROPE_LANE0 + MLA_ROPE_DIM)
    seg_q = same_head & ((in_nope[:, None] & in_nope[None, :]) | (in_rope[:, None] & in_rope[None, :]))
    seg_q = jnp.asarray(seg_q, jnp.bfloat16)
    seg_64 = jnp.asarray((l256[:, None] // FOX_HEAD_DIM) == (l256[None, :] // FOX_HEAD_DIM),
                         jnp.bfloat16)
    eq = np.zeros((LANES, HP_WIDTH), np.float32)
    ek = np.zeros((LANES, HP_WIDTH), np.float32)
    oneq = np.zeros((1, HP_WIDTH), np.float32)
    onek = np.zeros((1, HP_WIDTH), np.float32)
    for hd in range(HEADS):
        for piece in range(3):
            eq[piece * HEADS + hd, hd * HEAD_PAD + BIAS_LANE0 + piece] = 1.0
            ek[piece * HEADS + hd, hd * HEAD_PAD + BIAS_LANE0 + 3 + piece] = -1.0
            oneq[0, hd * HEAD_PAD + BIAS_LANE0 + 3 + piece] = 1.0
            onek[0, hd * HEAD_PAD + BIAS_LANE0 + piece] = 1.0
    eq = jnp.asarray(eq, jnp.bfloat16)
    ek = jnp.asarray(ek, jnp.bfloat16)

    mla_scale = float(qk_dim) ** -0.5 * LOG2E
    fox_scale = float(FOX_HEAD_DIM) ** -0.5 * LOG2E
    q_gain = _head_lanes(jnp.concatenate([q_nope_gain, q_rope_gain]) * mla_scale)
    icnt = np.ones((HEAD_PAD,), np.float32)
    icnt[:MLA_NOPE_DIM] = 1.0 / MLA_NOPE_DIM
    icnt[ROPE_LANE0:ROPE_LANE0 + MLA_ROPE_DIM] = 1.0 / MLA_ROPE_DIM
    q_icnt = jnp.asarray(np.tile(icnt, HEADS)[None, :])
    kn_gain = jnp.tile(k_nope_gain, HEADS)[None, :]
    kr_gain = jnp.zeros((1, LANES), f32).at[0, ROPE_LANE0:ROPE_LANE0 + MLA_ROPE_DIM].set(k_rope_gain)
    fq_gain = jnp.tile(fox_q_gain * fox_scale, HEADS)[None, :]
    fk_gain = jnp.tile(fox_k_gain, HEADS)[None, :]
    bf = jnp.zeros((1, LANES), f32).at[0, :HEADS].set(fox_b_f)
    inv_freq = ROPE_THETA ** (-jnp.arange(ROPE_HALF, dtype=f32) / ROPE_HALF)
    freq = jnp.zeros((1, LANES), f32).at[0, ROPE_LANE0:ROPE_LANE0 + MLA_ROPE_DIM].set(
        jnp.concatenate([inv_freq, inv_freq]))
    qlat_gain = jnp.pad(q_lat_norm, (0, Q_LAT_PAD - Q_LORA_RANK))[None, :]

    amax = lambda g: jnp.max(jnp.abs(g))
    bound_fox = (FOX_HEAD_DIM * amax(fox_q_gain) * amax(fox_k_gain) * fox_scale) * BOUND_SLACK
    q_sq = MLA_NOPE_DIM * amax(q_nope_gain) ** 2 + MLA_ROPE_DIM * amax(q_rope_gain) ** 2
    k_sq = MLA_NOPE_DIM * amax(k_nope_gain) ** 2 + MLA_ROPE_DIM * amax(k_rope_gain) ** 2
    bound_mla = jnp.sqrt(q_sq * k_sq) * mla_scale * BOUND_SLACK
    shift_lane = np.zeros((1, LANES), np.float32)
    shift_lane[0, MLA_SHIFT_LANE] = 1.0
    kone = jnp.asarray(shift_lane)
    qshift = kone * -(bound_mla - SHIFT_MARGIN)
    fox_shift_lanes = np.zeros((1, HP_WIDTH), np.float32)
    fox_shift_lanes[0, FOX_SHIFT_LANE::HEAD_PAD] = 1.0
    oneq = jnp.asarray(oneq) + jnp.asarray(fox_shift_lanes) * -(bound_fox - SHIFT_MARGIN)
    onek = jnp.asarray(onek + fox_shift_lanes)

    consts = (mix_norm[None, :], win, wqb, wkvb, wvt, wfvt, vones, tri, seg_q, seg_64, eq, ek,
              qlat_gain, kv_lat_norm[None, :], q_gain, q_icnt, kn_gain, kr_gain, fq_gain, fk_gain,
              bf, freq, oneq, onek, qshift, kone)
    return consts, bound_mla <= SHIFT_BOUND_LIMIT, bound_fox <= SHIFT_BOUND_LIMIT


def kernel(x, positions, ffn1_norm, ffn1_w_gate, ffn1_w_up, ffn1_w_down, mix_norm, w_in, mla_q_lat_norm, mla_w_qb, mla_kv_lat_norm, mla_w_kvb, mla_q_nope_gain, mla_q_rope_gain, mla_k_nope_gain, mla_k_rope_gain, fox_q_gain, fox_k_gain, fox_b_f, w_branch_mla, w_branch_fox, b_gate, w_o, ffn2_norm, ffn2_w_gate, ffn2_w_up, ffn2_w_down):
    b, s, d = x.shape
    n = b * s
    pos = positions.reshape(b, s, 1)
    xt = x.reshape(n, d)
    for l in range(ffn1_norm.shape[0]):
        xt = _ffn_call(xt, ffn1_norm[l][None, :], _bf16(ffn1_w_gate[l]), _bf16(ffn1_w_up[l]),
                       _bf16(ffn1_w_down[l]))
        consts, mla_shift_safe, fox_shift_safe = _proj_constants(
            mix_norm[l], w_in[l], mla_q_lat_norm[l], mla_w_qb[l], mla_kv_lat_norm[l], mla_w_kvb[l],
            mla_q_nope_gain[l], mla_q_rope_gain[l], mla_k_nope_gain[l], mla_k_rope_gain[l],
            fox_q_gain[l], fox_k_gain[l], fox_b_f[l])
        qm, km, vm, qf, kf, vf, gm, gf = _proj_call(xt.reshape(b, s, d), pos, consts)
        ym = _attention(qm, km, vm, mla_shift_safe, "attn_mla")
        yf = _attention(qf, kf, vf, fox_shift_safe, "attn_fox")
        xt = _merge_call(
            xt, ym.reshape(n, V_WIDTH), yf.reshape(n, V_WIDTH), gm.reshape(n, d), gf.reshape(n, d),
            b_gate[l], _bf16(w_branch_mla[l]), _bf16(w_branch_fox[l]), _bf16(w_o[l]),
            ffn2_norm[l][None, :], _bf16(ffn2_w_gate[l]), _bf16(ffn2_w_up[l]), _bf16(ffn2_w_down[l]))
    return xt.reshape(b, s, d)
```

```python
import functools

import numpy as np
import jax
import jax.numpy as jnp
from jax import lax
from jax.experimental import pallas as pl
from jax.experimental.pallas import tpu as pltpu

D_MODEL = 1024
FFN_HIDDEN = 2816
FFN_RESIDUAL_WEIGHT = 0.5
HEADS = 8
MLA_NOPE_DIM = 64
MLA_ROPE_DIM = 32
MLA_V_DIM = 64
Q_LORA_RANK = 192
KV_LORA_RANK = 128
ROPE_THETA = 10000.0
FOX_HEAD_DIM = 64
RMS_EPS = 1e-6

LANES = 128
HEAD_PAD = LANES
HP_WIDTH = HEADS * HEAD_PAD
V_WIDTH = HEADS * MLA_V_DIM
Q_LAT_PAD = 256
ROPE_HALF = MLA_ROPE_DIM // 2
ROPE_LANE0 = MLA_NOPE_DIM
BIAS_LANE0 = FOX_HEAD_DIM
FOX_SHIFT_LANE = BIAS_LANE0 + 6
MLA_SHIFT_LANE = ROPE_LANE0 + MLA_ROPE_DIM
SHIFT_MARGIN = 40.0
SHIFT_BOUND_LIMIT = 68.0
BOUND_SLACK = 1.05

TOKEN_TILE = 512
FFN_TILE = 1024
HIDDEN_CHUNK = 256
GATE_CHUNK = 256
ATTN_Q_TILE = 512
ATTN_KV_TILE = 512
ATTN_KV_SUB = 256
QK_LOOKAHEAD = 2
MASK_VALUE = -1e30
VMEM_LIMIT = 56 * 1024 * 1024

PROJ_QLAT = 0
PROJ_KVLAT = PROJ_QLAT + Q_LAT_PAD
PROJ_KR = PROJ_KVLAT + KV_LORA_RANK
PROJ_FQ = PROJ_KR + LANES
PROJ_FK = PROJ_FQ + V_WIDTH
PROJ_FV = PROJ_FK + V_WIDTH
PROJ_GM = PROJ_FV + V_WIDTH
PROJ_GF = PROJ_GM + D_MODEL
PROJ_WIDTH = PROJ_GF + D_MODEL

VT_ROWS = 80
VT_ONES_ROW = MLA_V_DIM
VT_WIDTH = HEADS * VT_ROWS
LOG2E = 1.4426950408889634


def _bf16(x):
    return x.astype(jnp.bfloat16)


def _dot(a, b):
    return jnp.dot(a, b, preferred_element_type=jnp.float32)


def _rms_scale(x, n):
    return lax.rsqrt(jnp.sum(x * x, axis=-1, keepdims=True) * (1.0 / n) + RMS_EPS)


def _split2(x):
    hi = _bf16(x)
    return hi, _bf16(x - hi.astype(jnp.float32))


def _split3(x):
    hi = _bf16(x)
    mid, lo = _split2(x - hi.astype(jnp.float32))
    return hi, mid, lo


def _segment_sumsq(x, seg_ref):
    seg = seg_ref[...]
    outs = []
    for c in range(x.shape[1] // 256):
        xs = x[:, c * 256:(c + 1) * 256]
        outs.append(_dot(_bf16(xs * xs), seg))
    return jnp.concatenate(outs, axis=1)


def _swiglu_residual(x, gain, wg_ref, wu_ref, wd_ref, t_ref):
    h = _bf16(x * _rms_scale(x, D_MODEL) * gain)
    for c in range(FFN_HIDDEN // HIDDEN_CHUNK):
        cols = slice(c * HIDDEN_CHUNK, (c + 1) * HIDDEN_CHUNK)
        a = _dot(h, wg_ref[:, cols])
        b = _dot(h, wu_ref[:, cols])
        t_ref[:, cols] = _bf16(a * jax.nn.sigmoid(a) * b)
    return x + FFN_RESIDUAL_WEIGHT * _dot(t_ref[...], wd_ref[...])


def _ffn_kernel(x_ref, gain_ref, wg_ref, wu_ref, wd_ref, o_ref, t_ref):
    o_ref[...] = _swiglu_residual(x_ref[...], gain_ref[...], wg_ref, wu_ref, wd_ref, t_ref)


def _const_spec(shape):
    return pl.BlockSpec(shape, lambda *_: (0,) * len(shape), pipeline_mode=pl.Buffered(1))


def _ffn_call(x, gain, wg, wu, wd):
    n = x.shape[0]
    return pl.pallas_call(
        _ffn_kernel,
        grid=(n // FFN_TILE,),
        in_specs=[
            pl.BlockSpec((FFN_TILE, D_MODEL), lambda i: (i, 0)),
            _const_spec((1, D_MODEL)),
            _const_spec((D_MODEL, FFN_HIDDEN)),
            _const_spec((D_MODEL, FFN_HIDDEN)),
            _const_spec((FFN_HIDDEN, D_MODEL)),
        ],
        out_specs=pl.BlockSpec((FFN_TILE, D_MODEL), lambda i: (i, 0)),
        out_shape=jax.ShapeDtypeStruct((n, D_MODEL), jnp.float32),
        scratch_shapes=[pltpu.VMEM((FFN_TILE, FFN_HIDDEN), jnp.bfloat16)],
        compiler_params=pltpu.CompilerParams(
            dimension_semantics=("arbitrary",), vmem_limit_bytes=VMEM_LIMIT),
        name="ffn1",
    )(x, gain, wg, wu, wd)


def _proj_kernel(x_ref, pos_ref, gain_ref, win_ref, wqb_ref, wkvb_ref, vtail_ref,
                 tri_ref, segq_ref, seg64_ref, eq_ref, ek_ref,
                 qlat_gain_ref, kvlat_gain_ref, q_gain_ref, q_icnt_ref, kn_gain_ref,
                 kr_gain_ref, fq_gain_ref, fk_gain_ref, bf_ref, freq_ref, oneq_ref, onek_ref,
                 qshift_ref, kone_ref,
                 qm_ref, km_ref, vm_ref, qf_ref, kf_ref, vf_ref, gm_ref, gf_ref,
                 carry_ref):
    x = x_ref[0]
    h = _bf16(x * _rms_scale(x, D_MODEL) * gain_ref[...])

    def proj(lo, width):
        return _dot(h, win_ref[:, lo:lo + width])

    gate_chunks = [(ref, col0, c * GATE_CHUNK) for ref, col0 in ((gm_ref, PROJ_GM), (gf_ref, PROJ_GF))
                   for c in range(D_MODEL // GATE_CHUNK)]

    def gates(n):
        for _ in range(min(n, len(gate_chunks))):
            ref, col0, c0 = gate_chunks.pop(0)
            ref[0, :, c0:c0 + GATE_CHUNK] = proj(col0 + c0, GATE_CHUNK)

    gates(2)
    lane =lax.broadcasted_iota(jnp.int32, (1, LANES), 1)
    rope_lanes = (lane >= ROPE_LANE0) & (lane < ROPE_LANE0 + MLA_ROPE_DIM)

    ang = pos_ref[0].astype(jnp.float32) * freq_ref[...]
    cos_t = jnp.cos(ang)
    sin_t = jnp.sin(ang)
    sin_hi = jnp.where(lane >= ROPE_LANE0 + ROPE_HALF, sin_t, 0.0)
    sin_lo = jnp.where(lane < ROPE_LANE0 + ROPE_HALF, -sin_t, 0.0)

    def rotary(t):
        return (t * cos_t + pltpu.roll(t, ROPE_HALF, 1) * sin_hi
                + pltpu.roll(t, LANES - ROPE_HALF, 1) * sin_lo)

    def store_values_t(ref, v):
        tail = jnp.concatenate([vtail_ref[...]] * (TOKEN_TILE // LANES), axis=1)
        blocks = []
        for pair in range(HEADS // 2):
            pair_t = v[:, pair * LANES:(pair + 1) * LANES].T
            blocks += [pair_t[:MLA_V_DIM], tail, pair_t[MLA_V_DIM:], tail]
        ref[0, 0] = _bf16(jnp.concatenate(blocks, axis=0))

    def head_normed(t, g_ref):
        return t * lax.rsqrt(_segment_sumsq(t, seg64_ref) * (1.0 / FOX_HEAD_DIM) + RMS_EPS) * g_ref[...]

    def head_tiles(t):
        low = lane < FOX_HEAD_DIM
        tiles = []
        for pair in range(HEADS // 2):
            src = t[:, pair * LANES:(pair + 1) * LANES]
            tiles.append(jnp.where(low, src, 0.0))
            tiles.append(jnp.where(low, pltpu.roll(src, LANES - FOX_HEAD_DIM, 1), 0.0))
        return tiles

    gates(1)
    q_lat = proj(PROJ_QLAT, Q_LAT_PAD)
    q_lat = _bf16(q_lat * _rms_scale(q_lat, Q_LORA_RANK) * qlat_gain_ref[...])
    q = _dot(q_lat, wqb_ref[...])
    gates(1)
    q = q * lax.rsqrt(_segment_sumsq(q, segq_ref) * q_icnt_ref[...] + RMS_EPS) * q_gain_ref[...]
    for hd in range(HEADS):
        if hd % 2 == 0:
            gates(1)
        cols = slice(hd * HEAD_PAD, (hd + 1) * HEAD_PAD)
        qm_ref[0, cols, :] = _bf16((rotary(q[:, cols]) + qshift_ref[...]).T)

    kv_lat = proj(PROJ_KVLAT, KV_LORA_RANK)
    kv_lat = _bf16(kv_lat * _rms_scale(kv_lat, KV_LORA_RANK) * kvlat_gain_ref[...])
    gates(1)
    kv = _dot(kv_lat, wkvb_ref[...])
    k_nope = head_tiles(head_normed(kv[:, :V_WIDTH], kn_gain_ref))
    store_values_t(vm_ref, kv[:, V_WIDTH:])

    kr_grp = proj(PROJ_KR, LANES)
    kr = jnp.where(rope_lanes, kr_grp, 0.0)
    kr = rotary(kr * _rms_scale(kr, MLA_ROPE_DIM) * kr_gain_ref[...]) + kone_ref[...]
    gates(1)
    for hd in range(HEADS):
        cols = slice(hd * HEAD_PAD, (hd + 1) * HEAD_PAD)
        km_ref[0, :, cols] = _bf16(k_nope[hd] + kr)
    gates(1)

    @pl.when(pl.program_id(1) == 0)
    def _():
        carry_ref[...] = jnp.zeros_like(carry_ref)

    log_f = jnp.where(lane < HEADS, jax.nn.log_sigmoid(kr_grp + bf_ref[...]), 0.0)
    tri = tri_ref[...]
    f_hi, f_lo = _split2(log_f)
    c = _dot(tri, f_hi) + _dot(tri, f_lo) + carry_ref[0:1, :]
    carry_ref[0:1, :] = c[TOKEN_TILE - 1:TOKEN_TILE, :]
    c2 = c * LOG2E
    c_rep = c2 + pltpu.roll(c2, HEADS, 1) + pltpu.roll(c2, 2 * HEADS, 1)
    c_hi, c_mid, c_lo = _split3(c_rep)
    c_parts = jnp.where(lane < HEADS, c_hi, jnp.where(lane < 2 * HEADS, c_mid, c_lo))

    for o_ref, col0, g_ref, e_ref, one_ref in ((qf_ref, PROJ_FQ, fq_gain_ref, eq_ref, oneq_ref),
                                               (kf_ref, PROJ_FK, fk_gain_ref, ek_ref, onek_ref)):
        gates(1)
        tiles = head_tiles(head_normed(proj(col0, V_WIDTH), g_ref))
        gates(1)
        bias = _dot(c_parts, e_ref[...]) + one_ref[...]
        for hd in range(HEADS):
            cols = slice(hd * HEAD_PAD, (hd + 1) * HEAD_PAD)
            tile_hd = tiles[hd] + bias[:, cols]
            if o_ref is qf_ref:
                o_ref[0, cols, :] = _bf16(tile_hd.T)
            else:
                o_ref[0, :, cols] = _bf16(tile_hd)
    store_values_t(vf_ref, proj(PROJ_FV, V_WIDTH))
    gates(len(gate_chunks))


def _proj_call(x, pos, consts):
    b, s, _ = x.shape
    bf16, f32 = jnp.bfloat16, jnp.float32
    n_tiles = s // TOKEN_TILE
    tile = lambda w: pl.BlockSpec((1, TOKEN_TILE, w), lambda bi, i: (bi, i, 0))
    tile_t = pl.BlockSpec((1, 1, VT_WIDTH, TOKEN_TILE), lambda bi, i: (bi, i, 0, 0))
    rows = lambda w, d: jax.ShapeDtypeStruct((b, s, w), d)
    rows_t = jax.ShapeDtypeStruct((b, n_tiles, VT_WIDTH, TOKEN_TILE), bf16)
    tile_q = pl.BlockSpec((1, HP_WIDTH, TOKEN_TILE), lambda bi, i: (bi, 0, i))
    rows_q = jax.ShapeDtypeStruct((b, HP_WIDTH, s), bf16)
    return pl.pallas_call(
        _proj_kernel,
        grid=(b, n_tiles),
        in_specs=[tile(D_MODEL), tile(1)] + [_const_spec(c.shape) for c in consts],
        out_specs=[tile_q, tile(HP_WIDTH), tile_t, tile_q, tile(HP_WIDTH), tile_t,
                   tile(D_MODEL), tile(D_MODEL)],
        out_shape=[rows_q, rows(HP_WIDTH, bf16), rows_t, rows_q, rows(HP_WIDTH, bf16), rows_t,
                   rows(D_MODEL, f32), rows(D_MODEL, f32)],
        scratch_shapes=[pltpu.VMEM((8, LANES), jnp.float32)],
        compiler_params=pltpu.CompilerParams(
            dimension_semantics=("arbitrary", "arbitrary"), vmem_limit_bytes=VMEM_LIMIT),
        name="proj",
    )(x, pos, *consts)


def _attn_kernel(qt_ref, k_ref, vt_ref, o_ref, m_ref, acc_ref, ahead_ref, *, running_max):
    qi = pl.program_id(1)
    tq, tk, ks = ATTN_Q_TILE, ATTN_KV_TILE, ATTN_KV_SUB
    units = [(hd, sub) for sub in range(tk // ks) for hd in range(HEADS)]

    def logits(t, unit, diagonal):
        hd, sub = unit
        hcols = slice(hd * HEAD_PAD, (hd + 1) * HEAD_PAD)
        q_lo = sub * ks if diagonal else 0
        off = pl.multiple_of(t * tk + sub * ks, ks)
        return _dot(k_ref[0, pl.ds(off, ks), hcols], qt_ref[0, hcols, q_lo:])

    def step(t, diagonal, next_t):
        pending = [ahead_ref[j] for j in range(QK_LOOKAHEAD)]
        for i, (hd, sub) in enumerate(units):
            s = pending.pop(0)
            j = i + QK_LOOKAHEAD
            if j < len(units):
                pending.append(logits(t, units[j], diagonal))
            elif next_t is not None:
                ahead_ref[j - len(units)] = logits(next_t, units[j - len(units)], False)
            q_lo = sub * ks if diagonal else 0
            if diagonal:
                key = lax.broadcasted_iota(jnp.int32, s.shape, 0)
                query = lax.broadcasted_iota(jnp.int32, s.shape, 1)
                s = jnp.where(key <= query, s, MASK_VALUE)
            vt = vt_ref[0, t, hd * VT_ROWS:(hd + 1) * VT_ROWS, sub * ks:(sub + 1) * ks]
            if running_max:
                m_prev = m_ref[hd, :, q_lo:]
                m_new = jnp.maximum(m_prev, jnp.max(s, axis=0, keepdims=True))
                alpha = jnp.exp2(m_prev - m_new)
                p = _bf16(jnp.exp2(s - m_new))
                acc_ref[hd, :, q_lo:] = alpha * acc_ref[hd, :, q_lo:] + _dot(vt, p)
                m_ref[hd, :, q_lo:] = m_new
            else:
                acc_ref[hd, :, q_lo:] += _dot(vt, _bf16(jnp.exp2(s)))

    if running_max:
        m_ref[...] = jnp.full(m_ref.shape, MASK_VALUE, jnp.float32)
    acc_ref[...] = jnp.zeros(acc_ref.shape, jnp.float32)
    for j in range(QK_LOOKAHEAD):
        ahead_ref[j] = logits(0, units[j], False)

    def body(t, carry):
        step(t, False, t + 1)
        return carry

    lax.fori_loop(0, qi, body, 0)
    step(qi, True, None)

    for pair in range(HEADS // 2):
        y_t = []
        for hd in (2 * pair, 2 * pair + 1):
            acc = acc_ref[hd]
            y_t.append(acc[:MLA_V_DIM] / acc[VT_ONES_ROW:VT_ONES_ROW + 1])
        y = jnp.concatenate(y_t, axis=0).T
        o_ref[0, :, pair * LANES:(pair + 1) * LANES] = y.astype(o_ref.dtype)


def _attn_call(qt, k, vt, *, running_max, name):
    b, s, _ = k.shape
    assert ATTN_Q_TILE == ATTN_KV_TILE == TOKEN_TILE
    return pl.pallas_call(
        functools.partial(_attn_kernel, running_max=running_max),
        grid=(b, s // ATTN_Q_TILE),
        in_specs=[
            pl.BlockSpec((1, HP_WIDTH, ATTN_Q_TILE), lambda bi, i: (bi, 0, i)),
            pl.BlockSpec((1, s, HP_WIDTH), lambda bi, i: (bi, 0, 0)),
            pl.BlockSpec((1,) + vt.shape[1:], lambda bi, i: (bi, 0, 0, 0)),
        ],
        out_specs=pl.BlockSpec((1, ATTN_Q_TILE, V_WIDTH), lambda bi, i: (bi, i, 0)),
        out_shape=jax.ShapeDtypeStruct((b, s, V_WIDTH), jnp.bfloat16),
        scratch_shapes=[
            pltpu.VMEM((HEADS, 1, ATTN_Q_TILE), jnp.float32),
            pltpu.VMEM((HEADS, VT_ROWS, ATTN_Q_TILE), jnp.float32),
            pltpu.VMEM((QK_LOOKAHEAD, ATTN_KV_SUB, ATTN_Q_TILE), jnp.float32),
        ],
        compiler_params=pltpu.CompilerParams(
            dimension_semantics=("arbitrary", "arbitrary"), vmem_limit_bytes=VMEM_LIMIT),
        name=name,
    )(qt, k, vt)


def _attention(q, k, vt, shift_is_safe, name):
    return lax.cond(
        shift_is_safe,
        functools.partial(_attn_call, running_max=False, name=name + "_shifted"),
        functools.partial(_attn_call, running_max=True, name=name),
        q, k, vt)


def _merge_kernel(x_ref, ym_ref, yf_ref, gm_ref, gf_ref, bg_ref, wa_ref, wb_ref, wo_ref,
                  gain_ref, wg_ref, wu_ref, wd_ref, o_ref, t_ref):
    mixed = (jax.nn.sigmoid(gm_ref[...] + bg_ref[0:1, :]) * _dot(ym_ref[...], wa_ref[...])
             + jax.nn.sigmoid(gf_ref[...] + bg_ref[1:2, :]) * _dot(yf_ref[...], wb_ref[...]))
    x = x_ref[...] + _dot(_bf16(mixed), wo_ref[...])
    o_ref[...] = _swiglu_residual(x, gain_ref[...], wg_ref, wu_ref, wd_ref, t_ref)


def _merge_call(x, ym, yf, gm, gf, bg, wa, wb, wo, gain, wg, wu, wd):
    n = x.shape[0]
    tile = lambda w: pl.BlockSpec((TOKEN_TILE, w), lambda i: (i, 0))
    consts = (bg, wa, wb, wo, gain, wg, wu, wd)
    return pl.pallas_call(
        _merge_kernel,
        grid=(n // TOKEN_TILE,),
        in_specs=[tile(D_MODEL), tile(V_WIDTH), tile(V_WIDTH), tile(D_MODEL), tile(D_MODEL)]
        + [_const_spec(c.shape) for c in consts],
        out_specs=tile(D_MODEL),
        out_shape=jax.ShapeDtypeStruct((n, D_MODEL), jnp.float32),
        scratch_shapes=[pltpu.VMEM((TOKEN_TILE, FFN_HIDDEN), jnp.bfloat16)],
        compiler_params=pltpu.CompilerParams(
            dimension_semantics=("arbitrary",), vmem_limit_bytes=VMEM_LIMIT),
        name="merge_ffn2",
    )(x, ym, yf, gm, gf, *consts)


def _pad_heads(w, width):
    rows = w.shape[0]
    w = w.reshape(rows, HEADS, -1)
    return jnp.pad(w, ((0, 0), (0, 0), (0, width - w.shape[2]))).reshape(rows, HEADS * width)


def _head_lanes(vec, lane0=0):
    tile = jnp.zeros((HEAD_PAD,), jnp.float32).at[lane0:lane0 + vec.shape[0]].set(vec)
    return jnp.tile(tile, HEADS)[None, :]


def _proj_constants(mix_norm, w_in, q_lat_norm, w_qb, kv_lat_norm, w_kvb, q_nope_gain, q_rope_gain,
                    k_nope_gain, k_rope_gain, fox_q_gain, fox_k_gain, fox_b_f):
    f32 = jnp.float32
    o_qlat, o_kvlat = 0, Q_LORA_RANK
    o_kr = o_kvlat + KV_LORA_RANK
    o_fq = o_kr + MLA_ROPE_DIM
    o_fk = o_fq + V_WIDTH
    o_fv = o_fk + V_WIDTH
    o_fl = o_fv + V_WIDTH
    o_gm = o_fl + HEADS
    o_gf = o_gm + D_MODEL

    def zero_cols(n):
        return jnp.zeros((D_MODEL, n), w_in.dtype)

    win = _bf16(jnp.concatenate([
        w_in[:, o_qlat:o_kvlat], zero_cols(Q_LAT_PAD - Q_LORA_RANK),
        w_in[:, o_kvlat:o_kr],
        w_in[:, o_fl:o_gm], zero_cols(ROPE_LANE0 - HEADS),
        w_in[:, o_kr:o_fq], zero_cols(LANES - ROPE_LANE0 - MLA_ROPE_DIM),
        w_in[:, o_fq:o_fl],
        w_in[:, o_gm:],
    ], axis=1))
    assert win.shape[1] == PROJ_WIDTH

    qk_dim = MLA_NOPE_DIM + MLA_ROPE_DIM
    wqb = _bf16(jnp.pad(_pad_heads(w_qb, HEAD_PAD), ((0, Q_LAT_PAD - Q_LORA_RANK), (0, 0))))
    w_kvb = w_kvb.reshape(KV_LORA_RANK, HEADS, MLA_NOPE_DIM + MLA_V_DIM)
    wkvb = _bf16(jnp.concatenate([
        w_kvb[:, :, :MLA_NOPE_DIM].reshape(KV_LORA_RANK, V_WIDTH),
        w_kvb[:, :, MLA_NOPE_DIM:].reshape(KV_LORA_RANK, V_WIDTH)], axis=1))
    vtail = np.zeros((VT_ROWS - MLA_V_DIM, LANES), np.float32)
    vtail[VT_ONES_ROW - MLA_V_DIM] = 1.0
    vtail = jnp.asarray(vtail)

    r = np.arange(TOKEN_TILE)
    tri = jnp.asarray(r[:, None] >= r[None, :], jnp.bfloat16)
    l256 = np.arange(256)
    same_head = (l256[:, None] // HEAD_PAD) == (l256[None, :] // HEAD_PAD)
    in_nope = (l256 % HEAD_PAD) < MLA_NOPE_DIM
    in_rope = ((l256 % HEAD_PAD) >= ROPE_LANE0) & ((l256 % HEAD_PAD) < ROPE_LANE0 + MLA_ROPE_DIM)
    seg_q = same_head & ((in_nope[:, None] & in_nope[None, :]) | (in_rope[:, None] & in_rope[None, :]))
    seg_q = jnp.asarray(seg_q, jnp.bfloat16)
    seg_64 = jnp.asarray((l256[:, None] // FOX_HEAD_DIM) == (l256[None, :] // FOX_HEAD_DIM),
                         jnp.bfloat16)
    eq = np.zeros((LANES, HP_WIDTH), np.float32)
    ek = np.zeros((LANES, HP_WIDTH), np.float32)
    oneq = np.zeros((1, HP_WIDTH), np.float32)
    onek = np.zeros((1, HP_WIDTH), np.float32)
    for hd in range(HEADS):
        for piece in range(3):
            eq[piece * HEADS + hd, hd * HEAD_PAD + BIAS_LANE0 + piece] = 1.0
            ek[piece * HEADS + hd, hd * HEAD_PAD + BIAS_LANE0 + 3 + piece] = -1.0
            oneq[0, hd * HEAD_PAD + BIAS_LANE0 + 3 + piece] = 1.0
            onek[0, hd * HEAD_PAD + BIAS_LANE0 + piece] = 1.0
    eq = jnp.asarray(eq, jnp.bfloat16)
    ek = jnp.asarray(ek, jnp.bfloat16)

    mla_scale = float(qk_dim) ** -0.5 * LOG2E
    fox_scale = float(FOX_HEAD_DIM) ** -0.5 * LOG2E
    q_gain = _head_lanes(jnp.concatenate([q_nope_gain, q_rope_gain]) * mla_scale)
    icnt = np.ones((HEAD_PAD,), np.float32)
    icnt[:MLA_NOPE_DIM] = 1.0 / MLA_NOPE_DIM
    icnt[ROPE_LANE0:ROPE_LANE0 + MLA_ROPE_DIM] = 1.0 / MLA_ROPE_DIM
    q_icnt = jnp.asarray(np.tile(icnt, HEADS)[None, :])
    kn_gain = jnp.tile(k_nope_gain, HEADS)[None, :]
    kr_gain = jnp.zeros((1, LANES), f32).at[0, ROPE_LANE0:ROPE_LANE0 + MLA_ROPE_DIM].set(k_rope_gain)
    fq_gain = jnp.tile(fox_q_gain * fox_scale, HEADS)[None, :]
    fk_gain = jnp.tile(fox_k_gain, HEADS)[None, :]
    bf = jnp.zeros((1, LANES), f32).at[0, :HEADS].set(fox_b_f)
    inv_freq = ROPE_THETA ** (-jnp.arange(ROPE_HALF, dtype=f32) / ROPE_HALF)
    freq = jnp.zeros((1, LANES), f32).at[0, ROPE_LANE0:ROPE_LANE0 + MLA_ROPE_DIM].set(
        jnp.concatenate([inv_freq, inv_freq]))
    qlat_gain = jnp.pad(q_lat_norm, (0, Q_LAT_PAD - Q_LORA_RANK))[None, :]

    amax = lambda g: jnp.max(jnp.abs(g))
    bound_fox = (FOX_HEAD_DIM * amax(fox_q_gain) * amax(fox_k_gain) * fox_scale) * BOUND_SLACK
    q_sq = MLA_NOPE_DIM * amax(q_nope_gain) ** 2 + MLA_ROPE_DIM * amax(q_rope_gain) ** 2
    k_sq = MLA_NOPE_DIM * amax(k_nope_gain) ** 2 + MLA_ROPE_DIM * amax(k_rope_gain) ** 2
    bound_mla = jnp.sqrt(q_sq * k_sq) * mla_scale * BOUND_SLACK
    shift_lane = np.zeros((1, LANES), np.float32)
    shift_lane[0, MLA_SHIFT_LANE] = 1.0
    kone = jnp.asarray(shift_lane)
    qshift = kone * -(bound_mla - SHIFT_MARGIN)
    fox_shift_lanes = np.zeros((1, HP_WIDTH), np.float32)
    fox_shift_lanes[0, FOX_SHIFT_LANE::HEAD_PAD] = 1.0
    oneq = jnp.asarray(oneq) + jnp.asarray(fox_shift_lanes) * -(bound_fox - SHIFT_MARGIN)
    onek = jnp.asarray(onek + fox_shift_lanes)

    consts = (mix_norm[None, :], win, wqb, wkvb, vtail, tri, seg_q, seg_64, eq, ek,
              qlat_gain, kv_lat_norm[None, :], q_gain, q_icnt, kn_gain, kr_gain, fq_gain, fk_gain,
              bf, freq, oneq, onek, qshift, kone)
    return consts, bound_mla <= SHIFT_BOUND_LIMIT, bound_fox <= SHIFT_BOUND_LIMIT


def kernel(x, positions, ffn1_norm, ffn1_w_gate, ffn1_w_up, ffn1_w_down, mix_norm, w_in, mla_q_lat_norm, mla_w_qb, mla_kv_lat_norm, mla_w_kvb, mla_q_nope_gain, mla_q_rope_gain, mla_k_nope_gain, mla_k_rope_gain, fox_q_gain, fox_k_gain, fox_b_f, w_branch_mla, w_branch_fox, b_gate, w_o, ffn2_norm, ffn2_w_gate, ffn2_w_up, ffn2_w_down):
    b, s, d = x.shape
    n = b * s
    pos = positions.reshape(b, s, 1)
    xt = x.reshape(n, d)
    for l in range(ffn1_norm.shape[0]):
        xt = _ffn_call(xt, ffn1_norm[l][None, :], _bf16(ffn1_w_gate[l]), _bf16(ffn1_w_up[l]),
                       _bf16(ffn1_w_down[l]))
        consts, mla_shift_safe, fox_shift_safe = _proj_constants(
            mix_norm[l], w_in[l], mla_q_lat_norm[l], mla_w_qb[l], mla_kv_lat_norm[l], mla_w_kvb[l],
            mla_q_nope_gain[l], mla_q_rope_gain[l], mla_k_nope_gain[l], mla_k_rope_gain[l],
            fox_q_gain[l], fox_k_gain[l], fox_b_f[l])
        qm, km, vm, qf, kf, vf, gm, gf = _proj_call(xt.reshape(b, s, d), pos, consts)
        ym = _attention(qm, km, vm, mla_shift_safe, "attn_mla")
        yf = _attention(qf, kf, vf, fox_shift_safe, "attn_fox")
        xt = _merge_call(
            xt, ym.reshape(n, V_WIDTH), yf.reshape(n, V_WIDTH), gm.reshape(n, d), gf.reshape(n, d),
            b_gate[l], _bf16(w_branch_mla[l]), _bf16(w_branch_fox[l]), _bf16(w_o[l]),
            ffn2_norm[l][None, :], _bf16(ffn2_w_gate[l]), _bf16(ffn2_w_up[l]), _bf16(ffn2_w_down[l]))
    return xt.reshape(b, s, d)
```

```python
import functools

import numpy as np
import jax
import jax.numpy as jnp
from jax import lax
from jax.experimental import pallas as pl
from jax.experimental.pallas import tpu as pltpu

D_MODEL = 1024
FFN_HIDDEN = 2816
FFN_RESIDUAL_WEIGHT = 0.5
HEADS = 8
MLA_NOPE_DIM = 64
MLA_ROPE_DIM = 32
MLA_V_DIM = 64
Q_LORA_RANK = 192
KV_LORA_RANK = 128
ROPE_THETA = 10000.0
FOX_HEAD_DIM = 64
RMS_EPS = 1e-6

LANES = 128
HEAD_PAD = LANES
HP_WIDTH = HEADS * HEAD_PAD
V_WIDTH = HEADS * MLA_V_DIM
Q_LAT_PAD = 256
ROPE_HALF = MLA_ROPE_DIM // 2
ROPE_LANE0 = MLA_NOPE_DIM
BIAS_LANE0 = FOX_HEAD_DIM
FOX_SHIFT_LANE = BIAS_LANE0 + 6
MLA_SHIFT_LANE = ROPE_LANE0 + MLA_ROPE_DIM
SHIFT_MARGIN = 40.0
SHIFT_BOUND_LIMIT = 68.0
BOUND_SLACK = 1.05

TOKEN_TILE = 512
FFN_TILE = 1024
HIDDEN_CHUNK = 256
GATE_CHUNK = 256
ATTN_Q_TILE = 512
ATTN_KV_TILE = 512
ATTN_KV_SUB = 256
QK_LOOKAHEAD = 2
MASK_VALUE = -1e30
VMEM_LIMIT = 56 * 1024 * 1024

PROJ_QLAT = 0
PROJ_KVLAT = PROJ_QLAT + Q_LAT_PAD
PROJ_KR = PROJ_KVLAT + KV_LORA_RANK
PROJ_FQ = PROJ_KR + LANES
PROJ_FK = PROJ_FQ + V_WIDTH
PROJ_FV = PROJ_FK + V_WIDTH
PROJ_GM = PROJ_FV + V_WIDTH
PROJ_GF = PROJ_GM + D_MODEL
PROJ_WIDTH = PROJ_GF + D_MODEL

VT_ROWS = 80
VT_ONES_ROW = MLA_V_DIM
VT_WIDTH = HEADS * VT_ROWS
LOG2E = 1.4426950408889634


def _bf16(x):
    return x.astype(jnp.bfloat16)


def _dot(a, b):
    return jnp.dot(a, b, preferred_element_type=jnp.float32)


def _rms_scale(x, n):
    return lax.rsqrt(jnp.sum(x * x, axis=-1, keepdims=True) * (1.0 / n) + RMS_EPS)


def _split2(x):
    hi = _bf16(x)
    return hi, _bf16(x - hi.astype(jnp.float32))


def _split3(x):
    hi = _bf16(x)
    mid, lo = _split2(x - hi.astype(jnp.float32))
    return hi, mid, lo


def _segment_sumsq(x, seg_ref):
    seg = seg_ref[...]
    outs = []
    for c in range(x.shape[1] // 256):
        xs = x[:, c * 256:(c + 1) * 256]
        outs.append(_dot(_bf16(xs * xs), seg))
    return jnp.concatenate(outs, axis=1)


def _swiglu_residual(x, gain, wg_ref, wu_ref, wd_ref, t_ref):
    h = _bf16(x * _rms_scale(x, D_MODEL) * gain)
    for c in range(FFN_HIDDEN // HIDDEN_CHUNK):
        cols = slice(c * HIDDEN_CHUNK, (c + 1) * HIDDEN_CHUNK)
        a = _dot(h, wg_ref[:, cols])
        b = _dot(h, wu_ref[:, cols])
        t_ref[:, cols] = _bf16(a * jax.nn.sigmoid(a) * b)
    return x + FFN_RESIDUAL_WEIGHT * _dot(t_ref[...], wd_ref[...])


def _ffn_kernel(x_ref, gain_ref, wg_ref, wu_ref, wd_ref, o_ref, t_ref):
    o_ref[...] = _swiglu_residual(x_ref[...], gain_ref[...], wg_ref, wu_ref, wd_ref, t_ref)


def _const_spec(shape):
    return pl.BlockSpec(shape, lambda *_: (0,) * len(shape), pipeline_mode=pl.Buffered(1))


def _ffn_call(x, gain, wg, wu, wd):
    n = x.shape[0]
    return pl.pallas_call(
        _ffn_kernel,
        grid=(n // FFN_TILE,),
        in_specs=[
            pl.BlockSpec((FFN_TILE, D_MODEL), lambda i: (i, 0)),
            _const_spec((1, D_MODEL)),
            _const_spec((D_MODEL, FFN_HIDDEN)),
            _const_spec((D_MODEL, FFN_HIDDEN)),
            _const_spec((FFN_HIDDEN, D_MODEL)),
        ],
        out_specs=pl.BlockSpec((FFN_TILE, D_MODEL), lambda i: (i, 0)),
        out_shape=jax.ShapeDtypeStruct((n, D_MODEL), jnp.float32),
        scratch_shapes=[pltpu.VMEM((FFN_TILE, FFN_HIDDEN), jnp.bfloat16)],
        compiler_params=pltpu.CompilerParams(
            dimension_semantics=("arbitrary",), vmem_limit_bytes=VMEM_LIMIT),
        name="ffn1",
    )(x, gain, wg, wu, wd)


def _proj_kernel(x_ref, posrow_ref, gain_ref, wlat_ref, wfox_ref, wgate_ref, wqb_ref, wkvb_ref,
                 vtail_ref,
                 tri_ref, segq_ref, seg64_ref, eq_ref, ek_ref,
                 qlat_gain_ref, kvlat_gain_ref, q_gain_ref, q_icnt_ref, kn_gain_ref,
                 kr_gain_ref, fq_gain_ref, fk_gain_ref, bf_ref, freq_ref, oneq_ref, onek_ref,
                 qshift_ref, kone_ref,
                 qm_ref, km_ref, vm_ref, qf_ref, kf_ref, vf_ref, gm_ref, gf_ref,
                 carry_ref):
    x = x_ref[0]
    h = _bf16(x * _rms_scale(x, D_MODEL) * gain_ref[...])

    def proj(lo, width):
        for ref, base in ((wlat_ref, PROJ_QLAT), (wfox_ref, PROJ_FQ), (wgate_ref, PROJ_GM)):
            if base <= lo and lo + width <= base + ref.shape[1]:
                return _dot(h, ref[:, lo - base:lo - base + width])
        raise ValueError("projection columns straddle two weight pieces")

    gate_chunks = [(ref, col0, c * GATE_CHUNK) for ref, col0 in ((gm_ref, PROJ_GM), (gf_ref, PROJ_GF))
                   for c in range(D_MODEL // GATE_CHUNK)]

    def gates(n):
        for _ in range(min(n, len(gate_chunks))):
            ref, col0, c0 = gate_chunks.pop(0)
            ref[0, :, c0:c0 + GATE_CHUNK] = proj(col0 + c0, GATE_CHUNK)

    gates(2)
    lane = lax.broadcasted_iota(jnp.int32, (1, LANES), 1)
    rope_lanes = (lane >= ROPE_LANE0) & (lane < ROPE_LANE0 + MLA_ROPE_DIM)

    ang = (jnp.concatenate([freq_ref[...]] * (TOKEN_TILE // LANES), axis=1)
           * posrow_ref[0].astype(jnp.float32))
    cos16, sin16 = jnp.cos(ang), jnp.sin(ang)
    fill = lambda n, v: jnp.full((n, TOKEN_TILE), v, jnp.float32)
    pad_hi = LANES - ROPE_LANE0 - MLA_ROPE_DIM
    cos_t = jnp.concatenate([fill(ROPE_LANE0, 1.0), cos16, cos16, fill(pad_hi, 1.0)], axis=0).T
    sin_hi = jnp.concatenate([fill(ROPE_LANE0 + ROPE_HALF, 0.0), sin16, fill(pad_hi, 0.0)],
                             axis=0).T
    sin_lo = jnp.concatenate([fill(ROPE_LANE0, 0.0), -sin16, fill(ROPE_HALF + pad_hi, 0.0)],
                             axis=0).T

    def rotary(t):
        return (t * cos_t + pltpu.roll(t, ROPE_HALF, 1) * sin_hi
                + pltpu.roll(t, LANES - ROPE_HALF, 1) * sin_lo)

    def store_values_t(ref, v):
        tail = jnp.concatenate([vtail_ref[...]] * (TOKEN_TILE // LANES), axis=1)
        blocks = []
        for pair in range(HEADS // 2):
            pair_t = v[:, pair * LANES:(pair + 1) * LANES].T
            blocks += [pair_t[:MLA_V_DIM], tail, pair_t[MLA_V_DIM:], tail]
        ref[0, 0] = _bf16(jnp.concatenate(blocks, axis=0))

    def head_normed(t, g_ref):
        return t * lax.rsqrt(_segment_sumsq(t, seg64_ref) * (1.0 / FOX_HEAD_DIM) + RMS_EPS) * g_ref[...]

    def head_tiles(t):
        low = lane < FOX_HEAD_DIM
        tiles = []
        for pair in range(HEADS // 2):
            src = t[:, pair * LANES:(pair + 1) * LANES]
            tiles.append(jnp.where(low, src, 0.0))
            tiles.append(jnp.where(low, pltpu.roll(src, LANES - FOX_HEAD_DIM, 1), 0.0))
        return tiles

    gates(1)
    q_lat = proj(PROJ_QLAT, Q_LAT_PAD)
    q_lat = _bf16(q_lat * _rms_scale(q_lat, Q_LORA_RANK) * qlat_gain_ref[...])
    q = _dot(q_lat, wqb_ref[...])
    gates(1)
    q = q * lax.rsqrt(_segment_sumsq(q, segq_ref) * q_icnt_ref[...] + RMS_EPS) * q_gain_ref[...]
    for hd in range(HEADS):
        if hd % 2 == 0:
            gates(1)
        cols = slice(hd * HEAD_PAD, (hd + 1) * HEAD_PAD)
        qm_ref[0, cols, :] = _bf16((rotary(q[:, cols]) + qshift_ref[...]).T)

    kv_lat = proj(PROJ_KVLAT, KV_LORA_RANK)
    kv_lat = _bf16(kv_lat * _rms_scale(kv_lat, KV_LORA_RANK) * kvlat_gain_ref[...])
    gates(1)
    kv = _dot(kv_lat, wkvb_ref[...])
    k_nope = head_tiles(head_normed(kv[:, :V_WIDTH], kn_gain_ref))
    store_values_t(vm_ref, kv[:, V_WIDTH:])

    kr_grp = proj(PROJ_KR, LANES)
    kr = jnp.where(rope_lanes, kr_grp, 0.0)
    kr = rotary(kr * _rms_scale(kr, MLA_ROPE_DIM) * kr_gain_ref[...]) + kone_ref[...]
    gates(1)
    for hd in range(HEADS):
        cols = slice(hd * HEAD_PAD, (hd + 1) * HEAD_PAD)
        km_ref[0, :, cols] = _bf16(k_nope[hd] + kr)
    gates(1)

    @pl.when(pl.program_id(1) == 0)
    def _():
        carry_ref[...] = jnp.zeros_like(carry_ref)

    log_f = jnp.where(lane < HEADS, jax.nn.log_sigmoid(kr_grp + bf_ref[...]), 0.0)
    tri = tri_ref[...]
    f_hi, f_lo = _split2(log_f)
    c = _dot(tri, f_hi) + _dot(tri, f_lo) + carry_ref[0:1, :]
    carry_ref[0:1, :] = c[TOKEN_TILE - 1:TOKEN_TILE, :]
    c2 = c * LOG2E
    c_rep = c2 + pltpu.roll(c2, HEADS, 1) + pltpu.roll(c2, 2 * HEADS, 1)
    c_hi, c_mid, c_lo = _split3(c_rep)
    c_parts = jnp.where(lane < HEADS, c_hi, jnp.where(lane < 2 * HEADS, c_mid, c_lo))

    for o_ref, col0, g_ref, e_ref, one_ref in ((qf_ref, PROJ_FQ, fq_gain_ref, eq_ref, oneq_ref),
                                               (kf_ref, PROJ_FK, fk_gain_ref, ek_ref, onek_ref)):
        gates(1)
        tiles = head_tiles(head_normed(proj(col0, V_WIDTH), g_ref))
        gates(1)
        bias = _dot(c_parts, e_ref[...]) + one_ref[...]
        for hd in range(HEADS):
            cols = slice(hd * HEAD_PAD, (hd + 1) * HEAD_PAD)
            tile_hd = tiles[hd] + bias[:, cols]
            if o_ref is qf_ref:
                o_ref[0, cols, :] = _bf16(tile_hd.T)
            else:
                o_ref[0, :, cols] = _bf16(tile_hd)
    store_values_t(vf_ref, proj(PROJ_FV, V_WIDTH))
    gates(len(gate_chunks))


def _proj_call(x, pos, consts):
    b, s, _ = x.shape
    bf16, f32 = jnp.bfloat16, jnp.float32
    n_tiles = s // TOKEN_TILE
    tile = lambda w: pl.BlockSpec((1, TOKEN_TILE, w), lambda bi, i: (bi, i, 0))
    tile_t = pl.BlockSpec((1, 1, VT_WIDTH, TOKEN_TILE), lambda bi, i: (bi, i, 0, 0))
    rows = lambda w, d: jax.ShapeDtypeStruct((b, s, w), d)
    rows_t = jax.ShapeDtypeStruct((b, n_tiles, VT_WIDTH, TOKEN_TILE), bf16)
    tile_q = pl.BlockSpec((1, HP_WIDTH, TOKEN_TILE), lambda bi, i: (bi, 0, i))
    rows_q = jax.ShapeDtypeStruct((b, HP_WIDTH, s), bf16)
    return pl.pallas_call(
        _proj_kernel,
        grid=(b, n_tiles),
        in_specs=[tile(D_MODEL), pl.BlockSpec((1, 1, TOKEN_TILE), lambda bi, i: (bi, 0, i))]
        + [_const_spec(c.shape) for c in consts],
        out_specs=[tile_q, tile(HP_WIDTH), tile_t, tile_q, tile(HP_WIDTH), tile_t,
                   tile(D_MODEL), tile(D_MODEL)],
        out_shape=[rows_q, rows(HP_WIDTH, bf16), rows_t, rows_q, rows(HP_WIDTH, bf16), rows_t,
                   rows(D_MODEL, f32), rows(D_MODEL, f32)],
        scratch_shapes=[pltpu.VMEM((8, LANES), jnp.float32)],
        compiler_params=pltpu.CompilerParams(
            dimension_semantics=("arbitrary", "arbitrary"), vmem_limit_bytes=VMEM_LIMIT),
        name="proj",
    )(x, pos, *consts)


def _attn_kernel(qt_ref, k_ref, vt_ref, o_ref, m_ref, acc_ref, ahead_ref, *, running_max):
    qi = pl.program_id(1)
    tq, tk, ks = ATTN_Q_TILE, ATTN_KV_TILE, ATTN_KV_SUB
    units = [(hd, sub) for sub in range(tk // ks) for hd in range(HEADS)]

    def logits(t, unit, diagonal):
        hd, sub = unit
        hcols = slice(hd * HEAD_PAD, (hd + 1) * HEAD_PAD)
        q_lo = sub * ks if diagonal else 0
        off = pl.multiple_of(t * tk + sub * ks, ks)
        return _dot(k_ref[0, pl.ds(off, ks), hcols], qt_ref[0, hcols, q_lo:])

    def step(t, diagonal, next_t):
        pending = [ahead_ref[j] for j in range(QK_LOOKAHEAD)]
        for i, (hd, sub) in enumerate(units):
            s = pending.pop(0)
            j = i + QK_LOOKAHEAD
            if j < len(units):
                pending.append(logits(t, units[j], diagonal))
            elif next_t is not None:
                ahead_ref[j - len(units)] = logits(next_t, units[j - len(units)], False)
            q_lo = sub * ks if diagonal else 0
            if diagonal:
                key = lax.broadcasted_iota(jnp.int32, s.shape, 0)
                query = lax.broadcasted_iota(jnp.int32, s.shape, 1)
                s = jnp.where(key <= query, s, MASK_VALUE)
            vt = vt_ref[0, t, hd * VT_ROWS:(hd + 1) * VT_ROWS, sub * ks:(sub + 1) * ks]
            if running_max:
                m_prev = m_ref[hd, :, q_lo:]
                m_new = jnp.maximum(m_prev, jnp.max(s, axis=0, keepdims=True))
                alpha = jnp.exp2(m_prev - m_new)
                p = _bf16(jnp.exp2(s - m_new))
                acc_ref[hd, :, q_lo:] = alpha * acc_ref[hd, :, q_lo:] + _dot(vt, p)
                m_ref[hd, :, q_lo:] = m_new
            else:
                acc_ref[hd, :, q_lo:] += _dot(vt, _bf16(jnp.exp2(s)))

    if running_max:
        m_ref[...] = jnp.full(m_ref.shape, MASK_VALUE, jnp.float32)
    acc_ref[...] = jnp.zeros(acc_ref.shape, jnp.float32)
    for j in range(QK_LOOKAHEAD):
        ahead_ref[j] = logits(0, units[j], False)

    def body(t, carry):
        step(t, False, t + 1)
        return carry

    lax.fori_loop(0, qi, body, 0)
    step(qi, True, None)

    for pair in range(HEADS // 2):
        y_t = []
        for hd in (2 * pair, 2 * pair + 1):
            acc = acc_ref[hd]
            y_t.append(acc[:MLA_V_DIM] / acc[VT_ONES_ROW:VT_ONES_ROW + 1])
        y = jnp.concatenate(y_t, axis=0).T
        o_ref[0, :, pair * LANES:(pair + 1) * LANES] = y.astype(o_ref.dtype)


def _attn_call(qt, k, vt, *, running_max, name):
    b, s, _ = k.shape
    assert ATTN_Q_TILE == ATTN_KV_TILE == TOKEN_TILE
    return pl.pallas_call(
        functools.partial(_attn_kernel, running_max=running_max),
        grid=(b, s // ATTN_Q_TILE),
        in_specs=[
            pl.BlockSpec((1, HP_WIDTH, ATTN_Q_TILE), lambda bi, i: (bi, 0, i)),
            pl.BlockSpec((1, s, HP_WIDTH), lambda bi, i: (bi, 0, 0)),
            pl.BlockSpec((1,) + vt.shape[1:], lambda bi, i: (bi, 0, 0, 0)),
        ],
        out_specs=pl.BlockSpec((1, ATTN_Q_TILE, V_WIDTH), lambda bi, i: (bi, i, 0)),
        out_shape=jax.ShapeDtypeStruct((b, s, V_WIDTH), jnp.bfloat16),
        scratch_shapes=[
            pltpu.VMEM((HEADS, 1, ATTN_Q_TILE), jnp.float32),
            pltpu.VMEM((HEADS, VT_ROWS, ATTN_Q_TILE), jnp.float32),
            pltpu.VMEM((QK_LOOKAHEAD, ATTN_KV_SUB, ATTN_Q_TILE), jnp.float32),
        ],
        compiler_params=pltpu.CompilerParams(
            dimension_semantics=("arbitrary", "arbitrary"), vmem_limit_bytes=VMEM_LIMIT),
        name=name,
    )(qt, k, vt)


def _attention(q, k, vt, shift_is_safe, name):
    return lax.cond(
        shift_is_safe,
        functools.partial(_attn_call, running_max=False, name=name + "_shifted"),
        functools.partial(_attn_call, running_max=True, name=name),
        q, k, vt)


def _merge_kernel(x_ref, ym_ref, yf_ref, gm_ref, gf_ref, bg_ref, wa_ref, wb_ref, wo_ref,
                  gain_ref, wg_ref, wu_ref, wd_ref, o_ref, t_ref):
    mixed = (jax.nn.sigmoid(gm_ref[...] + bg_ref[0:1, :]) * _dot(ym_ref[...], wa_ref[...])
             + jax.nn.sigmoid(gf_ref[...] + bg_ref[1:2, :]) * _dot(yf_ref[...], wb_ref[...]))
    x = x_ref[...] + _dot(_bf16(mixed), wo_ref[...])
    o_ref[...] = _swiglu_residual(x, gain_ref[...], wg_ref, wu_ref, wd_ref, t_ref)


def _merge_call(x, ym, yf, gm, gf, bg, wa, wb, wo, gain, wg, wu, wd):
    n = x.shape[0]
    tile = lambda w: pl.BlockSpec((TOKEN_TILE, w), lambda i: (i, 0))
    consts = (bg, wa, wb, wo, gain, wg, wu, wd)
    return pl.pallas_call(
        _merge_kernel,
        grid=(n // TOKEN_TILE,),
        in_specs=[tile(D_MODEL), tile(V_WIDTH), tile(V_WIDTH), tile(D_MODEL), tile(D_MODEL)]
        + [_const_spec(c.shape) for c in consts],
        out_specs=tile(D_MODEL),
        out_shape=jax.ShapeDtypeStruct((n, D_MODEL), jnp.float32),
        scratch_shapes=[pltpu.VMEM((TOKEN_TILE, FFN_HIDDEN), jnp.bfloat16)],
        compiler_params=pltpu.CompilerParams(
            dimension_semantics=("arbitrary",), vmem_limit_bytes=VMEM_LIMIT),
        name="merge_ffn2",
    )(x, ym, yf, gm, gf, *consts)


def _pad_heads(w, width):
    rows = w.shape[0]
    w = w.reshape(rows, HEADS, -1)
    return jnp.pad(w, ((0, 0), (0, 0), (0, width - w.shape[2]))).reshape(rows, HEADS * width)


def _head_lanes(vec, lane0=0):
    tile = jnp.zeros((HEAD_PAD,), jnp.float32).at[lane0:lane0 + vec.shape[0]].set(vec)
    return jnp.tile(tile, HEADS)[None, :]


def _proj_constants(mix_norm, w_in, q_lat_norm, w_qb, kv_lat_norm, w_kvb, q_nope_gain, q_rope_gain,
                    k_nope_gain, k_rope_gain, fox_q_gain, fox_k_gain, fox_b_f):
    f32 = jnp.float32
    o_qlat, o_kvlat = 0, Q_LORA_RANK
    o_kr = o_kvlat + KV_LORA_RANK
    o_fq = o_kr + MLA_ROPE_DIM
    o_fk = o_fq + V_WIDTH
    o_fv = o_fk + V_WIDTH
    o_fl = o_fv + V_WIDTH
    o_gm = o_fl + HEADS
    o_gf = o_gm + D_MODEL

    def zero_cols(n):
        return jnp.zeros((D_MODEL, n), w_in.dtype)

    wlat = _bf16(jnp.concatenate([
        w_in[:, o_qlat:o_kvlat], zero_cols(Q_LAT_PAD - Q_LORA_RANK),
        w_in[:, o_kvlat:o_kr],
        w_in[:, o_fl:o_gm], zero_cols(ROPE_LANE0 - HEADS),
        w_in[:, o_kr:o_fq], zero_cols(LANES - ROPE_LANE0 - MLA_ROPE_DIM),
    ], axis=1))
    wfox = _bf16(w_in[:, o_fq:o_fl])
    wgate = _bf16(w_in[:, o_gm:])
    assert wlat.shape[1] == PROJ_FQ and wfox.shape[1] == PROJ_GM - PROJ_FQ
    assert wgate.shape[1] == PROJ_WIDTH - PROJ_GM

    qk_dim = MLA_NOPE_DIM + MLA_ROPE_DIM
    wqb = _bf16(jnp.pad(_pad_heads(w_qb, HEAD_PAD), ((0, Q_LAT_PAD - Q_LORA_RANK), (0, 0))))
    w_kvb = w_kvb.reshape(KV_LORA_RANK, HEADS, MLA_NOPE_DIM + MLA_V_DIM)
    wkvb = _bf16(jnp.concatenate([
        w_kvb[:, :, :MLA_NOPE_DIM].reshape(KV_LORA_RANK, V_WIDTH),
        w_kvb[:, :, MLA_NOPE_DIM:].reshape(KV_LORA_RANK, V_WIDTH)], axis=1))
    vtail = np.zeros((VT_ROWS - MLA_V_DIM, LANES), np.float32)
    vtail[VT_ONES_ROW - MLA_V_DIM] = 1.0
    vtail = jnp.asarray(vtail)

    r = np.arange(TOKEN_TILE)
    tri = jnp.asarray(r[:, None] >= r[None, :], jnp.bfloat16)
    l256 = np.arange(256)
    same_head = (l256[:, None] // HEAD_PAD) == (l256[None, :] // HEAD_PAD)
    in_nope = (l256 % HEAD_PAD) < MLA_NOPE_DIM
    in_rope = ((l256 % HEAD_PAD) >= ROPE_LANE0) & ((l256 % HEAD_PAD) < ROPE_LANE0 + MLA_ROPE_DIM)
    seg_q = same_head & ((in_nope[:, None] & in_nope[None, :]) | (in_rope[:, None] & in_rope[None, :]))
    seg_q = jnp.asarray(seg_q, jnp.bfloat16)
    seg_64 = jnp.asarray((l256[:, None] // FOX_HEAD_DIM) == (l256[None, :] // FOX_HEAD_DIM),
                         jnp.bfloat16)
    eq = np.zeros((LANES, HP_WIDTH), np.float32)
    ek = np.zeros((LANES, HP_WIDTH), np.float32)
    oneq = np.zeros((1, HP_WIDTH), np.float32)
    onek = np.zeros((1, HP_WIDTH), np.float32)
    for hd in range(HEADS):
        for piece in range(3):
            eq[piece * HEADS + hd, hd * HEAD_PAD + BIAS_LANE0 + piece] = 1.0
            ek[piece * HEADS + hd, hd * HEAD_PAD + BIAS_LANE0 + 3 + piece] = -1.0
            oneq[0, hd * HEAD_PAD + BIAS_LANE0 + 3 + piece] = 1.0
            onek[0, hd * HEAD_PAD + BIAS_LANE0 + piece] = 1.0
    eq = jnp.asarray(eq, jnp.bfloat16)
    ek = jnp.asarray(ek, jnp.bfloat16)

    mla_scale = float(qk_dim) ** -0.5 * LOG2E
    fox_scale = float(FOX_HEAD_DIM) ** -0.5 * LOG2E
    q_gain = _head_lanes(jnp.concatenate([q_nope_gain, q_rope_gain]) * mla_scale)
    icnt = np.ones((HEAD_PAD,), np.float32)
    icnt[:MLA_NOPE_DIM] = 1.0 / MLA_NOPE_DIM
    icnt[ROPE_LANE0:ROPE_LANE0 + MLA_ROPE_DIM] = 1.0 / MLA_ROPE_DIM
    q_icnt = jnp.asarray(np.tile(icnt, HEADS)[None, :])
    kn_gain = jnp.tile(k_nope_gain, HEADS)[None, :]
    kr_gain = jnp.zeros((1, LANES), f32).at[0, ROPE_LANE0:ROPE_LANE0 + MLA_ROPE_DIM].set(k_rope_gain)
    fq_gain = jnp.tile(fox_q_gain * fox_scale, HEADS)[None, :]
    fk_gain = jnp.tile(fox_k_gain, HEADS)[None, :]
    bf = jnp.zeros((1, LANES), f32).at[0, :HEADS].set(fox_b_f)
    inv_freq = ROPE_THETA ** (-jnp.arange(ROPE_HALF, dtype=f32) / ROPE_HALF)
    freq = jnp.broadcast_to(inv_freq[:, None], (ROPE_HALF, LANES))
    qlat_gain = jnp.pad(q_lat_norm, (0, Q_LAT_PAD - Q_LORA_RANK))[None, :]

    amax = lambda g: jnp.max(jnp.abs(g))
    bound_fox = (FOX_HEAD_DIM * amax(fox_q_gain) * amax(fox_k_gain) * fox_scale) * BOUND_SLACK
    q_sq = MLA_NOPE_DIM * amax(q_nope_gain) ** 2 + MLA_ROPE_DIM * amax(q_rope_gain) ** 2
    k_sq = MLA_NOPE_DIM * amax(k_nope_gain) ** 2 + MLA_ROPE_DIM * amax(k_rope_gain) ** 2
    bound_mla = jnp.sqrt(q_sq * k_sq) * mla_scale * BOUND_SLACK
    shift_lane = np.zeros((1, LANES), np.float32)
    shift_lane[0, MLA_SHIFT_LANE] = 1.0
    kone = jnp.asarray(shift_lane)
    qshift = kone * -(bound_mla - SHIFT_MARGIN)
    fox_shift_lanes = np.zeros((1, HP_WIDTH), np.float32)
    fox_shift_lanes[0, FOX_SHIFT_LANE::HEAD_PAD] = 1.0
    oneq = jnp.asarray(oneq) + jnp.asarray(fox_shift_lanes) * -(bound_fox - SHIFT_MARGIN)
    onek = jnp.asarray(onek + fox_shift_lanes)

    consts = (mix_norm[None, :], wlat, wfox, wgate, wqb, wkvb, vtail, tri, seg_q, seg_64, eq, ek,
              qlat_gain, kv_lat_norm[None, :], q_gain, q_icnt, kn_gain, kr_gain, fq_gain, fk_gain,
              bf, freq, oneq, onek, qshift, kone)
    return consts, bound_mla <= SHIFT_BOUND_LIMIT, bound_fox <= SHIFT_BOUND_LIMIT


def kernel(x, positions, ffn1_norm, ffn1_w_gate, ffn1_w_up, ffn1_w_down, mix_norm, w_in, mla_q_lat_norm, mla_w_qb, mla_kv_lat_norm, mla_w_kvb, mla_q_nope_gain, mla_q_rope_gain, mla_k_nope_gain, mla_k_rope_gain, fox_q_gain, fox_k_gain, fox_b_f, w_branch_mla, w_branch_fox, b_gate, w_o, ffn2_norm, ffn2_w_gate, ffn2_w_up, ffn2_w_down):
    b, s, d = x.shape
    n = b * s
    pos = positions.reshape(b, 1, s)
    xt = x.reshape(n, d)
    for l in range(ffn1_norm.shape[0]):
        xt = _ffn_call(xt, ffn1_norm[l][None, :], _bf16(ffn1_w_gate[l]), _bf16(ffn1_w_up[l]),
                       _bf16(ffn1_w_down[l]))
        consts, mla_shift_safe, fox_shift_safe = _proj_constants(
            mix_norm[l], w_in[l], mla_q_lat_norm[l], mla_w_qb[l], mla_kv_lat_norm[l], mla_w_kvb[l],
            mla_q_nope_gain[l], mla_q_rope_gain[l], mla_k_nope_gain[l], mla_k_rope_gain[l],
            fox_q_gain[l], fox_k_gain[l], fox_b_f[l])
        qm, km, vm, qf, kf, vf, gm, gf = _proj_call(xt.reshape(b, s, d), pos, consts)
        ym = _attention(qm, km, vm, mla_shift_safe, "attn_mla")
        yf = _attention(qf, kf, vf, fox_shift_safe, "attn_fox")
        xt = _merge_call(
            xt, ym.reshape(n, V_WIDTH), yf.reshape(n, V_WIDTH), gm.reshape(n, d), gf.reshape(n, d),
            b_gate[l], _bf16(w_branch_mla[l]), _bf16(w_branch_fox[l]), _bf16(w_o[l]),
            ffn2_norm[l][None, :], _bf16(ffn2_w_gate[l]), _bf16(ffn2_w_up[l]), _bf16(ffn2_w_down[l]))
    return xt.reshape(b, s, d)
```

```python
import functools

import numpy as np
import jax
import jax.numpy as jnp
from jax import lax
from jax.experimental import pallas as pl
from jax.experimental.pallas import tpu as pltpu

D_MODEL = 1024
FFN_HIDDEN = 2816
FFN_RESIDUAL_WEIGHT = 0.5
HEADS = 8
MLA_NOPE_DIM = 64
MLA_ROPE_DIM = 32
MLA_V_DIM = 64
Q_LORA_RANK = 192
KV_LORA_RANK = 128
ROPE_THETA = 10000.0
FOX_HEAD_DIM = 64
RMS_EPS = 1e-6

LANES = 128
HEAD_PAD = LANES
HP_WIDTH = HEADS * HEAD_PAD
V_WIDTH = HEADS * MLA_V_DIM
Q_LAT_PAD = 256
ROPE_HALF = MLA_ROPE_DIM // 2
ROPE_LANE0 = MLA_NOPE_DIM
BIAS_LANE0 = FOX_HEAD_DIM
FOX_SHIFT_LANE = BIAS_LANE0 + 6
MLA_SHIFT_LANE = ROPE_LANE0 + MLA_ROPE_DIM
SHIFT_MARGIN = 40.0
SHIFT_BOUND_LIMIT = 68.0
BOUND_SLACK = 1.05

TOKEN_TILE = 512
PROJ_ROWS = 256
FFN_TILE = 1024
HIDDEN_CHUNK = 256
GATE_CHUNK = 256
ATTN_Q_TILE = 512
ATTN_KV_TILE = 512
ATTN_KV_SUB = 256
QK_LOOKAHEAD = 2
MASK_VALUE = -1e30
VMEM_LIMIT = 56 * 1024 * 1024

PROJ_QLAT = 0
PROJ_KVLAT = PROJ_QLAT + Q_LAT_PAD
PROJ_KR = PROJ_KVLAT + KV_LORA_RANK
PROJ_FQ = PROJ_KR + LANES
PROJ_FK = PROJ_FQ + V_WIDTH
PROJ_FV = PROJ_FK + V_WIDTH
PROJ_GM = PROJ_FV + V_WIDTH
PROJ_GF = PROJ_GM + D_MODEL
PROJ_WIDTH = PROJ_GF + D_MODEL

VT_ROWS = 80
VT_ONES_ROW = MLA_V_DIM
VT_WIDTH = HEADS * VT_ROWS
LOG2E = 1.4426950408889634


def _bf16(x):
    return x.astype(jnp.bfloat16)


def _dot(a, b):
    return jnp.dot(a, b, preferred_element_type=jnp.float32)


def _rms_scale(x, n):
    return lax.rsqrt(jnp.sum(x * x, axis=-1, keepdims=True) * (1.0 / n) + RMS_EPS)


def _split2(x):
    hi = _bf16(x)
    return hi, _bf16(x - hi.astype(jnp.float32))


def _split3(x):
    hi = _bf16(x)
    mid, lo = _split2(x - hi.astype(jnp.float32))
    return hi, mid, lo


def _segment_sumsq(x, seg_ref):
    seg = seg_ref[...]
    outs = []
    for c in range(x.shape[1] // 256):
        xs = x[:, c * 256:(c + 1) * 256]
        outs.append(_dot(_bf16(xs * xs), seg))
    return jnp.concatenate(outs, axis=1)


def _swiglu_residual(x, gain, wg_ref, wu_ref, wd_ref, t_ref):
    h = _bf16(x * _rms_scale(x, D_MODEL) * gain)
    for c in range(FFN_HIDDEN // HIDDEN_CHUNK):
        cols = slice(c * HIDDEN_CHUNK, (c + 1) * HIDDEN_CHUNK)
        a = _dot(h, wg_ref[:, cols])
        b = _dot(h, wu_ref[:, cols])
        t_ref[:, cols] = _bf16(a * jax.nn.sigmoid(a) * b)
    return x + FFN_RESIDUAL_WEIGHT * _dot(t_ref[...], wd_ref[...])


def _ffn_kernel(x_ref, gain_ref, wg_ref, wu_ref, wd_ref, o_ref, t_ref):
    o_ref[...] = _swiglu_residual(x_ref[...], gain_ref[...], wg_ref, wu_ref, wd_ref, t_ref)


def _const_spec(shape):
    return pl.BlockSpec(shape, lambda *_: (0,) * len(shape), pipeline_mode=pl.Buffered(1))


def _ffn_call(x, gain, wg, wu, wd):
    n = x.shape[0]
    return pl.pallas_call(
        _ffn_kernel,
        grid=(n // FFN_TILE,),
        in_specs=[
            pl.BlockSpec((FFN_TILE, D_MODEL), lambda i: (i, 0)),
            _const_spec((1, D_MODEL)),
            _const_spec((D_MODEL, FFN_HIDDEN)),
            _const_spec((D_MODEL, FFN_HIDDEN)),
            _const_spec((FFN_HIDDEN, D_MODEL)),
        ],
        out_specs=pl.BlockSpec((FFN_TILE, D_MODEL), lambda i: (i, 0)),
        out_shape=jax.ShapeDtypeStruct((n, D_MODEL), jnp.float32),
        scratch_shapes=[pltpu.VMEM((FFN_TILE, FFN_HIDDEN), jnp.bfloat16)],
        compiler_params=pltpu.CompilerParams(
            dimension_semantics=("arbitrary",), vmem_limit_bytes=VMEM_LIMIT),
        name="ffn1",
    )(x, gain, wg, wu, wd)


def _proj_kernel(x_ref, posrow_ref, gain_ref, wlat_ref, wfox_ref, wgate_ref, wqb_ref, wkvb_ref,
                 vtail_ref,
                 tri_ref, segq_ref, seg64_ref, eq_ref, ek_ref,
                 qlat_gain_ref, kvlat_gain_ref, q_gain_ref, q_icnt_ref, kn_gain_ref,
                 kr_gain_ref, fq_gain_ref, fk_gain_ref, bf_ref, freq_ref, oneq_ref, onek_ref,
                 qshift_ref, kone_ref,
                 qm_ref, km_ref, vm_ref, qf_ref, kf_ref, vf_ref, gm_ref, gf_ref,
                 carry_ref):
    lane = lax.broadcasted_iota(jnp.int32, (1, LANES), 1)
    rope_lanes = (lane >= ROPE_LANE0) & (lane < ROPE_LANE0 + MLA_ROPE_DIM)
    n_rows = PROJ_ROWS

    @pl.when(pl.program_id(1) == 0)
    def _():
        carry_ref[...] = jnp.zeros_like(carry_ref)

    def rows_program(r0):
        rows = slice(r0, r0 + n_rows)
        x = x_ref[0, rows, :]
        h = _bf16(x * _rms_scale(x, D_MODEL) * gain_ref[...])

        def proj(lo, width):
            for ref, base in ((wlat_ref, PROJ_QLAT), (wfox_ref, PROJ_FQ), (wgate_ref, PROJ_GM)):
                if base <= lo and lo + width <= base + ref.shape[1]:
                    return _dot(h, ref[:, lo - base:lo - base + width])
            raise ValueError("projection columns straddle two weight pieces")

        gate_chunks = [(ref, col0, c * GATE_CHUNK)
                       for ref, col0 in ((gm_ref, PROJ_GM), (gf_ref, PROJ_GF))
                       for c in range(D_MODEL // GATE_CHUNK)]

        def gates(n):
            for _ in range(min(n, len(gate_chunks))):
                ref, col0, c0 = gate_chunks.pop(0)
                ref[0, rows, c0:c0 + GATE_CHUNK] = proj(col0 + c0, GATE_CHUNK)

        gates(2)
        yield

        ang = (jnp.concatenate([freq_ref[...]] * (n_rows // LANES), axis=1)
               * posrow_ref[0, :, rows].astype(jnp.float32))
        cos16, sin16 = jnp.cos(ang), jnp.sin(ang)
        fill = lambda n, v: jnp.full((n, n_rows), v, jnp.float32)
        pad_hi = LANES - ROPE_LANE0 - MLA_ROPE_DIM
        cos_t = jnp.concatenate([fill(ROPE_LANE0, 1.0), cos16, cos16, fill(pad_hi, 1.0)], axis=0).T
        sin_hi = jnp.concatenate([fill(ROPE_LANE0 + ROPE_HALF, 0.0), sin16, fill(pad_hi, 0.0)],
                                 axis=0).T
        sin_lo = jnp.concatenate([fill(ROPE_LANE0, 0.0), -sin16, fill(ROPE_HALF + pad_hi, 0.0)],
                                 axis=0).T

        def rotary(t):
            return (t * cos_t + pltpu.roll(t, ROPE_HALF, 1) * sin_hi
                    + pltpu.roll(t, LANES - ROPE_HALF, 1) * sin_lo)

        def store_values_t(ref, v):
            tail = jnp.concatenate([vtail_ref[...]] * (n_rows // LANES), axis=1)
            blocks = []
            for pair in range(HEADS // 2):
                pair_t = v[:, pair * LANES:(pair + 1) * LANES].T
                blocks += [pair_t[:MLA_V_DIM], tail, pair_t[MLA_V_DIM:], tail]
            ref[0, 0, :, rows] = _bf16(jnp.concatenate(blocks, axis=0))

        def head_normed(t, g_ref):
            return (t * lax.rsqrt(_segment_sumsq(t, seg64_ref) * (1.0 / FOX_HEAD_DIM) + RMS_EPS)
                    * g_ref[...])

        def head_tiles(t):
            low = lane < FOX_HEAD_DIM
            tiles = []
            for pair in range(HEADS // 2):
                src = t[:, pair * LANES:(pair + 1) * LANES]
                tiles.append(jnp.where(low, src, 0.0))
                tiles.append(jnp.where(low, pltpu.roll(src, LANES - FOX_HEAD_DIM, 1), 0.0))
            return tiles

        gates(1)
        q_lat = proj(PROJ_QLAT, Q_LAT_PAD)
        yield
        q_lat = _bf16(q_lat * _rms_scale(q_lat, Q_LORA_RANK) * qlat_gain_ref[...])
        q = _dot(q_lat, wqb_ref[...])
        gates(1)
        yield
        q = q * lax.rsqrt(_segment_sumsq(q, segq_ref) * q_icnt_ref[...] + RMS_EPS) * q_gain_ref[...]
        yield
        for hd in range(HEADS):
            if hd % 2 == 0:
                gates(1)
            cols = slice(hd * HEAD_PAD, (hd + 1) * HEAD_PAD)
            q_t = (q[:, cols] + qshift_ref[...]).T
            x1 = q_t[ROPE_LANE0:ROPE_LANE0 + ROPE_HALF]
            x2 = q_t[ROPE_LANE0 + ROPE_HALF:ROPE_LANE0 + MLA_ROPE_DIM]
            qm_ref[0, cols, rows] = _bf16(jnp.concatenate(
                [q_t[:ROPE_LANE0], x1 * cos16 - x2 * sin16, x2 * cos16 + x1 * sin16,
                 q_t[ROPE_LANE0 + MLA_ROPE_DIM:]], axis=0))
            if hd % 2 == 1:
                yield

        kv_lat = proj(PROJ_KVLAT, KV_LORA_RANK)
        kv_lat = _bf16(kv_lat * _rms_scale(kv_lat, KV_LORA_RANK) * kvlat_gain_ref[...])
        gates(1)
        kv = _dot(kv_lat, wkvb_ref[...])
        yield
        k_nope = head_tiles(head_normed(kv[:, :V_WIDTH], kn_gain_ref))
        store_values_t(vm_ref, kv[:, V_WIDTH:])
        yield

        kr_grp = proj(PROJ_KR, LANES)
        kr = jnp.where(rope_lanes, kr_grp, 0.0)
        kr = rotary(kr * _rms_scale(kr, MLA_ROPE_DIM) * kr_gain_ref[...]) + kone_ref[...]
        gates(1)
        for hd in range(HEADS):
            cols = slice(hd * HEAD_PAD, (hd + 1) * HEAD_PAD)
            km_ref[0, rows, cols] = _bf16(k_nope[hd] + kr)
        gates(1)
        yield

        log_f = jnp.where(lane < HEADS, jax.nn.log_sigmoid(kr_grp + bf_ref[...]), 0.0)
        tri = tri_ref[...]
        f_hi, f_lo = _split2(log_f)
        c = _dot(tri, f_hi) + _dot(tri, f_lo) + carry_ref[0:1, :]
        carry_ref[0:1, :] = c[n_rows - 1:n_rows, :]
        yield
        c2 = c * LOG2E
        c_rep = c2 + pltpu.roll(c2, HEADS, 1) + pltpu.roll(c2, 2 * HEADS, 1)
        c_hi, c_mid, c_lo = _split3(c_rep)
        c_parts = jnp.where(lane < HEADS, c_hi, jnp.where(lane < 2 * HEADS, c_mid, c_lo))

        for o_ref, col0, g_ref, e_ref, one_ref in ((qf_ref, PROJ_FQ, fq_gain_ref, eq_ref, oneq_ref),
                                                   (kf_ref, PROJ_FK, fk_gain_ref, ek_ref, onek_ref)):
            gates(1)
            t = proj(col0, V_WIDTH)
            yield
            tiles = head_tiles(head_normed(t, g_ref))
            gates(1)
            bias = _dot(c_parts, e_ref[...]) + one_ref[...]
            yield
            for hd in range(HEADS):
                cols = slice(hd * HEAD_PAD, (hd + 1) * HEAD_PAD)
                tile_hd = tiles[hd] + bias[:, cols]
                if o_ref is qf_ref:
                    o_ref[0, cols, rows] = _bf16(tile_hd.T)
                else:
                    o_ref[0, rows, cols] = _bf16(tile_hd)
            yield
        store_values_t(vf_ref, proj(PROJ_FV, V_WIDTH))
        gates(len(gate_chunks))

    programs = [rows_program(r0) for r0 in range(0, TOKEN_TILE, n_rows)]
    while programs:
        for program in list(programs):
            if next(program, "done") == "done":
                programs.remove(program)


def _proj_call(x, pos, consts):
    b, s, _ = x.shape
    bf16, f32 = jnp.bfloat16, jnp.float32
    n_tiles = s // TOKEN_TILE
    tile = lambda w: pl.BlockSpec((1, TOKEN_TILE, w), lambda bi, i: (bi, i, 0))
    tile_t = pl.BlockSpec((1, 1, VT_WIDTH, TOKEN_TILE), lambda bi, i: (bi, i, 0, 0))
    rows = lambda w, d: jax.ShapeDtypeStruct((b, s, w), d)
    rows_t = jax.ShapeDtypeStruct((b, n_tiles, VT_WIDTH, TOKEN_TILE), bf16)
    tile_q = pl.BlockSpec((1, HP_WIDTH, TOKEN_TILE), lambda bi, i: (bi, 0, i))
    rows_q = jax.ShapeDtypeStruct((b, HP_WIDTH, s), bf16)
    return pl.pallas_call(
        _proj_kernel,
        grid=(b, n_tiles),
        in_specs=[tile(D_MODEL), pl.BlockSpec((1, 1, TOKEN_TILE), lambda bi, i: (bi, 0, i))]
        + [_const_spec(c.shape) for c in consts],
        out_specs=[tile_q, tile(HP_WIDTH), tile_t, tile_q, tile(HP_WIDTH), tile_t,
                   tile(D_MODEL), tile(D_MODEL)],
        out_shape=[rows_q, rows(HP_WIDTH, bf16), rows_t, rows_q, rows(HP_WIDTH, bf16), rows_t,
                   rows(D_MODEL, f32), rows(D_MODEL, f32)],
        scratch_shapes=[pltpu.VMEM((8, LANES), jnp.float32)],
        compiler_params=pltpu.CompilerParams(
            dimension_semantics=("arbitrary", "arbitrary"), vmem_limit_bytes=VMEM_LIMIT),
        name="proj",
    )(x, pos, *consts)


def _attn_kernel(qt_ref, k_ref, vt_ref, o_ref, m_ref, acc_ref, ahead_ref, *, running_max):
    qi = pl.program_id(1)
    tq, tk, ks = ATTN_Q_TILE, ATTN_KV_TILE, ATTN_KV_SUB
    units = [(hd, sub) for sub in range(tk // ks) for hd in range(HEADS)]

    def logits(t, unit, diagonal):
        hd, sub = unit
        hcols = slice(hd * HEAD_PAD, (hd + 1) * HEAD_PAD)
        q_lo = sub * ks if diagonal else 0
        off = pl.multiple_of(t * tk + sub * ks, ks)
        return _dot(k_ref[0, pl.ds(off, ks), hcols], qt_ref[0, hcols, q_lo:])

    def step(t, diagonal, next_t):
        pending = [ahead_ref[j] for j in range(QK_LOOKAHEAD)]
        for i, (hd, sub) in enumerate(units):
            s = pending.pop(0)
            j = i + QK_LOOKAHEAD
            if j < len(units):
                pending.append(logits(t, units[j], diagonal))
            elif next_t is not None:
                ahead_ref[j - len(units)] = logits(next_t, units[j - len(units)], False)
            q_lo = sub * ks if diagonal else 0
            if diagonal:
                key = lax.broadcasted_iota(jnp.int32, s.shape, 0)
                query = lax.broadcasted_iota(jnp.int32, s.shape, 1)
                s = jnp.where(key <= query, s, MASK_VALUE)
            vt = vt_ref[0, t, hd * VT_ROWS:(hd + 1) * VT_ROWS, sub * ks:(sub + 1) * ks]
            if running_max:
                m_prev = m_ref[hd, :, q_lo:]
                m_new = jnp.maximum(m_prev, jnp.max(s, axis=0, keepdims=True))
                alpha = jnp.exp2(m_prev - m_new)
                p = _bf16(jnp.exp2(s - m_new))
                acc_ref[hd, :, q_lo:] = alpha * acc_ref[hd, :, q_lo:] + _dot(vt, p)
                m_ref[hd, :, q_lo:] = m_new
            else:
                acc_ref[hd, :, q_lo:] += _dot(vt, _bf16(jnp.exp2(s)))

    if running_max:
        m_ref[...] = jnp.full(m_ref.shape, MASK_VALUE, jnp.float32)
    acc_ref[...] = jnp.zeros(acc_ref.shape, jnp.float32)
    for j in range(QK_LOOKAHEAD):
        ahead_ref[j] = logits(0, units[j], False)

    def body(t, carry):
        step(t, False, t + 1)
        return carry

    lax.fori_loop(0, qi, body, 0)
    step(qi, True, None)

    for pair in range(HEADS // 2):
        y_t = []
        for hd in (2 * pair, 2 * pair + 1):
            acc = acc_ref[hd]
            y_t.append(acc[:MLA_V_DIM] / acc[VT_ONES_ROW:VT_ONES_ROW + 1])
        y = jnp.concatenate(y_t, axis=0).T
        o_ref[0, :, pair * LANES:(pair + 1) * LANES] = y.astype(o_ref.dtype)


def _attn_call(qt, k, vt, *, running_max, name):
    b, s, _ = k.shape
    assert ATTN_Q_TILE == ATTN_KV_TILE == TOKEN_TILE
    return pl.pallas_call(
        functools.partial(_attn_kernel, running_max=running_max),
        grid=(b, s // ATTN_Q_TILE),
        in_specs=[
            pl.BlockSpec((1, HP_WIDTH, ATTN_Q_TILE), lambda bi, i: (bi, 0, i)),
            pl.BlockSpec((1, s, HP_WIDTH), lambda bi, i: (bi, 0, 0)),
            pl.BlockSpec((1,) + vt.shape[1:], lambda bi, i: (bi, 0, 0, 0)),
        ],
        out_specs=pl.BlockSpec((1, ATTN_Q_TILE, V_WIDTH), lambda bi, i: (bi, i, 0)),
        out_shape=jax.ShapeDtypeStruct((b, s, V_WIDTH), jnp.bfloat16),
        scratch_shapes=[
            pltpu.VMEM((HEADS, 1, ATTN_Q_TILE), jnp.float32),
            pltpu.VMEM((HEADS, VT_ROWS, ATTN_Q_TILE), jnp.float32),
            pltpu.VMEM((QK_LOOKAHEAD, ATTN_KV_SUB, ATTN_Q_TILE), jnp.float32),
        ],
        compiler_params=pltpu.CompilerParams(
            dimension_semantics=("arbitrary", "arbitrary"), vmem_limit_bytes=VMEM_LIMIT),
        name=name,
    )(qt, k, vt)


def _attention(q, k, vt, shift_is_safe, name):
    return lax.cond(
        shift_is_safe,
        functools.partial(_attn_call, running_max=False, name=name + "_shifted"),
        functools.partial(_attn_call, running_max=True, name=name),
        q, k, vt)


def _merge_kernel(x_ref, ym_ref, yf_ref, gm_ref, gf_ref, bg_ref, wa_ref, wb_ref, wo_ref,
                  gain_ref, wg_ref, wu_ref, wd_ref, o_ref, t_ref):
    mixed = (jax.nn.sigmoid(gm_ref[...] + bg_ref[0:1, :]) * _dot(ym_ref[...], wa_ref[...])
             + jax.nn.sigmoid(gf_ref[...] + bg_ref[1:2, :]) * _dot(yf_ref[...], wb_ref[...]))
    x = x_ref[...] + _dot(_bf16(mixed), wo_ref[...])
    o_ref[...] = _swiglu_residual(x, gain_ref[...], wg_ref, wu_ref, wd_ref, t_ref)


def _merge_call(x, ym, yf, gm, gf, bg, wa, wb, wo, gain, wg, wu, wd):
    n = x.shape[0]
    tile = lambda w: pl.BlockSpec((TOKEN_TILE, w), lambda i: (i, 0))
    consts = (bg, wa, wb, wo, gain, wg, wu, wd)
    return pl.pallas_call(
        _merge_kernel,
        grid=(n // TOKEN_TILE,),
        in_specs=[tile(D_MODEL), tile(V_WIDTH), tile(V_WIDTH), tile(D_MODEL), tile(D_MODEL)]
        + [_const_spec(c.shape) for c in consts],
        out_specs=tile(D_MODEL),
        out_shape=jax.ShapeDtypeStruct((n, D_MODEL), jnp.float32),
        scratch_shapes=[pltpu.VMEM((TOKEN_TILE, FFN_HIDDEN), jnp.bfloat16)],
        compiler_params=pltpu.CompilerParams(
            dimension_semantics=("arbitrary",), vmem_limit_bytes=VMEM_LIMIT),
        name="merge_ffn2",
    )(x, ym, yf, gm, gf, *consts)


def _pad_heads(w, width):
    rows = w.shape[0]
    w = w.reshape(rows, HEADS, -1)
    return jnp.pad(w, ((0, 0), (0, 0), (0, width - w.shape[2]))).reshape(rows, HEADS * width)


def _head_lanes(vec, lane0=0):
    tile = jnp.zeros((HEAD_PAD,), jnp.float32).at[lane0:lane0 + vec.shape[0]].set(vec)
    return jnp.tile(tile, HEADS)[None, :]


def _proj_constants(mix_norm, w_in, q_lat_norm, w_qb, kv_lat_norm, w_kvb, q_nope_gain, q_rope_gain,
                    k_nope_gain, k_rope_gain, fox_q_gain, fox_k_gain, fox_b_f):
    f32 = jnp.float32
    o_qlat, o_kvlat = 0, Q_LORA_RANK
    o_kr = o_kvlat + KV_LORA_RANK
    o_fq = o_kr + MLA_ROPE_DIM
    o_fk = o_fq + V_WIDTH
    o_fv = o_fk + V_WIDTH
    o_fl = o_fv + V_WIDTH
    o_gm = o_fl + HEADS
    o_gf = o_gm + D_MODEL

    def zero_cols(n):
        return jnp.zeros((D_MODEL, n), w_in.dtype)

    wlat = _bf16(jnp.concatenate([
        w_in[:, o_qlat:o_kvlat], zero_cols(Q_LAT_PAD - Q_LORA_RANK),
        w_in[:, o_kvlat:o_kr],
        w_in[:, o_fl:o_gm], zero_cols(ROPE_LANE0 - HEADS),
        w_in[:, o_kr:o_fq], zero_cols(LANES - ROPE_LANE0 - MLA_ROPE_DIM),
    ], axis=1))
    wfox = _bf16(w_in[:, o_fq:o_fl])
    wgate = _bf16(w_in[:, o_gm:])
    assert wlat.shape[1] == PROJ_FQ and wfox.shape[1] == PROJ_GM - PROJ_FQ
    assert wgate.shape[1] == PROJ_WIDTH - PROJ_GM

    qk_dim = MLA_NOPE_DIM + MLA_ROPE_DIM
    wqb = _bf16(jnp.pad(_pad_heads(w_qb, HEAD_PAD), ((0, Q_LAT_PAD - Q_LORA_RANK), (0, 0))))
    w_kvb = w_kvb.reshape(KV_LORA_RANK, HEADS, MLA_NOPE_DIM + MLA_V_DIM)
    wkvb = _bf16(jnp.concatenate([
        w_kvb[:, :, :MLA_NOPE_DIM].reshape(KV_LORA_RANK, V_WIDTH),
        w_kvb[:, :, MLA_NOPE_DIM:].reshape(KV_LORA_RANK, V_WIDTH)], axis=1))
    vtail = np.zeros((VT_ROWS - MLA_V_DIM, LANES), np.float32)
    vtail[VT_ONES_ROW - MLA_V_DIM] = 1.0
    vtail = jnp.asarray(vtail)

    r = np.arange(PROJ_ROWS)
    tri = jnp.asarray(r[:, None] >= r[None, :], jnp.bfloat16)
    l256 = np.arange(256)
    same_head = (l256[:, None] // HEAD_PAD) == (l256[None, :] // HEAD_PAD)
    in_nope = (l256 % HEAD_PAD) < MLA_NOPE_DIM
    in_rope = ((l256 % HEAD_PAD) >= ROPE_LANE0) & ((l256 % HEAD_PAD) < ROPE_LANE0 + MLA_ROPE_DIM)
    seg_q = same_head & ((in_nope[:, None] & in_nope[None, :]) | (in_rope[:, None] & in_rope[None, :]))
    seg_q = jnp.asarray(seg_q, jnp.bfloat16)
    seg_64 = jnp.asarray((l256[:, None] // FOX_HEAD_DIM) == (l256[None, :] // FOX_HEAD_DIM),
                         jnp.bfloat16)
    eq = np.zeros((LANES, HP_WIDTH), np.float32)
    ek = np.zeros((LANES, HP_WIDTH), np.float32)
    oneq = np.zeros((1, HP_WIDTH), np.float32)
    onek = np.zeros((1, HP_WIDTH), np.float32)
    for hd in range(HEADS):
        for piece in range(3):
            eq[piece * HEADS + hd, hd * HEAD_PAD + BIAS_LANE0 + piece] = 1.0
            ek[piece * HEADS + hd, hd * HEAD_PAD + BIAS_LANE0 + 3 + piece] = -1.0
            oneq[0, hd * HEAD_PAD + BIAS_LANE0 + 3 + piece] = 1.0
            onek[0, hd * HEAD_PAD + BIAS_LANE0 + piece] = 1.0
    eq = jnp.asarray(eq, jnp.bfloat16)
    ek = jnp.asarray(ek, jnp.bfloat16)

    mla_scale = float(qk_dim) ** -0.5 * LOG2E
    fox_scale = float(FOX_HEAD_DIM) ** -0.5 * LOG2E
    q_gain = _head_lanes(jnp.concatenate([q_nope_gain, q_rope_gain]) * mla_scale)
    icnt = np.ones((HEAD_PAD,), np.float32)
    icnt[:MLA_NOPE_DIM] = 1.0 / MLA_NOPE_DIM
    icnt[ROPE_LANE0:ROPE_LANE0 + MLA_ROPE_DIM] = 1.0 / MLA_ROPE_DIM
    q_icnt = jnp.asarray(np.tile(icnt, HEADS)[None, :])
    kn_gain = jnp.tile(k_nope_gain, HEADS)[None, :]
    kr_gain = jnp.zeros((1, LANES), f32).at[0, ROPE_LANE0:ROPE_LANE0 + MLA_ROPE_DIM].set(k_rope_gain)
    fq_gain = jnp.tile(fox_q_gain * fox_scale, HEADS)[None, :]
    fk_gain = jnp.tile(fox_k_gain, HEADS)[None, :]
    bf = jnp.zeros((1, LANES), f32).at[0, :HEADS].set(fox_b_f)
    inv_freq = ROPE_THETA ** (-jnp.arange(ROPE_HALF, dtype=f32) / ROPE_HALF)
    freq = jnp.broadcast_to(inv_freq[:, None], (ROPE_HALF, LANES))
    qlat_gain = jnp.pad(q_lat_norm, (0, Q_LAT_PAD - Q_LORA_RANK))[None, :]

    amax = lambda g: jnp.max(jnp.abs(g))
    bound_fox = (FOX_HEAD_DIM * amax(fox_q_gain) * amax(fox_k_gain) * fox_scale) * BOUND_SLACK
    q_sq = MLA_NOPE_DIM * amax(q_nope_gain) ** 2 + MLA_ROPE_DIM * amax(q_rope_gain) ** 2
    k_sq = MLA_NOPE_DIM * amax(k_nope_gain) ** 2 + MLA_ROPE_DIM * amax(k_rope_gain) ** 2
    bound_mla = jnp.sqrt(q_sq * k_sq) * mla_scale * BOUND_SLACK
    shift_lane = np.zeros((1, LANES), np.float32)
    shift_lane[0, MLA_SHIFT_LANE] = 1.0
    kone = jnp.asarray(shift_lane)
    qshift = kone * -(bound_mla - SHIFT_MARGIN)
    fox_shift_lanes = np.zeros((1, HP_WIDTH), np.float32)
    fox_shift_lanes[0, FOX_SHIFT_LANE::HEAD_PAD] = 1.0
    oneq = jnp.asarray(oneq) + jnp.asarray(fox_shift_lanes) * -(bound_fox - SHIFT_MARGIN)
    onek = jnp.asarray(onek + fox_shift_lanes)

    consts = (mix_norm[None, :], wlat, wfox, wgate, wqb, wkvb, vtail, tri, seg_q, seg_64, eq, ek,
              qlat_gain, kv_lat_norm[None, :], q_gain, q_icnt, kn_gain, kr_gain, fq_gain, fk_gain,
              bf, freq, oneq, onek, qshift, kone)
    return consts, bound_mla <= SHIFT_BOUND_LIMIT, bound_fox <= SHIFT_BOUND_LIMIT


def kernel(x, positions, ffn1_norm, ffn1_w_gate, ffn1_w_up, ffn1_w_down, mix_norm, w_in, mla_q_lat_norm, mla_w_qb, mla_kv_lat_norm, mla_w_kvb, mla_q_nope_gain, mla_q_rope_gain, mla_k_nope_gain, mla_k_rope_gain, fox_q_gain, fox_k_gain, fox_b_f, w_branch_mla, w_branch_fox, b_gate, w_o, ffn2_norm, ffn2_w_gate, ffn2_w_up, ffn2_w_down):
    b, s, d = x.shape
    n = b * s
    pos = positions.reshape(b, 1, s)
    xt = x.reshape(n, d)
    for l in range(ffn1_norm.shape[0]):
        xt = _ffn_call(xt, ffn1_norm[l][None, :], _bf16(ffn1_w_gate[l]), _bf16(ffn1_w_up[l]),
                       _bf16(ffn1_w_down[l]))
        consts, mla_shift_safe, fox_shift_safe = _proj_constants(
            mix_norm[l], w_in[l], mla_q_lat_norm[l], mla_w_qb[l], mla_kv_lat_norm[l], mla_w_kvb[l],
            mla_q_nope_gain[l], mla_q_rope_gain[l], mla_k_nope_gain[l], mla_k_rope_gain[l],
            fox_q_gain[l], fox_k_gain[l], fox_b_f[l])
        qm, km, vm, qf, kf, vf, gm, gf = _proj_call(xt.reshape(b, s, d), pos, consts)
        ym = _attention(qm, km, vm, mla_shift_safe, "attn_mla")
        yf = _attention(qf, kf, vf, fox_shift_safe, "attn_fox")
        xt = _merge_call(
            xt, ym.reshape(n, V_WIDTH), yf.reshape(n, V_WIDTH), gm.reshape(n, d), gf.reshape(n, d),
            b_gate[l], _bf16(w_branch_mla[l]), _bf16(w_branch_fox[l]), _bf16(w_o[l]),
            ffn2_norm[l][None, :], _bf16(ffn2_w_gate[l]), _bf16(ffn2_w_up[l]), _bf16(ffn2_w_down[l]))
    return xt.reshape(b, s, d)
```

```python
import functools

import numpy as np
import jax
import jax.numpy as jnp
from jax import lax
from jax.experimental import pallas as pl
from jax.experimental.pallas import tpu as pltpu

D_MODEL = 1024
FFN_HIDDEN = 2816
FFN_RESIDUAL_WEIGHT = 0.5
HEADS = 8
MLA_NOPE_DIM = 64
MLA_ROPE_DIM = 32
MLA_V_DIM = 64
Q_LORA_RANK = 192
KV_LORA_RANK = 128
ROPE_THETA = 10000.0
FOX_HEAD_DIM = 64
RMS_EPS = 1e-6

LANES = 128
HEAD_PAD = LANES
HP_WIDTH = HEADS * HEAD_PAD
V_WIDTH = HEADS * MLA_V_DIM
Q_LAT_PAD = 256
ROPE_HALF = MLA_ROPE_DIM // 2
ROPE_LANE0 = MLA_NOPE_DIM
BIAS_LANE0 = FOX_HEAD_DIM
FOX_SHIFT_LANE = BIAS_LANE0 + 6
MLA_SHIFT_LANE = ROPE_LANE0 + MLA_ROPE_DIM
SHIFT_MARGIN = 40.0
SHIFT_BOUND_LIMIT = 68.0
BOUND_SLACK = 1.05

TOKEN_TILE = 512
PROJ_ROWS = 256
FFN_TILE = 1024
HIDDEN_CHUNK = 256
GATE_CHUNK = 256
ATTN_Q_TILE = 512
ATTN_KV_TILE = 512
ATTN_KV_SUB = 256
QK_LOOKAHEAD = 2
MASK_VALUE = -1e30
VMEM_LIMIT = 56 * 1024 * 1024

PROJ_QLAT = 0
PROJ_KVLAT = PROJ_QLAT + Q_LAT_PAD
PROJ_KR = PROJ_KVLAT + KV_LORA_RANK
PROJ_FQ = PROJ_KR + LANES
PROJ_FK = PROJ_FQ + V_WIDTH
PROJ_FV = PROJ_FK + V_WIDTH
PROJ_GM = PROJ_FV + V_WIDTH
PROJ_GF = PROJ_GM + D_MODEL
PROJ_WIDTH = PROJ_GF + D_MODEL

VT_ROWS = 80
VT_ONES_ROW = MLA_V_DIM
VT_WIDTH = HEADS * VT_ROWS
LOG2E = 1.4426950408889634


def _bf16(x):
    return x.astype(jnp.bfloat16)


def _dot(a, b):
    return jnp.dot(a, b, preferred_element_type=jnp.float32)


def _rms_scale(x, n):
    return lax.rsqrt(jnp.sum(x * x, axis=-1, keepdims=True) * (1.0 / n) + RMS_EPS)


def _split2(x):
    hi = _bf16(x)
    return hi, _bf16(x - hi.astype(jnp.float32))


def _split3(x):
    hi = _bf16(x)
    mid, lo = _split2(x - hi.astype(jnp.float32))
    return hi, mid, lo


def _segment_sumsq(x, seg_ref):
    seg = seg_ref[...]
    outs = []
    for c in range(x.shape[1] // 256):
        xs = x[:, c * 256:(c + 1) * 256]
        outs.append(_dot(_bf16(xs * xs), seg))
    return jnp.concatenate(outs, axis=1)


def _swiglu_residual(x, gain, wg_ref, wu_ref, wd_ref, t_ref):
    h = _bf16(x * _rms_scale(x, D_MODEL) * gain)
    for c in range(FFN_HIDDEN // HIDDEN_CHUNK):
        cols = slice(c * HIDDEN_CHUNK, (c + 1) * HIDDEN_CHUNK)
        a = _dot(h, wg_ref[:, cols])
        b = _dot(h, wu_ref[:, cols])
        t_ref[:, cols] = _bf16(a * jax.nn.sigmoid(a) * b)
    return x + FFN_RESIDUAL_WEIGHT * _dot(t_ref[...], wd_ref[...])


def _ffn_kernel(x_ref, gain_ref, wg_ref, wu_ref, wd_ref, o_ref, t_ref):
    o_ref[...] = _swiglu_residual(x_ref[...], gain_ref[...], wg_ref, wu_ref, wd_ref, t_ref)


def _const_spec(shape):
    return pl.BlockSpec(shape, lambda *_: (0,) * len(shape), pipeline_mode=pl.Buffered(1))


def _ffn_call(x, gain, wg, wu, wd):
    n = x.shape[0]
    return pl.pallas_call(
        _ffn_kernel,
        grid=(n // FFN_TILE,),
        in_specs=[
            pl.BlockSpec((FFN_TILE, D_MODEL), lambda i: (i, 0)),
            _const_spec((1, D_MODEL)),
            _const_spec((D_MODEL, FFN_HIDDEN)),
            _const_spec((D_MODEL, FFN_HIDDEN)),
            _const_spec((FFN_HIDDEN, D_MODEL)),
        ],
        out_specs=pl.BlockSpec((FFN_TILE, D_MODEL), lambda i: (i, 0)),
        out_shape=jax.ShapeDtypeStruct((n, D_MODEL), jnp.float32),
        scratch_shapes=[pltpu.VMEM((FFN_TILE, FFN_HIDDEN), jnp.bfloat16)],
        compiler_params=pltpu.CompilerParams(
            dimension_semantics=("arbitrary",), vmem_limit_bytes=VMEM_LIMIT),
        name="ffn1",
    )(x, gain, wg, wu, wd)


def _proj_kernel(x_ref, posrow_ref, gain_ref, wlat_ref, wfox_ref, wgate_ref, wqb_ref, wkvb_ref,
                 vtail_ref,
                 tri_ref, segq_ref, seg64_ref, ek_ref,
                 qlat_gain_ref, kvlat_gain_ref, q_gain_ref, q_icnt_ref, kn_gain_ref,
                 kr_gain_ref, fq_gain_ref, fk_gain_ref, bf_ref, freq_ref, qtail_ref, onek_ref,
                 qshift_ref, kone_ref,
                 qm_ref, km_ref, vm_ref, qf_ref, kf_ref, vf_ref, gm_ref, gf_ref,
                 carry_ref):
    lane = lax.broadcasted_iota(jnp.int32, (1, LANES), 1)
    rope_lanes = (lane >= ROPE_LANE0) & (lane < ROPE_LANE0 + MLA_ROPE_DIM)
    n_rows = PROJ_ROWS

    @pl.when(pl.program_id(1) == 0)
    def _():
        carry_ref[...] = jnp.zeros_like(carry_ref)

    def rows_program(r0):
        rows = slice(r0, r0 + n_rows)
        x = x_ref[0, rows, :]
        h = _bf16(x * _rms_scale(x, D_MODEL) * gain_ref[...])

        def proj(lo, width):
            for ref, base in ((wlat_ref, PROJ_QLAT), (wfox_ref, PROJ_FQ), (wgate_ref, PROJ_GM)):
                if base <= lo and lo + width <= base + ref.shape[1]:
                    return _dot(h, ref[:, lo - base:lo - base + width])
            raise ValueError("projection columns straddle two weight pieces")

        gate_chunks = [(ref, col0, c * GATE_CHUNK)
                       for ref, col0 in ((gm_ref, PROJ_GM), (gf_ref, PROJ_GF))
                       for c in range(D_MODEL // GATE_CHUNK)]

        def gates(n):
            for _ in range(min(n, len(gate_chunks))):
                ref, col0, c0 = gate_chunks.pop(0)
                ref[0, rows, c0:c0 + GATE_CHUNK] = proj(col0 + c0, GATE_CHUNK)

        gates(2)
        yield

        ang = (jnp.concatenate([freq_ref[...]] * (n_rows // LANES), axis=1)
               * posrow_ref[0, :, rows].astype(jnp.float32))
        cos16, sin16 = jnp.cos(ang), jnp.sin(ang)
        fill = lambda n, v: jnp.full((n, n_rows), v, jnp.float32)
        pad_hi = LANES - ROPE_LANE0 - MLA_ROPE_DIM
        cos_t = jnp.concatenate([fill(ROPE_LANE0, 1.0), cos16, cos16, fill(pad_hi, 1.0)], axis=0).T
        sin_hi = jnp.concatenate([fill(ROPE_LANE0 + ROPE_HALF, 0.0), sin16, fill(pad_hi, 0.0)],
                                 axis=0).T
        sin_lo = jnp.concatenate([fill(ROPE_LANE0, 0.0), -sin16, fill(ROPE_HALF + pad_hi, 0.0)],
                                 axis=0).T

        def rotary(t):
            return (t * cos_t + pltpu.roll(t, ROPE_HALF, 1) * sin_hi
                    + pltpu.roll(t, LANES - ROPE_HALF, 1) * sin_lo)

        def store_values_t(ref, v):
            tail = jnp.concatenate([vtail_ref[...]] * (n_rows // LANES), axis=1)
            blocks = []
            for pair in range(HEADS // 2):
                pair_t = v[:, pair * LANES:(pair + 1) * LANES].T
                blocks += [pair_t[:MLA_V_DIM], tail, pair_t[MLA_V_DIM:], tail]
            ref[0, 0, :, rows] = _bf16(jnp.concatenate(blocks, axis=0))

        def head_normed(t, g_ref):
            return (t * lax.rsqrt(_segment_sumsq(t, seg64_ref) * (1.0 / FOX_HEAD_DIM) + RMS_EPS)
                    * g_ref[...])

        def head_tiles(t):
            low = lane < FOX_HEAD_DIM
            tiles = []
            for pair in range(HEADS // 2):
                src = t[:, pair * LANES:(pair + 1) * LANES]
                tiles.append(jnp.where(low, src, 0.0))
                tiles.append(jnp.where(low, pltpu.roll(src, LANES - FOX_HEAD_DIM, 1), 0.0))
            return tiles

        gates(1)
        q_lat = proj(PROJ_QLAT, Q_LAT_PAD)
        yield
        q_lat = _bf16(q_lat * _rms_scale(q_lat, Q_LORA_RANK) * qlat_gain_ref[...])
        q = _dot(q_lat, wqb_ref[...])
        gates(1)
        yield
        q = q * lax.rsqrt(_segment_sumsq(q, segq_ref) * q_icnt_ref[...] + RMS_EPS) * q_gain_ref[...]
        yield
        for hd in range(HEADS):
            if hd % 2 == 0:
                gates(1)
            cols = slice(hd * HEAD_PAD, (hd + 1) * HEAD_PAD)
            q_t = (q[:, cols] + qshift_ref[...]).T
            x1 = q_t[ROPE_LANE0:ROPE_LANE0 + ROPE_HALF]
            x2 = q_t[ROPE_LANE0 + ROPE_HALF:ROPE_LANE0 + MLA_ROPE_DIM]
            qm_ref[0, cols, rows] = _bf16(jnp.concatenate(
                [q_t[:ROPE_LANE0], x1 * cos16 - x2 * sin16, x2 * cos16 + x1 * sin16,
                 q_t[ROPE_LANE0 + MLA_ROPE_DIM:]], axis=0))
            if hd % 2 == 1:
                yield

        kv_lat = proj(PROJ_KVLAT, KV_LORA_RANK)
        kv_lat = _bf16(kv_lat * _rms_scale(kv_lat, KV_LORA_RANK) * kvlat_gain_ref[...])
        gates(1)
        kv = _dot(kv_lat, wkvb_ref[...])
        yield
        k_nope = head_tiles(head_normed(kv[:, :V_WIDTH], kn_gain_ref))
        store_values_t(vm_ref, kv[:, V_WIDTH:])
        yield

        kr_grp = proj(PROJ_KR, LANES)
        kr = jnp.where(rope_lanes, kr_grp, 0.0)
        kr = rotary(kr * _rms_scale(kr, MLA_ROPE_DIM) * kr_gain_ref[...]) + kone_ref[...]
        gates(1)
        for hd in range(HEADS):
            cols = slice(hd * HEAD_PAD, (hd + 1) * HEAD_PAD)
            km_ref[0, rows, cols] = _bf16(k_nope[hd] + kr)
        gates(1)
        yield

        log_f = jnp.where(lane < HEADS, jax.nn.log_sigmoid(kr_grp + bf_ref[...]), 0.0)
        tri = tri_ref[...]
        f_hi, f_lo = _split2(log_f)
        c = _dot(tri, f_hi) + _dot(tri, f_lo) + carry_ref[0:1, :]
        carry_ref[0:1, :] = c[n_rows - 1:n_rows, :]
        yield
        c2 = c * LOG2E
        c_rep = c2 + pltpu.roll(c2, HEADS, 1) + pltpu.roll(c2, 2 * HEADS, 1)
        c_hi, c_mid, c_lo = _split3(c_rep)
        c_parts = jnp.where(lane < HEADS, c_hi, jnp.where(lane < 2 * HEADS, c_mid, c_lo))

        gates(1)
        t = proj(PROJ_FQ, V_WIDTH)
        yield
        t = head_normed(t, fq_gain_ref)
        c_t = c_parts.T
        tail = jnp.concatenate([qtail_ref[...]] * (n_rows // LANES), axis=1)
        zero_rows = jnp.zeros((HEAD_PAD - FOX_HEAD_DIM - 8, n_rows), jnp.float32)
        gates(1)
        yield
        for pair in range(HEADS // 2):
            pair_t = t[:, pair * LANES:(pair + 1) * LANES].T
            for e in range(2):
                hd = 2 * pair + e
                cols = slice(hd * HEAD_PAD, (hd + 1) * HEAD_PAD)
                pieces = [c_t[piece * HEADS + hd:piece * HEADS + hd + 1] for piece in range(3)]
                qf_ref[0, cols, rows] = _bf16(jnp.concatenate(
                    [pair_t[e * FOX_HEAD_DIM:(e + 1) * FOX_HEAD_DIM]] + pieces + [tail, zero_rows],
                    axis=0))
        yield

        gates(1)
        t = proj(PROJ_FK, V_WIDTH)
        yield
        tiles = head_tiles(head_normed(t, fk_gain_ref))
        gates(1)
        bias = _dot(c_parts, ek_ref[...]) + onek_ref[...]
        yield
        for hd in range(HEADS):
            cols = slice(hd * HEAD_PAD, (hd + 1) * HEAD_PAD)
            kf_ref[0, rows, cols] = _bf16(tiles[hd] + bias[:, cols])
        yield
        store_values_t(vf_ref, proj(PROJ_FV, V_WIDTH))
        gates(len(gate_chunks))

    programs = [rows_program(r0) for r0 in range(0, TOKEN_TILE, n_rows)]
    while programs:
        for program in list(programs):
            if next(program, "done") == "done":
                programs.remove(program)


def _proj_call(x, pos, consts):
    b, s, _ = x.shape
    bf16, f32 = jnp.bfloat16, jnp.float32
    n_tiles = s // TOKEN_TILE
    tile = lambda w: pl.BlockSpec((1, TOKEN_TILE, w), lambda bi, i: (bi, i, 0))
    tile_t = pl.BlockSpec((1, 1, VT_WIDTH, TOKEN_TILE), lambda bi, i: (bi, i, 0, 0))
    rows = lambda w, d: jax.ShapeDtypeStruct((b, s, w), d)
    rows_t = jax.ShapeDtypeStruct((b, n_tiles, VT_WIDTH, TOKEN_TILE), bf16)
    tile_q = pl.BlockSpec((1, HP_WIDTH, TOKEN_TILE), lambda bi, i: (bi, 0, i))
    rows_q = jax.ShapeDtypeStruct((b, HP_WIDTH, s), bf16)
    return pl.pallas_call(
        _proj_kernel,
        grid=(b, n_tiles),
        in_specs=[tile(D_MODEL), pl.BlockSpec((1, 1, TOKEN_TILE), lambda bi, i: (bi, 0, i))]
        + [_const_spec(c.shape) for c in consts],
        out_specs=[tile_q, tile(HP_WIDTH), tile_t, tile_q, tile(HP_WIDTH), tile_t,
                   tile(D_MODEL), tile(D_MODEL)],
        out_shape=[rows_q, rows(HP_WIDTH, bf16), rows_t, rows_q, rows(HP_WIDTH, bf16), rows_t,
                   rows(D_MODEL, f32), rows(D_MODEL, f32)],
        scratch_shapes=[pltpu.VMEM((8, LANES), jnp.float32)],
        compiler_params=pltpu.CompilerParams(
            dimension_semantics=("arbitrary", "arbitrary"), vmem_limit_bytes=VMEM_LIMIT),
        name="proj",
    )(x, pos, *consts)


def _attn_kernel(qt_ref, k_ref, vt_ref, o_ref, m_ref, acc_ref, ahead_ref, *, running_max):
    qi = pl.program_id(1)
    tq, tk, ks = ATTN_Q_TILE, ATTN_KV_TILE, ATTN_KV_SUB
    units = [(hd, sub) for sub in range(tk // ks) for hd in range(HEADS)]

    def logits(t, unit, diagonal):
        hd, sub = unit
        hcols = slice(hd * HEAD_PAD, (hd + 1) * HEAD_PAD)
        q_lo = sub * ks if diagonal else 0
        off = pl.multiple_of(t * tk + sub * ks, ks)
        return _dot(k_ref[0, pl.ds(off, ks), hcols], qt_ref[0, hcols, q_lo:])

    def step(t, diagonal, next_t):
        pending = [ahead_ref[j] for j in range(QK_LOOKAHEAD)]
        for i, (hd, sub) in enumerate(units):
            s = pending.pop(0)
            j = i + QK_LOOKAHEAD
            if j < len(units):
                pending.append(logits(t, units[j], diagonal))
            elif next_t is not None:
                ahead_ref[j - len(units)] = logits(next_t, units[j - len(units)], False)
            q_lo = sub * ks if diagonal else 0
            if diagonal:
                key = lax.broadcasted_iota(jnp.int32, s.shape, 0)
                query = lax.broadcasted_iota(jnp.int32, s.shape, 1)
                s = jnp.where(key <= query, s, MASK_VALUE)
            vt = vt_ref[0, t, hd * VT_ROWS:(hd + 1) * VT_ROWS, sub * ks:(sub + 1) * ks]
            if running_max:
                m_prev = m_ref[hd, :, q_lo:]
                m_new = jnp.maximum(m_prev, jnp.max(s, axis=0, keepdims=True))
                alpha = jnp.exp2(m_prev - m_new)
                p = _bf16(jnp.exp2(s - m_new))
                acc_ref[hd, :, q_lo:] = alpha * acc_ref[hd, :, q_lo:] + _dot(vt, p)
                m_ref[hd, :, q_lo:] = m_new
            else:
                acc_ref[hd, :, q_lo:] += _dot(vt, _bf16(jnp.exp2(s)))

    if running_max:
        m_ref[...] = jnp.full(m_ref.shape, MASK_VALUE, jnp.float32)
    acc_ref[...] = jnp.zeros(acc_ref.shape, jnp.float32)
    for j in range(QK_LOOKAHEAD):
        ahead_ref[j] = logits(0, units[j], False)

    def body(t, carry):
        step(t, False, t + 1)
        return carry

    lax.fori_loop(0, qi, body, 0)
    step(qi, True, None)

    for pair in range(HEADS // 2):
        y_t = []
        for hd in (2 * pair, 2 * pair + 1):
            acc = acc_ref[hd]
            y_t.append(acc[:MLA_V_DIM] / acc[VT_ONES_ROW:VT_ONES_ROW + 1])
        y = jnp.concatenate(y_t, axis=0).T
        o_ref[0, :, pair * LANES:(pair + 1) * LANES] = y.astype(o_ref.dtype)


def _attn_call(qt, k, vt, *, running_max, name):
    b, s, _ = k.shape
    assert ATTN_Q_TILE == ATTN_KV_TILE == TOKEN_TILE
    return pl.pallas_call(
        functools.partial(_attn_kernel, running_max=running_max),
        grid=(b, s // ATTN_Q_TILE),
        in_specs=[
            pl.BlockSpec((1, HP_WIDTH, ATTN_Q_TILE), lambda bi, i: (bi, 0, i)),
            pl.BlockSpec((1, s, HP_WIDTH), lambda bi, i: (bi, 0, 0)),
            pl.BlockSpec((1,) + vt.shape[1:], lambda bi, i: (bi, 0, 0, 0)),
        ],
        out_specs=pl.BlockSpec((1, ATTN_Q_TILE, V_WIDTH), lambda bi, i: (bi, i, 0)),
        out_shape=jax.ShapeDtypeStruct((b, s, V_WIDTH), jnp.bfloat16),
        scratch_shapes=[
            pltpu.VMEM((HEADS, 1, ATTN_Q_TILE), jnp.float32),
            pltpu.VMEM((HEADS, VT_ROWS, ATTN_Q_TILE), jnp.float32),
            pltpu.VMEM((QK_LOOKAHEAD, ATTN_KV_SUB, ATTN_Q_TILE), jnp.float32),
        ],
        compiler_params=pltpu.CompilerParams(
            dimension_semantics=("arbitrary", "arbitrary"), vmem_limit_bytes=VMEM_LIMIT),
        name=name,
    )(qt, k, vt)


def _attention(q, k, vt, shift_is_safe, name):
    return lax.cond(
        shift_is_safe,
        functools.partial(_attn_call, running_max=False, name=name + "_shifted"),
        functools.partial(_attn_call, running_max=True, name=name),
        q, k, vt)


def _merge_kernel(x_ref, ym_ref, yf_ref, gm_ref, gf_ref, bg_ref, wa_ref, wb_ref, wo_ref,
                  gain_ref, wg_ref, wu_ref, wd_ref, o_ref, t_ref):
    mixed = (jax.nn.sigmoid(gm_ref[...] + bg_ref[0:1, :]) * _dot(ym_ref[...], wa_ref[...])
             + jax.nn.sigmoid(gf_ref[...] + bg_ref[1:2, :]) * _dot(yf_ref[...], wb_ref[...]))
    x = x_ref[...] + _dot(_bf16(mixed), wo_ref[...])
    o_ref[...] = _swiglu_residual(x, gain_ref[...], wg_ref, wu_ref, wd_ref, t_ref)


def _merge_call(x, ym, yf, gm, gf, bg, wa, wb, wo, gain, wg, wu, wd):
    n = x.shape[0]
    tile = lambda w: pl.BlockSpec((TOKEN_TILE, w), lambda i: (i, 0))
    consts = (bg, wa, wb, wo, gain, wg, wu, wd)
    return pl.pallas_call(
        _merge_kernel,
        grid=(n // TOKEN_TILE,),
        in_specs=[tile(D_MODEL), tile(V_WIDTH), tile(V_WIDTH), tile(D_MODEL), tile(D_MODEL)]
        + [_const_spec(c.shape) for c in consts],
        out_specs=tile(D_MODEL),
        out_shape=jax.ShapeDtypeStruct((n, D_MODEL), jnp.float32),
        scratch_shapes=[pltpu.VMEM((TOKEN_TILE, FFN_HIDDEN), jnp.bfloat16)],
        compiler_params=pltpu.CompilerParams(
            dimension_semantics=("arbitrary",), vmem_limit_bytes=VMEM_LIMIT),
        name="merge_ffn2",
    )(x, ym, yf, gm, gf, *consts)


def _pad_heads(w, width):
    rows = w.shape[0]
    w = w.reshape(rows, HEADS, -1)
    return jnp.pad(w, ((0, 0), (0, 0), (0, width - w.shape[2]))).reshape(rows, HEADS * width)


def _head_lanes(vec, lane0=0):
    tile = jnp.zeros((HEAD_PAD,), jnp.float32).at[lane0:lane0 + vec.shape[0]].set(vec)
    return jnp.tile(tile, HEADS)[None, :]


def _proj_constants(mix_norm, w_in, q_lat_norm, w_qb, kv_lat_norm, w_kvb, q_nope_gain, q_rope_gain,
                    k_nope_gain, k_rope_gain, fox_q_gain, fox_k_gain, fox_b_f):
    f32 = jnp.float32
    o_qlat, o_kvlat = 0, Q_LORA_RANK
    o_kr = o_kvlat + KV_LORA_RANK
    o_fq = o_kr + MLA_ROPE_DIM
    o_fk = o_fq + V_WIDTH
    o_fv = o_fk + V_WIDTH
    o_fl = o_fv + V_WIDTH
    o_gm = o_fl + HEADS
    o_gf = o_gm + D_MODEL

    def zero_cols(n):
        return jnp.zeros((D_MODEL, n), w_in.dtype)

    wlat = _bf16(jnp.concatenate([
        w_in[:, o_qlat:o_kvlat], zero_cols(Q_LAT_PAD - Q_LORA_RANK),
        w_in[:, o_kvlat:o_kr],
        w_in[:, o_fl:o_gm], zero_cols(ROPE_LANE0 - HEADS),
        w_in[:, o_kr:o_fq], zero_cols(LANES - ROPE_LANE0 - MLA_ROPE_DIM),
    ], axis=1))
    wfox = _bf16(w_in[:, o_fq:o_fl])
    wgate = _bf16(w_in[:, o_gm:])
    assert wlat.shape[1] == PROJ_FQ and wfox.shape[1] == PROJ_GM - PROJ_FQ
    assert wgate.shape[1] == PROJ_WIDTH - PROJ_GM

    qk_dim = MLA_NOPE_DIM + MLA_ROPE_DIM
    wqb = _bf16(jnp.pad(_pad_heads(w_qb, HEAD_PAD), ((0, Q_LAT_PAD - Q_LORA_RANK), (0, 0))))
    w_kvb = w_kvb.reshape(KV_LORA_RANK, HEADS, MLA_NOPE_DIM + MLA_V_DIM)
    wkvb = _bf16(jnp.concatenate([
        w_kvb[:, :, :MLA_NOPE_DIM].reshape(KV_LORA_RANK, V_WIDTH),
        w_kvb[:, :, MLA_NOPE_DIM:].reshape(KV_LORA_RANK, V_WIDTH)], axis=1))
    vtail = np.zeros((VT_ROWS - MLA_V_DIM, LANES), np.float32)
    vtail[VT_ONES_ROW - MLA_V_DIM] = 1.0
    vtail = jnp.asarray(vtail)

    r = np.arange(PROJ_ROWS)
    tri = jnp.asarray(r[:, None] >= r[None, :], jnp.bfloat16)
    l256 = np.arange(256)
    same_head = (l256[:, None] // HEAD_PAD) == (l256[None, :] // HEAD_PAD)
    in_nope = (l256 % HEAD_PAD) < MLA_NOPE_DIM
    in_rope = ((l256 % HEAD_PAD) >= ROPE_LANE0) & ((l256 % HEAD_PAD) < ROPE_LANE0 + MLA_ROPE_DIM)
    seg_q = same_head & ((in_nope[:, None] & in_nope[None, :]) | (in_rope[:, None] & in_rope[None, :]))
    seg_q = jnp.asarray(seg_q, jnp.bfloat16)
    seg_64 = jnp.asarray((l256[:, None] // FOX_HEAD_DIM) == (l256[None, :] // FOX_HEAD_DIM),
                         jnp.bfloat16)
    ek = np.zeros((LANES, HP_WIDTH), np.float32)
    onek = np.zeros((1, HP_WIDTH), np.float32)
    for hd in range(HEADS):
        for piece in range(3):
            ek[piece * HEADS + hd, hd * HEAD_PAD + BIAS_LANE0 + 3 + piece] = -1.0
            onek[0, hd * HEAD_PAD + BIAS_LANE0 + piece] = 1.0
    ek = jnp.asarray(ek, jnp.bfloat16)

    mla_scale = float(qk_dim) ** -0.5 * LOG2E
    fox_scale = float(FOX_HEAD_DIM) ** -0.5 * LOG2E
    q_gain = _head_lanes(jnp.concatenate([q_nope_gain, q_rope_gain]) * mla_scale)
    icnt = np.ones((HEAD_PAD,), np.float32)
    icnt[:MLA_NOPE_DIM] = 1.0 / MLA_NOPE_DIM
    icnt[ROPE_LANE0:ROPE_LANE0 + MLA_ROPE_DIM] = 1.0 / MLA_ROPE_DIM
    q_icnt = jnp.asarray(np.tile(icnt, HEADS)[None, :])
    kn_gain = jnp.tile(k_nope_gain, HEADS)[None, :]
    kr_gain = jnp.zeros((1, LANES), f32).at[0, ROPE_LANE0:ROPE_LANE0 + MLA_ROPE_DIM].set(k_rope_gain)
    fq_gain = jnp.tile(fox_q_gain * fox_scale, HEADS)[None, :]
    fk_gain = jnp.tile(fox_k_gain, HEADS)[None, :]
    bf = jnp.zeros((1, LANES), f32).at[0, :HEADS].set(fox_b_f)
    inv_freq = ROPE_THETA ** (-jnp.arange(ROPE_HALF, dtype=f32) / ROPE_HALF)
    freq = jnp.broadcast_to(inv_freq[:, None], (ROPE_HALF, LANES))
    qlat_gain = jnp.pad(q_lat_norm, (0, Q_LAT_PAD - Q_LORA_RANK))[None, :]

    amax = lambda g: jnp.max(jnp.abs(g))
    bound_fox = (FOX_HEAD_DIM * amax(fox_q_gain) * amax(fox_k_gain) * fox_scale) * BOUND_SLACK
    q_sq = MLA_NOPE_DIM * amax(q_nope_gain) ** 2 + MLA_ROPE_DIM * amax(q_rope_gain) ** 2
    k_sq = MLA_NOPE_DIM * amax(k_nope_gain) ** 2 + MLA_ROPE_DIM * amax(k_rope_gain) ** 2
    bound_mla = jnp.sqrt(q_sq * k_sq) * mla_scale * BOUND_SLACK
    shift_lane = np.zeros((1, LANES), np.float32)
    shift_lane[0, MLA_SHIFT_LANE] = 1.0
    kone = jnp.asarray(shift_lane)
    qshift = kone * -(bound_mla - SHIFT_MARGIN)
    fox_shift_lanes = np.zeros((1, HP_WIDTH), np.float32)
    fox_shift_lanes[0, FOX_SHIFT_LANE::HEAD_PAD] = 1.0
    onek = jnp.asarray(onek + fox_shift_lanes)
    qtail = jnp.concatenate([jnp.ones((3, LANES), f32),
                             jnp.full((1, LANES), -(bound_fox - SHIFT_MARGIN), f32),
                             jnp.zeros((1, LANES), f32)], axis=0)

    consts = (mix_norm[None, :], wlat, wfox, wgate, wqb, wkvb, vtail, tri, seg_q, seg_64, ek,
              qlat_gain, kv_lat_norm[None, :], q_gain, q_icnt, kn_gain, kr_gain, fq_gain, fk_gain,
              bf, freq, qtail, onek, qshift, kone)
    return consts, bound_mla <= SHIFT_BOUND_LIMIT, bound_fox <= SHIFT_BOUND_LIMIT


def kernel(x, positions, ffn1_norm, ffn1_w_gate, ffn1_w_up, ffn1_w_down, mix_norm, w_in, mla_q_lat_norm, mla_w_qb, mla_kv_lat_norm, mla_w_kvb, mla_q_nope_gain, mla_q_rope_gain, mla_k_nope_gain, mla_k_rope_gain, fox_q_gain, fox_k_gain, fox_b_f, w_branch_mla, w_branch_fox, b_gate, w_o, ffn2_norm, ffn2_w_gate, ffn2_w_up, ffn2_w_down):
    b, s, d = x.shape
    n = b * s
    pos = positions.reshape(b, 1, s)
    xt = x.reshape(n, d)
    for l in range(ffn1_norm.shape[0]):
        xt = _ffn_call(xt, ffn1_norm[l][None, :], _bf16(ffn1_w_gate[l]), _bf16(ffn1_w_up[l]),
                       _bf16(ffn1_w_down[l]))
        consts, mla_shift_safe, fox_shift_safe = _proj_constants(
            mix_norm[l], w_in[l], mla_q_lat_norm[l], mla_w_qb[l], mla_kv_lat_norm[l], mla_w_kvb[l],
            mla_q_nope_gain[l], mla_q_rope_gain[l], mla_k_nope_gain[l], mla_k_rope_gain[l],
            fox_q_gain[l], fox_k_gain[l], fox_b_f[l])
        qm, km, vm, qf, kf, vf, gm, gf = _proj_call(xt.reshape(b, s, d), pos, consts)
        ym = _attention(qm, km, vm, mla_shift_safe, "attn_mla")
        yf = _attention(qf, kf, vf, fox_shift_safe, "attn_fox")
        xt = _merge_call(
            xt, ym.reshape(n, V_WIDTH), yf.reshape(n, V_WIDTH), gm.reshape(n, d), gf.reshape(n, d),
            b_gate[l], _bf16(w_branch_mla[l]), _bf16(w_branch_fox[l]), _bf16(w_o[l]),
            ffn2_norm[l][None, :], _bf16(ffn2_w_gate[l]), _bf16(ffn2_w_up[l]), _bf16(ffn2_w_down[l]))
    return xt.reshape(b, s, d)
```

```python
import functools

import numpy as np
import jax
import jax.numpy as jnp
from jax import lax
from jax.experimental import pallas as pl
from jax.experimental.pallas import tpu as pltpu

D_MODEL = 1024
FFN_HIDDEN = 2816
FFN_RESIDUAL_WEIGHT = 0.5
HEADS = 8
MLA_NOPE_DIM = 64
MLA_ROPE_DIM = 32
MLA_V_DIM = 64
Q_LORA_RANK = 192
KV_LORA_RANK = 128
ROPE_THETA = 10000.0
FOX_HEAD_DIM = 64
RMS_EPS = 1e-6

LANES = 128
HEAD_PAD = LANES
HP_WIDTH = HEADS * HEAD_PAD
V_WIDTH = HEADS * MLA_V_DIM
Q_LAT_PAD = 256
ROPE_HALF = MLA_ROPE_DIM // 2
ROPE_LANE0 = MLA_NOPE_DIM
BIAS_LANE0 = FOX_HEAD_DIM
FOX_SHIFT_LANE = BIAS_LANE0 + 6
MLA_SHIFT_LANE = ROPE_LANE0 + MLA_ROPE_DIM
SHIFT_MARGIN = 40.0
SHIFT_BOUND_LIMIT = 68.0
BOUND_SLACK = 1.05

TOKEN_TILE = 512
PROJ_ROWS = 256
FFN_TILE = 1024
HIDDEN_CHUNK = 256
GATE_CHUNK = 256
ATTN_Q_TILE = 512
ATTN_KV_TILE = 512
ATTN_KV_SUB = 256
QK_LOOKAHEAD = 2
MASK_VALUE = -1e30
VMEM_LIMIT = 56 * 1024 * 1024

PROJ_QLAT = 0
PROJ_KVLAT = PROJ_QLAT + Q_LAT_PAD
PROJ_KR = PROJ_KVLAT + KV_LORA_RANK
PROJ_FQ = PROJ_KR + LANES
PROJ_FK = PROJ_FQ + V_WIDTH
PROJ_FV = PROJ_FK + V_WIDTH
PROJ_GM = PROJ_FV + V_WIDTH
PROJ_GF = PROJ_GM + D_MODEL
PROJ_WIDTH = PROJ_GF + D_MODEL

SRC_QLAT = 0
SRC_KVLAT = SRC_QLAT + Q_LORA_RANK
SRC_KR = SRC_KVLAT + KV_LORA_RANK
SRC_FQ = SRC_KR + MLA_ROPE_DIM
SRC_FLOGIT = SRC_FQ + 3 * V_WIDTH
SRC_GM = SRC_FLOGIT + HEADS
SRC_WIDTH = SRC_GM + 2 * D_MODEL

VT_ROWS = 80
VT_ONES_ROW = MLA_V_DIM
VT_WIDTH = HEADS * VT_ROWS
LOG2E = 1.4426950408889634


def _bf16(x):
    return x.astype(jnp.bfloat16)


def _dot(a, b):
    return jnp.dot(a, b, preferred_element_type=jnp.float32)


def _rms_scale(x, n):
    return lax.rsqrt(jnp.sum(x * x, axis=-1, keepdims=True) * (1.0 / n) + RMS_EPS)


def _split2(x):
    hi = _bf16(x)
    return hi, _bf16(x - hi.astype(jnp.float32))


def _split3(x):
    hi = _bf16(x)
    mid, lo = _split2(x - hi.astype(jnp.float32))
    return hi, mid, lo


def _segment_sumsq(x, seg_ref):
    seg = seg_ref[...]
    outs = []
    for c in range(x.shape[1] // 256):
        xs = x[:, c * 256:(c + 1) * 256]
        outs.append(_dot(_bf16(xs * xs), seg))
    return jnp.concatenate(outs, axis=1)


def _swiglu_residual(x, gain, wg_ref, wu_ref, wd_ref, t_ref):
    h = _bf16(x * _rms_scale(x, D_MODEL) * gain)
    for c in range(FFN_HIDDEN // HIDDEN_CHUNK):
        cols = slice(c * HIDDEN_CHUNK, (c + 1) * HIDDEN_CHUNK)
        a = _dot(h, wg_ref[:, cols])
        b = _dot(h, wu_ref[:, cols])
        t_ref[:, cols] = _bf16(a * jax.nn.sigmoid(a) * b)
    return x + FFN_RESIDUAL_WEIGHT * _dot(t_ref[...], wd_ref[...])


def _ffn_kernel(x_ref, gain_ref, wg_ref, wu_ref, wd_ref, win_ref, o_ref, wfox_ref, wgate_ref, t_ref):
    o_ref[...] = _swiglu_residual(x_ref[...], gain_ref[...], wg_ref, wu_ref, wd_ref, t_ref)
    wfox_ref[...] = _bf16(win_ref[:, SRC_FQ:SRC_FLOGIT])
    wgate_ref[...] = _bf16(win_ref[:, SRC_GM:])


def _const_spec(shape):
    return pl.BlockSpec(shape, lambda *_: (0,) * len(shape), pipeline_mode=pl.Buffered(1))


def _ffn_call(x, gain, wg, wu, wd, w_in):
    n = x.shape[0]
    steps = n // FFN_TILE
    slab = D_MODEL // steps
    assert slab * steps == D_MODEL and slab % 16 == 0
    fox_w, gate_w = SRC_FLOGIT - SRC_FQ, SRC_WIDTH - SRC_GM
    rows = lambda w: pl.BlockSpec((slab, w), lambda i: (i, 0))
    return pl.pallas_call(
        _ffn_kernel,
        grid=(steps,),
        in_specs=[
            pl.BlockSpec((FFN_TILE, D_MODEL), lambda i: (i, 0)),
            _const_spec((1, D_MODEL)),
            _const_spec((D_MODEL, FFN_HIDDEN)),
            _const_spec((D_MODEL, FFN_HIDDEN)),
            _const_spec((FFN_HIDDEN, D_MODEL)),
            rows(SRC_WIDTH),
        ],
        out_specs=[pl.BlockSpec((FFN_TILE, D_MODEL), lambda i: (i, 0)), rows(fox_w), rows(gate_w)],
        out_shape=[jax.ShapeDtypeStruct((n, D_MODEL), jnp.float32),
                   jax.ShapeDtypeStruct((D_MODEL, fox_w), jnp.bfloat16),
                   jax.ShapeDtypeStruct((D_MODEL, gate_w), jnp.bfloat16)],
        scratch_shapes=[pltpu.VMEM((FFN_TILE, FFN_HIDDEN), jnp.bfloat16)],
        compiler_params=pltpu.CompilerParams(
            dimension_semantics=("arbitrary",), vmem_limit_bytes=VMEM_LIMIT),
        name="ffn1",
    )(x, gain, wg, wu, wd, w_in)


def _proj_kernel(x_ref, posrow_ref, gain_ref, wlat_ref, wfox_ref, wgate_ref, wqb_ref, wkvb_ref,
                 vtail_ref,
                 tri_ref, segq_ref, seg64_ref, ek_ref,
                 qlat_gain_ref, kvlat_gain_ref, q_gain_ref, q_icnt_ref, kn_gain_ref,
                 kr_gain_ref, fq_gain_ref, fk_gain_ref, bf_ref, freq_ref, qtail_ref, onek_ref,
                 qshift_ref, kone_ref,
                 qm_ref, km_ref, vm_ref, qf_ref, kf_ref, vf_ref, gm_ref, gf_ref,
                 carry_ref):
    lane = lax.broadcasted_iota(jnp.int32, (1, LANES), 1)
    rope_lanes = (lane >= ROPE_LANE0) & (lane < ROPE_LANE0 + MLA_ROPE_DIM)
    n_rows = PROJ_ROWS

    @pl.when(pl.program_id(1) == 0)
    def _():
        carry_ref[...] = jnp.zeros_like(carry_ref)

    def rows_program(r0):
        rows = slice(r0, r0 + n_rows)
        x = x_ref[0, rows, :]
        h = _bf16(x * _rms_scale(x, D_MODEL) * gain_ref[...])

        def proj(lo, width):
            for ref, base in ((wlat_ref, PROJ_QLAT), (wfox_ref, PROJ_FQ), (wgate_ref, PROJ_GM)):
                if base <= lo and lo + width <= base + ref.shape[1]:
                    return _dot(h, ref[:, lo - base:lo - base + width])
            raise ValueError("projection columns straddle two weight pieces")

        gate_chunks = [(ref, col0, c * GATE_CHUNK)
                       for ref, col0 in ((gm_ref, PROJ_GM), (gf_ref, PROJ_GF))
                       for c in range(D_MODEL // GATE_CHUNK)]

        def gates(n):
            for _ in range(min(n, len(gate_chunks))):
                ref, col0, c0 = gate_chunks.pop(0)
                ref[0, rows, c0:c0 + GATE_CHUNK] = proj(col0 + c0, GATE_CHUNK)

        gates(2)
        yield

        ang = (jnp.concatenate([freq_ref[...]] * (n_rows // LANES), axis=1)
               * posrow_ref[0, :, rows].astype(jnp.float32))
        cos16, sin16 = jnp.cos(ang), jnp.sin(ang)
        fill = lambda n, v: jnp.full((n, n_rows), v, jnp.float32)
        pad_hi = LANES - ROPE_LANE0 - MLA_ROPE_DIM
        cos_t = jnp.concatenate([fill(ROPE_LANE0, 1.0), cos16, cos16, fill(pad_hi, 1.0)], axis=0).T
        sin_hi = jnp.concatenate([fill(ROPE_LANE0 + ROPE_HALF, 0.0), sin16, fill(pad_hi, 0.0)],
                                 axis=0).T
        sin_lo = jnp.concatenate([fill(ROPE_LANE0, 0.0), -sin16, fill(ROPE_HALF + pad_hi, 0.0)],
                                 axis=0).T

        def rotary(t):
            return (t * cos_t + pltpu.roll(t, ROPE_HALF, 1) * sin_hi
                    + pltpu.roll(t, LANES - ROPE_HALF, 1) * sin_lo)

        def store_values_t(ref, v):
            tail = jnp.concatenate([vtail_ref[...]] * (n_rows // LANES), axis=1)
            blocks = []
            for pair in range(HEADS // 2):
                pair_t = v[:, pair * LANES:(pair + 1) * LANES].T
                blocks += [pair_t[:MLA_V_DIM], tail, pair_t[MLA_V_DIM:], tail]
            ref[0, 0, :, rows] = _bf16(jnp.concatenate(blocks, axis=0))

        def head_normed(t, g_ref):
            return (t * lax.rsqrt(_segment_sumsq(t, seg64_ref) * (1.0 / FOX_HEAD_DIM) + RMS_EPS)
                    * g_ref[...])

        def head_tiles(t):
            low = lane < FOX_HEAD_DIM
            tiles = []
            for pair in range(HEADS // 2):
                src = t[:, pair * LANES:(pair + 1) * LANES]
                tiles.append(jnp.where(low, src, 0.0))
                tiles.append(jnp.where(low, pltpu.roll(src, LANES - FOX_HEAD_DIM, 1), 0.0))
            return tiles

        gates(1)
        q_lat = proj(PROJ_QLAT, Q_LAT_PAD)
        yield
        q_lat = _bf16(q_lat * _rms_scale(q_lat, Q_LORA_RANK) * qlat_gain_ref[...])
        q = _dot(q_lat, wqb_ref[...])
        gates(1)
        yield
        q = q * lax.rsqrt(_segment_sumsq(q, segq_ref) * q_icnt_ref[...] + RMS_EPS) * q_gain_ref[...]
        yield
        for hd in range(HEADS):
            if hd % 2 == 0:
                gates(1)
            cols = slice(hd * HEAD_PAD, (hd + 1) * HEAD_PAD)
            q_t = (q[:, cols] + qshift_ref[...]).T
            x1 = q_t[ROPE_LANE0:ROPE_LANE0 + ROPE_HALF]
            x2 = q_t[ROPE_LANE0 + ROPE_HALF:ROPE_LANE0 + MLA_ROPE_DIM]
            qm_ref[0, cols, rows] = _bf16(jnp.concatenate(
                [q_t[:ROPE_LANE0], x1 * cos16 - x2 * sin16, x2 * cos16 + x1 * sin16,
                 q_t[ROPE_LANE0 + MLA_ROPE_DIM:]], axis=0))
            if hd % 2 == 1:
                yield

        kv_lat = proj(PROJ_KVLAT, KV_LORA_RANK)
        kv_lat = _bf16(kv_lat * _rms_scale(kv_lat, KV_LORA_RANK) * kvlat_gain_ref[...])
        gates(1)
        kv = _dot(kv_lat, wkvb_ref[...])
        yield
        k_nope = head_tiles(head_normed(kv[:, :V_WIDTH], kn_gain_ref))
        store_values_t(vm_ref, kv[:, V_WIDTH:])
        yield

        kr_grp = proj(PROJ_KR, LANES)
        kr = jnp.where(rope_lanes, kr_grp, 0.0)
        kr = rotary(kr * _rms_scale(kr, MLA_ROPE_DIM) * kr_gain_ref[...]) + kone_ref[...]
        gates(1)
        for hd in range(HEADS):
            cols = slice(hd * HEAD_PAD, (hd + 1) * HEAD_PAD)
            km_ref[0, rows, cols] = _bf16(k_nope[hd] + kr)
        gates(1)
        yield

        log_f = jnp.where(lane < HEADS, jax.nn.log_sigmoid(kr_grp + bf_ref[...]), 0.0)
        tri = tri_ref[...]
        f_hi, f_lo = _split2(log_f)
        c = _dot(tri, f_hi) + _dot(tri, f_lo) + carry_ref[0:1, :]
        carry_ref[0:1, :] = c[n_rows - 1:n_rows, :]
        yield
        c2 = c * LOG2E
        c_rep = c2 + pltpu.roll(c2, HEADS, 1) + pltpu.roll(c2, 2 * HEADS, 1)
        c_hi, c_mid, c_lo = _split3(c_rep)
        c_parts = jnp.where(lane < HEADS, c_hi, jnp.where(lane < 2 * HEADS, c_mid, c_lo))

        gates(1)
        t = proj(PROJ_FQ, V_WIDTH)
        yield
        t = head_normed(t, fq_gain_ref)
        c_t = c_parts.T
        tail = jnp.concatenate([qtail_ref[...]] * (n_rows // LANES), axis=1)
        zero_rows = jnp.zeros((HEAD_PAD - FOX_HEAD_DIM - 8, n_rows), jnp.float32)
        gates(1)
        yield
        for pair in range(HEADS // 2):
            pair_t = t[:, pair * LANES:(pair + 1) * LANES].T
            for e in range(2):
                hd = 2 * pair + e
                cols = slice(hd * HEAD_PAD, (hd + 1) * HEAD_PAD)
                pieces = [c_t[piece * HEADS + hd:piece * HEADS + hd + 1] for piece in range(3)]
                qf_ref[0, cols, rows] = _bf16(jnp.concatenate(
                    [pair_t[e * FOX_HEAD_DIM:(e + 1) * FOX_HEAD_DIM]] + pieces + [tail, zero_rows],
                    axis=0))
        yield

        gates(1)
        t = proj(PROJ_FK, V_WIDTH)
        yield
        tiles = head_tiles(head_normed(t, fk_gain_ref))
        gates(1)
        bias = _dot(c_parts, ek_ref[...]) + onek_ref[...]
        yield
        for hd in range(HEADS):
            cols = slice(hd * HEAD_PAD, (hd + 1) * HEAD_PAD)
            kf_ref[0, rows, cols] = _bf16(tiles[hd] + bias[:, cols])
        yield
        store_values_t(vf_ref, proj(PROJ_FV, V_WIDTH))
        gates(len(gate_chunks))

    programs = [rows_program(r0) for r0 in range(0, TOKEN_TILE, n_rows)]
    while programs:
        for program in list(programs):
            if next(program, "done") == "done":
                programs.remove(program)


def _proj_call(x, pos, consts):
    b, s, _ = x.shape
    bf16, f32 = jnp.bfloat16, jnp.float32
    n_tiles = s // TOKEN_TILE
    tile = lambda w: pl.BlockSpec((1, TOKEN_TILE, w), lambda bi, i: (bi, i, 0))
    tile_t = pl.BlockSpec((1, 1, VT_WIDTH, TOKEN_TILE), lambda bi, i: (bi, i, 0, 0))
    rows = lambda w, d: jax.ShapeDtypeStruct((b, s, w), d)
    rows_t = jax.ShapeDtypeStruct((b, n_tiles, VT_WIDTH, TOKEN_TILE), bf16)
    tile_q = pl.BlockSpec((1, HP_WIDTH, TOKEN_TILE), lambda bi, i: (bi, 0, i))
    rows_q = jax.ShapeDtypeStruct((b, HP_WIDTH, s), bf16)
    return pl.pallas_call(
        _proj_kernel,
        grid=(b, n_tiles),
        in_specs=[tile(D_MODEL), pl.BlockSpec((1, 1, TOKEN_TILE), lambda bi, i: (bi, 0, i))]
        + [_const_spec(c.shape) for c in consts],
        out_specs=[tile_q, tile(HP_WIDTH), tile_t, tile_q, tile(HP_WIDTH), tile_t,
                   tile(D_MODEL), tile(D_MODEL)],
        out_shape=[rows_q, rows(HP_WIDTH, bf16), rows_t, rows_q, rows(HP_WIDTH, bf16), rows_t,
                   rows(D_MODEL, f32), rows(D_MODEL, f32)],
        scratch_shapes=[pltpu.VMEM((8, LANES), jnp.float32)],
        compiler_params=pltpu.CompilerParams(
            dimension_semantics=("arbitrary", "arbitrary"), vmem_limit_bytes=VMEM_LIMIT),
        name="proj",
    )(x, pos, *consts)


def _attn_kernel(qt_ref, k_ref, vt_ref, o_ref, m_ref, acc_ref, ahead_ref, *, running_max):
    qi = pl.program_id(1)
    tq, tk, ks = ATTN_Q_TILE, ATTN_KV_TILE, ATTN_KV_SUB
    units = [(hd, sub) for sub in range(tk // ks) for hd in range(HEADS)]

    def logits(t, unit, diagonal):
        hd, sub = unit
        hcols = slice(hd * HEAD_PAD, (hd + 1) * HEAD_PAD)
        q_lo = sub * ks if diagonal else 0
        off = pl.multiple_of(t * tk + sub * ks, ks)
        return _dot(k_ref[0, pl.ds(off, ks), hcols], qt_ref[0, hcols, q_lo:])

    def step(t, diagonal, next_t):
        pending = [ahead_ref[j] for j in range(QK_LOOKAHEAD)]
        for i, (hd, sub) in enumerate(units):
            s = pending.pop(0)
            j = i + QK_LOOKAHEAD
            if j < len(units):
                pending.append(logits(t, units[j], diagonal))
            elif next_t is not None:
                ahead_ref[j - len(units)] = logits(next_t, units[j - len(units)], False)
            q_lo = sub * ks if diagonal else 0
            if diagonal:
                key = lax.broadcasted_iota(jnp.int32, s.shape, 0)
                query = lax.broadcasted_iota(jnp.int32, s.shape, 1)
                s = jnp.where(key <= query, s, MASK_VALUE)
            vt = vt_ref[0, t, hd * VT_ROWS:(hd + 1) * VT_ROWS, sub * ks:(sub + 1) * ks]
            if running_max:
                m_prev = m_ref[hd, :, q_lo:]
                m_new = jnp.maximum(m_prev, jnp.max(s, axis=0, keepdims=True))
                alpha = jnp.exp2(m_prev - m_new)
                p = _bf16(jnp.exp2(s - m_new))
                acc_ref[hd, :, q_lo:] = alpha * acc_ref[hd, :, q_lo:] + _dot(vt, p)
                m_ref[hd, :, q_lo:] = m_new
            else:
                acc_ref[hd, :, q_lo:] += _dot(vt, _bf16(jnp.exp2(s)))

    if running_max:
        m_ref[...] = jnp.full(m_ref.shape, MASK_VALUE, jnp.float32)
    acc_ref[...] = jnp.zeros(acc_ref.shape, jnp.float32)
    for j in range(QK_LOOKAHEAD):
        ahead_ref[j] = logits(0, units[j], False)

    def body(t, carry):
        step(t, False, t + 1)
        return carry

    lax.fori_loop(0, qi, body, 0)
    step(qi, True, None)

    for pair in range(HEADS // 2):
        y_t = []
        for hd in (2 * pair, 2 * pair + 1):
            acc = acc_ref[hd]
            y_t.append(acc[:MLA_V_DIM] / acc[VT_ONES_ROW:VT_ONES_ROW + 1])
        y = jnp.concatenate(y_t, axis=0).T
        o_ref[0, :, pair * LANES:(pair + 1) * LANES] = y.astype(o_ref.dtype)


def _attn_call(qt, k, vt, *, running_max, name):
    b, s, _ = k.shape
    assert ATTN_Q_TILE == ATTN_KV_TILE == TOKEN_TILE
    return pl.pallas_call(
        functools.partial(_attn_kernel, running_max=running_max),
        grid=(b, s // ATTN_Q_TILE),
        in_specs=[
            pl.BlockSpec((1, HP_WIDTH, ATTN_Q_TILE), lambda bi, i: (bi, 0, i)),
            pl.BlockSpec((1, s, HP_WIDTH), lambda bi, i: (bi, 0, 0)),
            pl.BlockSpec((1,) + vt.shape[1:], lambda bi, i: (bi, 0, 0, 0)),
        ],
        out_specs=pl.BlockSpec((1, ATTN_Q_TILE, V_WIDTH), lambda bi, i: (bi, i, 0)),
        out_shape=jax.ShapeDtypeStruct((b, s, V_WIDTH), jnp.bfloat16),
        scratch_shapes=[
            pltpu.VMEM((HEADS, 1, ATTN_Q_TILE), jnp.float32),
            pltpu.VMEM((HEADS, VT_ROWS, ATTN_Q_TILE), jnp.float32),
            pltpu.VMEM((QK_LOOKAHEAD, ATTN_KV_SUB, ATTN_Q_TILE), jnp.float32),
        ],
        compiler_params=pltpu.CompilerParams(
            dimension_semantics=("arbitrary", "arbitrary"), vmem_limit_bytes=VMEM_LIMIT),
        name=name,
    )(qt, k, vt)


def _attention(q, k, vt, shift_is_safe, name):
    return lax.cond(
        shift_is_safe,
        functools.partial(_attn_call, running_max=False, name=name + "_shifted"),
        functools.partial(_attn_call, running_max=True, name=name),
        q, k, vt)


def _merge_kernel(x_ref, ym_ref, yf_ref, gm_ref, gf_ref, bg_ref, wa_ref, wb_ref, wo_ref,
                  gain_ref, wg_ref, wu_ref, wd_ref, o_ref, t_ref):
    mixed = (jax.nn.sigmoid(gm_ref[...] + bg_ref[0:1, :]) * _dot(ym_ref[...], wa_ref[...])
             + jax.nn.sigmoid(gf_ref[...] + bg_ref[1:2, :]) * _dot(yf_ref[...], wb_ref[...]))
    x = x_ref[...] + _dot(_bf16(mixed), wo_ref[...])
    o_ref[...] = _swiglu_residual(x, gain_ref[...], wg_ref, wu_ref, wd_ref, t_ref)


def _merge_call(x, ym, yf, gm, gf, bg, wa, wb, wo, gain, wg, wu, wd):
    n = x.shape[0]
    tile = lambda w: pl.BlockSpec((TOKEN_TILE, w), lambda i: (i, 0))
    consts = (bg, wa, wb, wo, gain, wg, wu, wd)
    return pl.pallas_call(
        _merge_kernel,
        grid=(n // TOKEN_TILE,),
        in_specs=[tile(D_MODEL), tile(V_WIDTH), tile(V_WIDTH), tile(D_MODEL), tile(D_MODEL)]
        + [_const_spec(c.shape) for c in consts],
        out_specs=tile(D_MODEL),
        out_shape=jax.ShapeDtypeStruct((n, D_MODEL), jnp.float32),
        scratch_shapes=[pltpu.VMEM((TOKEN_TILE, FFN_HIDDEN), jnp.bfloat16)],
        compiler_params=pltpu.CompilerParams(
            dimension_semantics=("arbitrary",), vmem_limit_bytes=VMEM_LIMIT),
        name="merge_ffn2",
    )(x, ym, yf, gm, gf, *consts)


def _pad_heads(w, width):
    rows = w.shape[0]
    w = w.reshape(rows, HEADS, -1)
    return jnp.pad(w, ((0, 0), (0, 0), (0, width - w.shape[2]))).reshape(rows, HEADS * width)


def _head_lanes(vec, lane0=0):
    tile = jnp.zeros((HEAD_PAD,), jnp.float32).at[lane0:lane0 + vec.shape[0]].set(vec)
    return jnp.tile(tile, HEADS)[None, :]


def _proj_constants(mix_norm, w_in, wfox, wgate, q_lat_norm, w_qb, kv_lat_norm, w_kvb, q_nope_gain,
                    q_rope_gain, k_nope_gain, k_rope_gain, fox_q_gain, fox_k_gain, fox_b_f):
    f32 = jnp.float32

    def zero_cols(n):
        return jnp.zeros((D_MODEL, n), w_in.dtype)

    wlat = _bf16(jnp.concatenate([
        w_in[:, SRC_QLAT:SRC_KVLAT], zero_cols(Q_LAT_PAD - Q_LORA_RANK),
        w_in[:, SRC_KVLAT:SRC_KR],
        w_in[:, SRC_FLOGIT:SRC_GM], zero_cols(ROPE_LANE0 - HEADS),
        w_in[:, SRC_KR:SRC_FQ], zero_cols(LANES - ROPE_LANE0 - MLA_ROPE_DIM),
    ], axis=1))
    assert wlat.shape[1] == PROJ_FQ and wfox.shape[1] == PROJ_GM - PROJ_FQ
    assert wgate.shape[1] == PROJ_WIDTH - PROJ_GM

    qk_dim = MLA_NOPE_DIM + MLA_ROPE_DIM
    wqb = _bf16(jnp.pad(_pad_heads(w_qb, HEAD_PAD), ((0, Q_LAT_PAD - Q_LORA_RANK), (0, 0))))
    w_kvb = w_kvb.reshape(KV_LORA_RANK, HEADS, MLA_NOPE_DIM + MLA_V_DIM)
    wkvb = _bf16(jnp.concatenate([
        w_kvb[:, :, :MLA_NOPE_DIM].reshape(KV_LORA_RANK, V_WIDTH),
        w_kvb[:, :, MLA_NOPE_DIM:].reshape(KV_LORA_RANK, V_WIDTH)], axis=1))
    vtail = np.zeros((VT_ROWS - MLA_V_DIM, LANES), np.float32)
    vtail[VT_ONES_ROW - MLA_V_DIM] = 1.0
    vtail = jnp.asarray(vtail)

    r = np.arange(PROJ_ROWS)
    tri = jnp.asarray(r[:, None] >= r[None, :], jnp.bfloat16)
    l256 = np.arange(256)
    same_head = (l256[:, None] // HEAD_PAD) == (l256[None, :] // HEAD_PAD)
    in_nope = (l256 % HEAD_PAD) < MLA_NOPE_DIM
    in_rope = ((l256 % HEAD_PAD) >= ROPE_LANE0) & ((l256 % HEAD_PAD) < ROPE_LANE0 + MLA_ROPE_DIM)
    seg_q = same_head & ((in_nope[:, None] & in_nope[None, :]) | (in_rope[:, None] & in_rope[None, :]))
    seg_q = jnp.asarray(seg_q, jnp.bfloat16)
    seg_64 = jnp.asarray((l256[:, None] // FOX_HEAD_DIM) == (l256[None, :] // FOX_HEAD_DIM),
                         jnp.bfloat16)
    ek = np.zeros((LANES, HP_WIDTH), np.float32)
    onek = np.zeros((1, HP_WIDTH), np.float32)
    for hd in range(HEADS):
        for piece in range(3):
            ek[piece * HEADS + hd, hd * HEAD_PAD + BIAS_LANE0 + 3 + piece] = -1.0
            onek[0, hd * HEAD_PAD + BIAS_LANE0 + piece] = 1.0
    ek = jnp.asarray(ek, jnp.bfloat16)

    mla_scale = float(qk_dim) ** -0.5 * LOG2E
    fox_scale = float(FOX_HEAD_DIM) ** -0.5 * LOG2E
    q_gain = _head_lanes(jnp.concatenate([q_nope_gain, q_rope_gain]) * mla_scale)
    icnt = np.ones((HEAD_PAD,), np.float32)
    icnt[:MLA_NOPE_DIM] = 1.0 / MLA_NOPE_DIM
    icnt[ROPE_LANE0:ROPE_LANE0 + MLA_ROPE_DIM] = 1.0 / MLA_ROPE_DIM
    q_icnt = jnp.asarray(np.tile(icnt, HEADS)[None, :])
    kn_gain = jnp.tile(k_nope_gain, HEADS)[None, :]
    kr_gain = jnp.zeros((1, LANES), f32).at[0, ROPE_LANE0:ROPE_LANE0 + MLA_ROPE_DIM].set(k_rope_gain)
    fq_gain = jnp.tile(fox_q_gain * fox_scale, HEADS)[None, :]
    fk_gain = jnp.tile(fox_k_gain, HEADS)[None, :]
    bf = jnp.zeros((1, LANES), f32).at[0, :HEADS].set(fox_b_f)
    inv_freq = ROPE_THETA ** (-jnp.arange(ROPE_HALF, dtype=f32) / ROPE_HALF)
    freq = jnp.broadcast_to(inv_freq[:, None], (ROPE_HALF, LANES))
    qlat_gain = jnp.pad(q_lat_norm, (0, Q_LAT_PAD - Q_LORA_RANK))[None, :]

    amax = lambda g: jnp.max(jnp.abs(g))
    bound_fox = (FOX_HEAD_DIM * amax(fox_q_gain) * amax(fox_k_gain) * fox_scale) * BOUND_SLACK
    q_sq = MLA_NOPE_DIM * amax(q_nope_gain) ** 2 + MLA_ROPE_DIM * amax(q_rope_gain) ** 2
    k_sq = MLA_NOPE_DIM * amax(k_nope_gain) ** 2 + MLA_ROPE_DIM * amax(k_rope_gain) ** 2
    bound_mla = jnp.sqrt(q_sq * k_sq) * mla_scale * BOUND_SLACK
    shift_lane = np.zeros((1, LANES), np.float32)
    shift_lane[0, MLA_SHIFT_LANE] = 1.0
    kone = jnp.asarray(shift_lane)
    qshift = kone * -(bound_mla - SHIFT_MARGIN)
    fox_shift_lanes = np.zeros((1, HP_WIDTH), np.float32)
    fox_shift_lanes[0, FOX_SHIFT_LANE::HEAD_PAD] = 1.0
    onek = jnp.asarray(onek + fox_shift_lanes)
    qtail = jnp.concatenate([jnp.ones((3, LANES), f32),
                             jnp.full((1, LANES), -(bound_fox - SHIFT_MARGIN), f32),
                             jnp.zeros((1, LANES), f32)], axis=0)

    consts = (mix_norm[None, :], wlat, wfox, wgate, wqb, wkvb, vtail, tri, seg_q, seg_64, ek,
              qlat_gain, kv_lat_norm[None, :], q_gain, q_icnt, kn_gain, kr_gain, fq_gain, fk_gain,
              bf, freq, qtail, onek, qshift, kone)
    return consts, bound_mla <= SHIFT_BOUND_LIMIT, bound_fox <= SHIFT_BOUND_LIMIT


def kernel(x, positions, ffn1_norm, ffn1_w_gate, ffn1_w_up, ffn1_w_down, mix_norm, w_in, mla_q_lat_norm, mla_w_qb, mla_kv_lat_norm, mla_w_kvb, mla_q_nope_gain, mla_q_rope_gain, mla_k_nope_gain, mla_k_rope_gain, fox_q_gain, fox_k_gain, fox_b_f, w_branch_mla, w_branch_fox, b_gate, w_o, ffn2_norm, ffn2_w_gate, ffn2_w_up, ffn2_w_down):
    b, s, d = x.shape
    n = b * s
    pos = positions.reshape(b, 1, s)
    xt = x.reshape(n, d)
    for l in range(ffn1_norm.shape[0]):
        xt, wfox, wgate = _ffn_call(xt, ffn1_norm[l][None, :], _bf16(ffn1_w_gate[l]),
                                    _bf16(ffn1_w_up[l]), _bf16(ffn1_w_down[l]), w_in[l])
        consts, mla_shift_safe, fox_shift_safe = _proj_constants(
            mix_norm[l], w_in[l], wfox, wgate, mla_q_lat_norm[l], mla_w_qb[l], mla_kv_lat_norm[l], mla_w_kvb[l],
            mla_q_nope_gain[l], mla_q_rope_gain[l], mla_k_nope_gain[l], mla_k_rope_gain[l],
            fox_q_gain[l], fox_k_gain[l], fox_b_f[l])
        qm, km, vm, qf, kf, vf, gm, gf = _proj_call(xt.reshape(b, s, d), pos, consts)
        ym = _attention(qm, km, vm, mla_shift_safe, "attn_mla")
        yf = _attention(qf, kf, vf, fox_shift_safe, "attn_fox")
        xt = _merge_call(
            xt, ym.reshape(n, V_WIDTH), yf.reshape(n, V_WIDTH), gm.reshape(n, d), gf.reshape(n, d),
            b_gate[l], _bf16(w_branch_mla[l]), _bf16(w_branch_fox[l]), _bf16(w_o[l]),
            ffn2_norm[l][None, :], _bf16(ffn2_w_gate[l]), _bf16(ffn2_w_up[l]), _bf16(ffn2_w_down[l]))
    return xt.reshape(b, s, d)
```

```python
import functools

import numpy as np
import jax
import jax.numpy as jnp
from jax import lax
from jax.experimental import pallas as pl
from jax.experimental.pallas import tpu as pltpu

D_MODEL = 1024
FFN_HIDDEN = 2816
FFN_RESIDUAL_WEIGHT = 0.5
HEADS = 8
MLA_NOPE_DIM = 64
MLA_ROPE_DIM = 32
MLA_V_DIM = 64
Q_LORA_RANK = 192
KV_LORA_RANK = 128
ROPE_THETA = 10000.0
FOX_HEAD_DIM = 64
RMS_EPS = 1e-6

LANES = 128
HEAD_PAD = LANES
HP_WIDTH = HEADS * HEAD_PAD
V_WIDTH = HEADS * MLA_V_DIM
Q_LAT_PAD = 256
ROPE_HALF = MLA_ROPE_DIM // 2
ROPE_LANE0 = MLA_NOPE_DIM
BIAS_LANE0 = FOX_HEAD_DIM
FOX_SHIFT_LANE = BIAS_LANE0 + 6
MLA_SHIFT_LANE = ROPE_LANE0 + MLA_ROPE_DIM
SHIFT_MARGIN = 40.0
SHIFT_BOUND_LIMIT = 68.0
BOUND_SLACK = 1.05

TOKEN_TILE = 512
PROJ_ROWS = 256
FFN_TILE = 1024
HIDDEN_CHUNK = 256
GATE_CHUNK = 256
ATTN_Q_TILE = 512
ATTN_KV_TILE = 512
ATTN_KV_SUB = 256
QK_LOOKAHEAD = 2
MASK_VALUE = -1e30
VMEM_LIMIT = 56 * 1024 * 1024

PROJ_QLAT = 0
PROJ_KVLAT = PROJ_QLAT + Q_LAT_PAD
PROJ_KR = PROJ_KVLAT + KV_LORA_RANK
PROJ_FQ = PROJ_KR + LANES
PROJ_FK = PROJ_FQ + V_WIDTH
PROJ_FV = PROJ_FK + V_WIDTH
PROJ_GM = PROJ_FV + V_WIDTH
PROJ_GF = PROJ_GM + D_MODEL
PROJ_WIDTH = PROJ_GF + D_MODEL

VT_ROWS = 80
VT_ONES_ROW = MLA_V_DIM
VT_WIDTH = HEADS * VT_ROWS
LOG2E = 1.4426950408889634


def _bf16(x):
    return x.astype(jnp.bfloat16)


def _dot(a, b):
    return jnp.dot(a, b, preferred_element_type=jnp.float32)


def _rms_scale(x, n):
    return lax.rsqrt(jnp.sum(x * x, axis=-1, keepdims=True) * (1.0 / n) + RMS_EPS)


def _split2(x):
    hi = _bf16(x)
    return hi, _bf16(x - hi.astype(jnp.float32))


def _split3(x):
    hi = _bf16(x)
    mid, lo = _split2(x - hi.astype(jnp.float32))
    return hi, mid, lo


def _segment_sumsq(x, seg_ref):
    seg = seg_ref[...]
    outs = []
    for c in range(x.shape[1] // 256):
        xs = x[:, c * 256:(c + 1) * 256]
        outs.append(_dot(_bf16(xs * xs), seg))
    return jnp.concatenate(outs, axis=1)


def _swiglu_residual(x, gain, wg_ref, wu_ref, wd_ref, t_ref):
    h = _bf16(x * _rms_scale(x, D_MODEL) * gain)
    for c in range(FFN_HIDDEN // HIDDEN_CHUNK):
        cols = slice(c * HIDDEN_CHUNK, (c + 1) * HIDDEN_CHUNK)
        a = _dot(h, wg_ref[:, cols])
        b = _dot(h, wu_ref[:, cols])
        t_ref[:, cols] = _bf16(a * jax.nn.sigmoid(a) * b)
    return x + FFN_RESIDUAL_WEIGHT * _dot(t_ref[...], wd_ref[...])


def _ffn_kernel(x_ref, gain_ref, wg_ref, wu_ref, wd_ref, o_ref, t_ref):
    o_ref[...] = _swiglu_residual(x_ref[...], gain_ref[...], wg_ref, wu_ref, wd_ref, t_ref)


def _const_spec(shape):
    return pl.BlockSpec(shape, lambda *_: (0,) * len(shape), pipeline_mode=pl.Buffered(1))


def _ffn_call(x, gain, wg, wu, wd):
    n = x.shape[0]
    return pl.pallas_call(
        _ffn_kernel,
        grid=(n // FFN_TILE,),
        in_specs=[
            pl.BlockSpec((FFN_TILE, D_MODEL), lambda i: (i, 0)),
            _const_spec((1, D_MODEL)),
            _const_spec((D_MODEL, FFN_HIDDEN)),
            _const_spec((D_MODEL, FFN_HIDDEN)),
            _const_spec((FFN_HIDDEN, D_MODEL)),
        ],
        out_specs=pl.BlockSpec((FFN_TILE, D_MODEL), lambda i: (i, 0)),
        out_shape=jax.ShapeDtypeStruct((n, D_MODEL), jnp.float32),
        scratch_shapes=[pltpu.VMEM((FFN_TILE, FFN_HIDDEN), jnp.bfloat16)],
        compiler_params=pltpu.CompilerParams(
            dimension_semantics=("arbitrary",), vmem_limit_bytes=VMEM_LIMIT),
        name="ffn1",
    )(x, gain, wg, wu, wd)


def _proj_kernel(x_ref, posrow_ref, gain_ref, wlat_ref, wfox_ref, wgate_ref, wqb_ref, wkvb_ref,
                 vtail_ref,
                 tri_ref, segq_ref, seg64_ref, ek_ref,
                 qlat_gain_ref, kvlat_gain_ref, q_gain_ref, q_icnt_ref, kn_gain_ref,
                 kr_gain_ref, fq_gain_ref, fk_gain_ref, bf_ref, freq_ref, qtail_ref, onek_ref,
                 qshift_ref, kone_ref,
                 qm_ref, km_ref, vm_ref, qf_ref, kf_ref, vf_ref, gm_ref, gf_ref,
                 carry_ref):
    lane = lax.broadcasted_iota(jnp.int32, (1, LANES), 1)
    rope_lanes = (lane >= ROPE_LANE0) & (lane < ROPE_LANE0 + MLA_ROPE_DIM)
    n_rows = PROJ_ROWS

    @pl.when(pl.program_id(1) == 0)
    def _():
        carry_ref[...] = jnp.zeros_like(carry_ref)

    def rows_program(r0):
        rows = slice(r0, r0 + n_rows)
        x = x_ref[0, rows, :]
        h = _bf16(x * _rms_scale(x, D_MODEL) * gain_ref[...])

        def proj(lo, width):
            for ref, base in ((wlat_ref, PROJ_QLAT), (wfox_ref, PROJ_FQ), (wgate_ref, PROJ_GM)):
                if base <= lo and lo + width <= base + ref.shape[1]:
                    return _dot(h, ref[:, lo - base:lo - base + width])
            raise ValueError("projection columns straddle two weight pieces")

        gate_chunks = [(ref, col0, c * GATE_CHUNK)
                       for ref, col0 in ((gm_ref, PROJ_GM), (gf_ref, PROJ_GF))
                       for c in range(D_MODEL // GATE_CHUNK)]

        def gates(n):
            for _ in range(min(n, len(gate_chunks))):
                ref, col0, c0 = gate_chunks.pop(0)
                ref[0, rows, c0:c0 + GATE_CHUNK] = proj(col0 + c0, GATE_CHUNK)

        gates(2)
        yield

        ang = (jnp.concatenate([freq_ref[...]] * (n_rows // LANES), axis=1)
               * posrow_ref[0, :, rows].astype(jnp.float32))
        cos16, sin16 = jnp.cos(ang), jnp.sin(ang)
        fill = lambda n, v: jnp.full((n, n_rows), v, jnp.float32)
        pad_hi = LANES - ROPE_LANE0 - MLA_ROPE_DIM
        cos_t = jnp.concatenate([fill(ROPE_LANE0, 1.0), cos16, cos16, fill(pad_hi, 1.0)], axis=0).T
        sin_hi = jnp.concatenate([fill(ROPE_LANE0 + ROPE_HALF, 0.0), sin16, fill(pad_hi, 0.0)],
                                 axis=0).T
        sin_lo = jnp.concatenate([fill(ROPE_LANE0, 0.0), -sin16, fill(ROPE_HALF + pad_hi, 0.0)],
                                 axis=0).T

        def rotary(t):
            return (t * cos_t + pltpu.roll(t, ROPE_HALF, 1) * sin_hi
                    + pltpu.roll(t, LANES - ROPE_HALF, 1) * sin_lo)

        def store_values_t(ref, v):
            tail = jnp.concatenate([vtail_ref[...]] * (n_rows // LANES), axis=1)
            blocks = []
            for pair in range(HEADS // 2):
                pair_t = v[:, pair * LANES:(pair + 1) * LANES].T
                blocks += [pair_t[:MLA_V_DIM], tail, pair_t[MLA_V_DIM:], tail]
            ref[0, 0, :, rows] = _bf16(jnp.concatenate(blocks, axis=0))

        def head_normed(t, g_ref):
            return (t * lax.rsqrt(_segment_sumsq(t, seg64_ref) * (1.0 / FOX_HEAD_DIM) + RMS_EPS)
                    * g_ref[...])

        def head_tiles(t):
            low = lane < FOX_HEAD_DIM
            tiles = []
            for pair in range(HEADS // 2):
                src = t[:, pair * LANES:(pair + 1) * LANES]
                tiles.append(jnp.where(low, src, 0.0))
                tiles.append(jnp.where(low, pltpu.roll(src, LANES - FOX_HEAD_DIM, 1), 0.0))
            return tiles

        gates(1)
        q_lat = proj(PROJ_QLAT, Q_LAT_PAD)
        yield
        q_lat = _bf16(q_lat * _rms_scale(q_lat, Q_LORA_RANK) * qlat_gain_ref[...])
        q = _dot(q_lat, wqb_ref[...])
        gates(1)
        yield
        q = q * lax.rsqrt(_segment_sumsq(q, segq_ref) * q_icnt_ref[...] + RMS_EPS) * q_gain_ref[...]
        yield
        for hd in range(HEADS):
            if hd % 2 == 0:
                gates(1)
            cols = slice(hd * HEAD_PAD, (hd + 1) * HEAD_PAD)
            q_t = (q[:, cols] + qshift_ref[...]).T
            x1 = q_t[ROPE_LANE0:ROPE_LANE0 + ROPE_HALF]
            x2 = q_t[ROPE_LANE0 + ROPE_HALF:ROPE_LANE0 + MLA_ROPE_DIM]
            qm_ref[0, cols, rows] = _bf16(jnp.concatenate(
                [q_t[:ROPE_LANE0], x1 * cos16 - x2 * sin16, x2 * cos16 + x1 * sin16,
                 q_t[ROPE_LANE0 + MLA_ROPE_DIM:]], axis=0))
            if hd % 2 == 1:
                yield

        kv_lat = proj(PROJ_KVLAT, KV_LORA_RANK)
        kv_lat = _bf16(kv_lat * _rms_scale(kv_lat, KV_LORA_RANK) * kvlat_gain_ref[...])
        gates(1)
        kv = _dot(kv_lat, wkvb_ref[...])
        yield
        k_nope = head_tiles(head_normed(kv[:, :V_WIDTH], kn_gain_ref))
        store_values_t(vm_ref, kv[:, V_WIDTH:])
        yield

        kr_grp = proj(PROJ_KR, LANES)
        kr = jnp.where(rope_lanes, kr_grp, 0.0)
        kr = rotary(kr * _rms_scale(kr, MLA_ROPE_DIM) * kr_gain_ref[...]) + kone_ref[...]
        gates(1)
        for hd in range(HEADS):
            cols = slice(hd * HEAD_PAD, (hd + 1) * HEAD_PAD)
            km_ref[0, rows, cols] = _bf16(k_nope[hd] + kr)
        gates(1)
        yield

        log_f = jnp.where(lane < HEADS, jax.nn.log_sigmoid(kr_grp + bf_ref[...]), 0.0)
        tri = tri_ref[...]
        f_hi, f_lo = _split2(log_f)
        c = _dot(tri, f_hi) + _dot(tri, f_lo) + carry_ref[0:1, :]
        carry_ref[0:1, :] = c[n_rows - 1:n_rows, :]
        yield
        c2 = c * LOG2E
        c_rep = c2 + pltpu.roll(c2, HEADS, 1) + pltpu.roll(c2, 2 * HEADS, 1)
        c_hi, c_mid, c_lo = _split3(c_rep)
        c_parts = jnp.where(lane < HEADS, c_hi, jnp.where(lane < 2 * HEADS, c_mid, c_lo))

        gates(1)
        t = proj(PROJ_FQ, V_WIDTH)
        yield
        t = head_normed(t, fq_gain_ref)
        c_t = c_parts.T
        tail = jnp.concatenate([qtail_ref[...]] * (n_rows // LANES), axis=1)
        zero_rows = jnp.zeros((HEAD_PAD - FOX_HEAD_DIM - 8, n_rows), jnp.float32)
        gates(1)
        yield
        for pair in range(HEADS // 2):
            pair_t = t[:, pair * LANES:(pair + 1) * LANES].T
            for e in range(2):
                hd = 2 * pair + e
                cols = slice(hd * HEAD_PAD, (hd + 1) * HEAD_PAD)
                pieces = [c_t[piece * HEADS + hd:piece * HEADS + hd + 1] for piece in range(3)]
                qf_ref[0, cols, rows] = _bf16(jnp.concatenate(
                    [pair_t[e * FOX_HEAD_DIM:(e + 1) * FOX_HEAD_DIM]] + pieces + [tail, zero_rows],
                    axis=0))
        yield

        gates(1)
        t = proj(PROJ_FK, V_WIDTH)
        yield
        tiles = head_tiles(head_normed(t, fk_gain_ref))
        gates(1)
        bias = _dot(c_parts, ek_ref[...]) + onek_ref[...]
        yield
        for hd in range(HEADS):
            cols = slice(hd * HEAD_PAD, (hd + 1) * HEAD_PAD)
            kf_ref[0, rows, cols] = _bf16(tiles[hd] + bias[:, cols])
        yield
        store_values_t(vf_ref, proj(PROJ_FV, V_WIDTH))
        gates(len(gate_chunks))

    programs = [rows_program(r0) for r0 in range(0, TOKEN_TILE, n_rows)]
    while programs:
        for program in list(programs):
            if next(program, "done") == "done":
                programs.remove(program)


def _proj_call(x, pos, consts):
    b, s, _ = x.shape
    bf16, f32 = jnp.bfloat16, jnp.float32
    n_tiles = s // TOKEN_TILE
    tile = lambda w: pl.BlockSpec((1, TOKEN_TILE, w), lambda bi, i: (bi, i, 0))
    tile_t = pl.BlockSpec((1, 1, VT_WIDTH, TOKEN_TILE), lambda bi, i: (bi, i, 0, 0))
    rows = lambda w, d: jax.ShapeDtypeStruct((b, s, w), d)
    rows_t = jax.ShapeDtypeStruct((b, n_tiles, VT_WIDTH, TOKEN_TILE), bf16)
    tile_q = pl.BlockSpec((1, HP_WIDTH, TOKEN_TILE), lambda bi, i: (bi, 0, i))
    rows_q = jax.ShapeDtypeStruct((b, HP_WIDTH, s), bf16)
    return pl.pallas_call(
        _proj_kernel,
        grid=(b, n_tiles),
        in_specs=[tile(D_MODEL), pl.BlockSpec((1, 1, TOKEN_TILE), lambda bi, i: (bi, 0, i))]
        + [_const_spec(c.shape) for c in consts],
        out_specs=[tile_q, tile(HP_WIDTH), tile_t, tile_q, tile(HP_WIDTH), tile_t,
                   tile(D_MODEL), tile(D_MODEL)],
        out_shape=[rows_q, rows(HP_WIDTH, bf16), rows_t, rows_q, rows(HP_WIDTH, bf16), rows_t,
                   rows(D_MODEL, f32), rows(D_MODEL, f32)],
        scratch_shapes=[pltpu.VMEM((8, LANES), jnp.float32)],
        compiler_params=pltpu.CompilerParams(
            dimension_semantics=("arbitrary", "arbitrary"), vmem_limit_bytes=VMEM_LIMIT),
        name="proj",
    )(x, pos, *consts)


def _attn_kernel(qt_ref, k_ref, vt_ref, *refs, running_max, n_cast):
    cast_in, o_ref, cast_out = refs[:n_cast], refs[n_cast], refs[n_cast + 1:2 * n_cast + 1]
    m_ref, acc_ref, ahead_ref = refs[2 * n_cast + 1:]
    for src, dst in zip(cast_in, cast_out):
        dst[...] = _bf16(src[...])

    qi = pl.program_id(1)
    tq, tk, ks = ATTN_Q_TILE, ATTN_KV_TILE, ATTN_KV_SUB
    units = [(hd, sub) for sub in range(tk // ks) for hd in range(HEADS)]

    def logits(t, unit, diagonal):
        hd, sub = unit
        hcols = slice(hd * HEAD_PAD, (hd + 1) * HEAD_PAD)
        q_lo = sub * ks if diagonal else 0
        off = pl.multiple_of(t * tk + sub * ks, ks)
        return _dot(k_ref[0, pl.ds(off, ks), hcols], qt_ref[0, hcols, q_lo:])

    def step(t, diagonal, next_t):
        pending = [ahead_ref[j] for j in range(QK_LOOKAHEAD)]
        for i, (hd, sub) in enumerate(units):
            s = pending.pop(0)
            j = i + QK_LOOKAHEAD
            if j < len(units):
                pending.append(logits(t, units[j], diagonal))
            elif next_t is not None:
                ahead_ref[j - len(units)] = logits(next_t, units[j - len(units)], False)
            q_lo = sub * ks if diagonal else 0
            if diagonal:
                key = lax.broadcasted_iota(jnp.int32, s.shape, 0)
                query = lax.broadcasted_iota(jnp.int32, s.shape, 1)
                s = jnp.where(key <= query, s, MASK_VALUE)
            vt = vt_ref[0, t, hd * VT_ROWS:(hd + 1) * VT_ROWS, sub * ks:(sub + 1) * ks]
            if running_max:
                m_prev = m_ref[hd, :, q_lo:]
                m_new = jnp.maximum(m_prev, jnp.max(s, axis=0, keepdims=True))
                alpha = jnp.exp2(m_prev - m_new)
                p = _bf16(jnp.exp2(s - m_new))
                acc_ref[hd, :, q_lo:] = alpha * acc_ref[hd, :, q_lo:] + _dot(vt, p)
                m_ref[hd, :, q_lo:] = m_new
            else:
                acc_ref[hd, :, q_lo:] += _dot(vt, _bf16(jnp.exp2(s)))

    if running_max:
        m_ref[...] = jnp.full(m_ref.shape, MASK_VALUE, jnp.float32)
    acc_ref[...] = jnp.zeros(acc_ref.shape, jnp.float32)
    for j in range(QK_LOOKAHEAD):
        ahead_ref[j] = logits(0, units[j], False)

    def body(t, carry):
        step(t, False, t + 1)
        return carry

    lax.fori_loop(0, qi, body, 0)
    step(qi, True, None)

    for pair in range(HEADS // 2):
        y_t = []
        for hd in (2 * pair, 2 * pair + 1):
            acc = acc_ref[hd]
            y_t.append(acc[:MLA_V_DIM] / acc[VT_ONES_ROW:VT_ONES_ROW + 1])
        y = jnp.concatenate(y_t, axis=0).T
        o_ref[0, :, pair * LANES:(pair + 1) * LANES] = y.astype(o_ref.dtype)


def _attn_call(qt, k, vt, *weights, running_max, name):
    b, s, _ = k.shape
    assert ATTN_Q_TILE == ATTN_KV_TILE == TOKEN_TILE
    n_q = s // ATTN_Q_TILE
    steps = b * n_q

    def slab_spec(w):
        rows = w.shape[0]
        n_blocks = steps if (rows // steps) % 16 == 0 and rows % steps == 0 else steps // 2
        assert rows % n_blocks == 0 and (rows // n_blocks) % 16 == 0 and steps % n_blocks == 0
        repeat = steps // n_blocks
        return pl.BlockSpec((rows // n_blocks, w.shape[1]),
                            lambda bi, i: ((bi * n_q + i) // repeat, 0))

    slabs = [slab_spec(w) for w in weights]
    return pl.pallas_call(
        functools.partial(_attn_kernel, running_max=running_max, n_cast=len(weights)),
        grid=(b, n_q),
        in_specs=[
            pl.BlockSpec((1, HP_WIDTH, ATTN_Q_TILE), lambda bi, i: (bi, 0, i)),
            pl.BlockSpec((1, s, HP_WIDTH), lambda bi, i: (bi, 0, 0)),
            pl.BlockSpec((1,) + vt.shape[1:], lambda bi, i: (bi, 0, 0, 0)),
        ] + slabs,
        out_specs=[pl.BlockSpec((1, ATTN_Q_TILE, V_WIDTH), lambda bi, i: (bi, i, 0))] + slabs,
        out_shape=[jax.ShapeDtypeStruct((b, s, V_WIDTH), jnp.bfloat16)]
        + [jax.ShapeDtypeStruct(w.shape, jnp.bfloat16) for w in weights],
        scratch_shapes=[
            pltpu.VMEM((HEADS, 1, ATTN_Q_TILE), jnp.float32),
            pltpu.VMEM((HEADS, VT_ROWS, ATTN_Q_TILE), jnp.float32),
            pltpu.VMEM((QK_LOOKAHEAD, ATTN_KV_SUB, ATTN_Q_TILE), jnp.float32),
        ],
        compiler_params=pltpu.CompilerParams(
            dimension_semantics=("arbitrary", "arbitrary"), vmem_limit_bytes=VMEM_LIMIT),
        name=name,
    )(qt, k, vt, *weights)


def _attention(q, k, vt, weights, shift_is_safe, name):
    return lax.cond(
        shift_is_safe,
        functools.partial(_attn_call, running_max=False, name=name + "_shifted"),
        functools.partial(_attn_call, running_max=True, name=name),
        q, k, vt, *weights)


def _merge_kernel(x_ref, ym_ref, yf_ref, gm_ref, gf_ref, bg_ref, wa_ref, wb_ref, wo_ref,
                  gain_ref, wg_ref, wu_ref, wd_ref, o_ref, t_ref):
    mixed = (jax.nn.sigmoid(gm_ref[...] + bg_ref[0:1, :]) * _dot(ym_ref[...], wa_ref[...])
             + jax.nn.sigmoid(gf_ref[...] + bg_ref[1:2, :]) * _dot(yf_ref[...], wb_ref[...]))
    x = x_ref[...] + _dot(_bf16(mixed), wo_ref[...])
    o_ref[...] = _swiglu_residual(x, gain_ref[...], wg_ref, wu_ref, wd_ref, t_ref)


def _merge_call(x, ym, yf, gm, gf, bg, wa, wb, wo, gain, wg, wu, wd):
    n = x.shape[0]
    tile = lambda w: pl.BlockSpec((TOKEN_TILE, w), lambda i: (i, 0))
    consts = (bg, wa, wb, wo, gain, wg, wu, wd)
    return pl.pallas_call(
        _merge_kernel,
        grid=(n // TOKEN_TILE,),
        in_specs=[tile(D_MODEL), tile(V_WIDTH), tile(V_WIDTH), tile(D_MODEL), tile(D_MODEL)]
        + [_const_spec(c.shape) for c in consts],
        out_specs=tile(D_MODEL),
        out_shape=jax.ShapeDtypeStruct((n, D_MODEL), jnp.float32),
        scratch_shapes=[pltpu.VMEM((TOKEN_TILE, FFN_HIDDEN), jnp.bfloat16)],
        compiler_params=pltpu.CompilerParams(
            dimension_semantics=("arbitrary",), vmem_limit_bytes=VMEM_LIMIT),
        name="merge_ffn2",
    )(x, ym, yf, gm, gf, *consts)


def _pad_heads(w, width):
    rows = w.shape[0]
    w = w.reshape(rows, HEADS, -1)
    return jnp.pad(w, ((0, 0), (0, 0), (0, width - w.shape[2]))).reshape(rows, HEADS * width)


def _head_lanes(vec, lane0=0):
    tile = jnp.zeros((HEAD_PAD,), jnp.float32).at[lane0:lane0 + vec.shape[0]].set(vec)
    return jnp.tile(tile, HEADS)[None, :]


def _proj_constants(mix_norm, w_in, q_lat_norm, w_qb, kv_lat_norm, w_kvb, q_nope_gain, q_rope_gain,
                    k_nope_gain, k_rope_gain, fox_q_gain, fox_k_gain, fox_b_f):
    f32 = jnp.float32
    o_qlat, o_kvlat = 0, Q_LORA_RANK
    o_kr = o_kvlat + KV_LORA_RANK
    o_fq = o_kr + MLA_ROPE_DIM
    o_fk = o_fq + V_WIDTH
    o_fv = o_fk + V_WIDTH
    o_fl = o_fv + V_WIDTH
    o_gm = o_fl + HEADS
    o_gf = o_gm + D_MODEL

    def zero_cols(n):
        return jnp.zeros((D_MODEL, n), w_in.dtype)

    wlat = _bf16(jnp.concatenate([
        w_in[:, o_qlat:o_kvlat], zero_cols(Q_LAT_PAD - Q_LORA_RANK),
        w_in[:, o_kvlat:o_kr],
        w_in[:, o_fl:o_gm], zero_cols(ROPE_LANE0 - HEADS),
        w_in[:, o_kr:o_fq], zero_cols(LANES - ROPE_LANE0 - MLA_ROPE_DIM),
    ], axis=1))
    wfox = _bf16(w_in[:, o_fq:o_fl])
    wgate = _bf16(w_in[:, o_gm:])
    assert wlat.shape[1] == PROJ_FQ and wfox.shape[1] == PROJ_GM - PROJ_FQ
    assert wgate.shape[1] == PROJ_WIDTH - PROJ_GM

    qk_dim = MLA_NOPE_DIM + MLA_ROPE_DIM
    wqb = _bf16(jnp.pad(_pad_heads(w_qb, HEAD_PAD), ((0, Q_LAT_PAD - Q_LORA_RANK), (0, 0))))
    w_kvb = w_kvb.reshape(KV_LORA_RANK, HEADS, MLA_NOPE_DIM + MLA_V_DIM)
    wkvb = _bf16(jnp.concatenate([
        w_kvb[:, :, :MLA_NOPE_DIM].reshape(KV_LORA_RANK, V_WIDTH),
        w_kvb[:, :, MLA_NOPE_DIM:].reshape(KV_LORA_RANK, V_WIDTH)], axis=1))
    vtail = np.zeros((VT_ROWS - MLA_V_DIM, LANES), np.float32)
    vtail[VT_ONES_ROW - MLA_V_DIM] = 1.0
    vtail = jnp.asarray(vtail)

    r = np.arange(PROJ_ROWS)
    tri = jnp.asarray(r[:, None] >= r[None, :], jnp.bfloat16)
    l256 = np.arange(256)
    same_head = (l256[:, None] // HEAD_PAD) == (l256[None, :] // HEAD_PAD)
    in_nope = (l256 % HEAD_PAD) < MLA_NOPE_DIM
    in_rope = ((l256 % HEAD_PAD) >= ROPE_LANE0) & ((l256 % HEAD_PAD) < ROPE_LANE0 + MLA_ROPE_DIM)
    seg_q = same_head & ((in_nope[:, None] & in_nope[None, :]) | (in_rope[:, None] & in_rope[None, :]))
    seg_q = jnp.asarray(seg_q, jnp.bfloat16)
    seg_64 = jnp.asarray((l256[:, None] // FOX_HEAD_DIM) == (l256[None, :] // FOX_HEAD_DIM),
                         jnp.bfloat16)
    ek = np.zeros((LANES, HP_WIDTH), np.float32)
    onek = np.zeros((1, HP_WIDTH), np.float32)
    for hd in range(HEADS):
        for piece in range(3):
            ek[piece * HEADS + hd, hd * HEAD_PAD + BIAS_LANE0 + 3 + piece] = -1.0
            onek[0, hd * HEAD_PAD + BIAS_LANE0 + piece] = 1.0
    ek = jnp.asarray(ek, jnp.bfloat16)

    mla_scale = float(qk_dim) ** -0.5 * LOG2E
    fox_scale = float(FOX_HEAD_DIM) ** -0.5 * LOG2E
    q_gain = _head_lanes(jnp.concatenate([q_nope_gain, q_rope_gain]) * mla_scale)
    icnt = np.ones((HEAD_PAD,), np.float32)
    icnt[:MLA_NOPE_DIM] = 1.0 / MLA_NOPE_DIM
    icnt[ROPE_LANE0:ROPE_LANE0 + MLA_ROPE_DIM] = 1.0 / MLA_ROPE_DIM
    q_icnt = jnp.asarray(np.tile(icnt, HEADS)[None, :])
    kn_gain = jnp.tile(k_nope_gain, HEADS)[None, :]
    kr_gain = jnp.zeros((1, LANES), f32).at[0, ROPE_LANE0:ROPE_LANE0 + MLA_ROPE_DIM].set(k_rope_gain)
    fq_gain = jnp.tile(fox_q_gain * fox_scale, HEADS)[None, :]
    fk_gain = jnp.tile(fox_k_gain, HEADS)[None, :]
    bf = jnp.zeros((1, LANES), f32).at[0, :HEADS].set(fox_b_f)
    inv_freq = ROPE_THETA ** (-jnp.arange(ROPE_HALF, dtype=f32) / ROPE_HALF)
    freq = jnp.broadcast_to(inv_freq[:, None], (ROPE_HALF, LANES))
    qlat_gain = jnp.pad(q_lat_norm, (0, Q_LAT_PAD - Q_LORA_RANK))[None, :]

    amax = lambda g: jnp.max(jnp.abs(g))
    bound_fox = (FOX_HEAD_DIM * amax(fox_q_gain) * amax(fox_k_gain) * fox_scale) * BOUND_SLACK
    q_sq = MLA_NOPE_DIM * amax(q_nope_gain) ** 2 + MLA_ROPE_DIM * amax(q_rope_gain) ** 2
    k_sq = MLA_NOPE_DIM * amax(k_nope_gain) ** 2 + MLA_ROPE_DIM * amax(k_rope_gain) ** 2
    bound_mla = jnp.sqrt(q_sq * k_sq) * mla_scale * BOUND_SLACK
    shift_lane = np.zeros((1, LANES), np.float32)
    shift_lane[0, MLA_SHIFT_LANE] = 1.0
    kone = jnp.asarray(shift_lane)
    qshift = kone * -(bound_mla - SHIFT_MARGIN)
    fox_shift_lanes = np.zeros((1, HP_WIDTH), np.float32)
    fox_shift_lanes[0, FOX_SHIFT_LANE::HEAD_PAD] = 1.0
    onek = jnp.asarray(onek + fox_shift_lanes)
    qtail = jnp.concatenate([jnp.ones((3, LANES), f32),
                             jnp.full((1, LANES), -(bound_fox - SHIFT_MARGIN), f32),
                             jnp.zeros((1, LANES), f32)], axis=0)

    consts = (mix_norm[None, :], wlat, wfox, wgate, wqb, wkvb, vtail, tri, seg_q, seg_64, ek,
              qlat_gain, kv_lat_norm[None, :], q_gain, q_icnt, kn_gain, kr_gain, fq_gain, fk_gain,
              bf, freq, qtail, onek, qshift, kone)
    return consts, bound_mla <= SHIFT_BOUND_LIMIT, bound_fox <= SHIFT_BOUND_LIMIT


def kernel(x, positions, ffn1_norm, ffn1_w_gate, ffn1_w_up, ffn1_w_down, mix_norm, w_in, mla_q_lat_norm, mla_w_qb, mla_kv_lat_norm, mla_w_kvb, mla_q_nope_gain, mla_q_rope_gain, mla_k_nope_gain, mla_k_rope_gain, fox_q_gain, fox_k_gain, fox_b_f, w_branch_mla, w_branch_fox, b_gate, w_o, ffn2_norm, ffn2_w_gate, ffn2_w_up, ffn2_w_down):
    b, s, d = x.shape
    n = b * s
    pos = positions.reshape(b, 1, s)
    xt = x.reshape(n, d)
    for l in range(ffn1_norm.shape[0]):
        xt = _ffn_call(xt, ffn1_norm[l][None, :], _bf16(ffn1_w_gate[l]), _bf16(ffn1_w_up[l]),
                       _bf16(ffn1_w_down[l]))
        consts, mla_shift_safe, fox_shift_safe = _proj_constants(
            mix_norm[l], w_in[l], mla_q_lat_norm[l], mla_w_qb[l], mla_kv_lat_norm[l], mla_w_kvb[l],
            mla_q_nope_gain[l], mla_q_rope_gain[l], mla_k_nope_gain[l], mla_k_rope_gain[l],
            fox_q_gain[l], fox_k_gain[l], fox_b_f[l])
        qm, km, vm, qf, kf, vf, gm, gf = _proj_call(xt.reshape(b, s, d), pos, consts)
        ym, wg2, wu2 = _attention(qm, km, vm, (ffn2_w_gate[l], ffn2_w_up[l]), mla_shift_safe,
                                  "attn_mla")
        yf, wd2 = _attention(qf, kf, vf, (ffn2_w_down[l],), fox_shift_safe, "attn_fox")
        xt = _merge_call(
            xt, ym.reshape(n, V_WIDTH), yf.reshape(n, V_WIDTH), gm.reshape(n, d), gf.reshape(n, d),
            b_gate[l], _bf16(w_branch_mla[l]), _bf16(w_branch_fox[l]), _bf16(w_o[l]),
            ffn2_norm[l][None, :], wg2, wu2, wd2)
    return xt.reshape(b, s, d)
```
